```python
import math
import jax, jax.numpy as jnp
from jax import lax
import numpy as np

D_MODEL = 1024
BATCH = 4
SEQ = 8192
DEPTH = 2

EPS = 1e-6
CHUNK = 64
Q_BLOCK = 128
CONV_K = 4
GLA_HEADS = 4
GLA_DK = 64
GLA_DV = 128
GLA_GATE_RANK = 16
GLA_GATE_NORM = 16.0
SSD_HEADS = 16
SSD_HEADDIM = 64
SSD_GROUPS = 2
SSD_HPG = SSD_HEADS // SSD_GROUPS
SSD_STATE = 128
SSD_INNER = SSD_HEADS * SSD_HEADDIM
SSD_CONV_DIM = SSD_INNER + 2 * SSD_GROUPS * SSD_STATE
LRU_WIDTH = D_MODEL
LRU_BLOCKS = 16
LRU_BLOCK_W = LRU_WIDTH // LRU_BLOCKS
LRU_C = 8.0
MLA_HEADS = 8
MLA_NOPE = 128
MLA_ROPE = 64
MLA_QK_DIM = MLA_NOPE + MLA_ROPE
MLA_V = 128
MLA_Q_LORA = 384
MLA_KV_LORA = 256
ROPE_THETA = 10000.0
N_EXPERTS = 16
N_EXPERT_GROUPS = 4
EXPERTS_PER_GROUP = N_EXPERTS // N_EXPERT_GROUPS
TOP_K = 2
D_EXPERT = 512
N_EVEN = (DEPTH + 1) // 2
N_ODD = DEPTH // 2
EVEN_SPLITS = (GLA_HEADS * GLA_DK, GLA_HEADS * GLA_DK, GLA_HEADS * GLA_DV, GLA_GATE_RANK,
               GLA_HEADS * GLA_DV, SSD_INNER, SSD_CONV_DIM, SSD_HEADS)
EVEN_IN = sum(EVEN_SPLITS)
EVEN_MIX = GLA_HEADS * GLA_DV + SSD_INNER
ODD_SPLITS = (LRU_WIDTH, LRU_WIDTH, MLA_Q_LORA, MLA_KV_LORA, MLA_ROPE)
ODD_IN = sum(ODD_SPLITS)
ODD_MIX = LRU_WIDTH + MLA_HEADS * MLA_V

kernel_name = 'hybrid_gla_ssd_rglru_mla_moe'


def _split(u, sizes):
    offs, acc = [], 0
    for s in sizes[:-1]:
        acc += s
        offs.append(acc)
    return jnp.split(u, offs, axis=-1)


def rms_norm(x, g):
    xf = x.astype(jnp.float32)
    y = xf * lax.rsqrt(jnp.mean(xf * xf, axis=-1, keepdims=True) + EPS)
    return (y * g).astype(x.dtype)


def modulate(x, g, shift, scale):
    return rms_norm(x, g) * (1.0 + scale) + shift


def causal_dwconv(x, w, b):
    ch = x.shape[-1]
    y = lax.conv_general_dilated(x, w.astype(x.dtype)[:, None, :], window_strides=(1,),
                                 padding=[(CONV_K - 1, 0)],
                                 dimension_numbers=('NWC', 'WIO', 'NWC'),
                                 feature_group_count=ch)
    return y + b.astype(x.dtype)


def gla_chunked(q, k, v, log_a):
    b_, s_, h_, dk = q.shape
    dv = v.shape[-1]
    nc = s_ // CHUNK
    q = (q * dk ** -0.5).reshape(b_, nc, CHUNK, h_, dk)
    k = k.reshape(b_, nc, CHUNK, h_, dk)
    v = v.reshape(b_, nc, CHUNK, h_, dv)
    cum = jnp.cumsum(log_a.reshape(b_, nc, CHUNK, h_, dk), axis=2)
    last = cum[:, :, -1:]
    q_dec = q * jnp.exp(cum)
    k_inv = k * jnp.exp(-cum)
    k_end = k * jnp.exp(last - cum)
    causal = jnp.tril(jnp.ones((CHUNK, CHUNK), dtype=bool))
    att = jnp.where(causal, jnp.einsum('bnihd,bnjhd->bnhij', q_dec, k_inv), 0.0)
    o_intra = jnp.einsum('bnhij,bnjhe->bnihe', att, v)
    upd = jnp.einsum('bnjhd,bnjhe->nbhde', k_end, v)
    decay = jnp.moveaxis(jnp.exp(last[:, :, 0]), 1, 0)
    q_scan = jnp.moveaxis(q_dec, 1, 0)

    def step(state, xs):
        upd_c, dec_c, q_c = xs
        o_c = jnp.einsum('bihd,bhde->bihe', q_c, state)
        return dec_c[..., None] * state + upd_c, o_c

    init = jnp.zeros((b_, h_, dk, dv), q.dtype)
    _, o_inter = lax.scan(step, init, (upd, decay, q_scan))
    o = o_intra + jnp.moveaxis(o_inter, 0, 1)
    return o.reshape(b_, s_, h_, dv)


def ssd_chunked(x, dt, a, bm, cm):
    b_, s_, g_, hg, p_ = x.shape
    n_ = bm.shape[-1]
    nc = s_ // CHUNK
    x = x.reshape(b_, nc, CHUNK, g_, hg, p_)
    dt = dt.reshape(b_, nc, CHUNK, g_, hg)
    bm = bm.reshape(b_, nc, CHUNK, g_, n_)
    cm = cm.reshape(b_, nc, CHUNK, g_, n_)
    cum = jnp.cumsum(dt * a, axis=2)
    causal = jnp.tril(jnp.ones((CHUNK, CHUNK), dtype=bool))[:, :, None, None]
    seg = cum[:, :, :, None] - cum[:, :, None, :]
    decay_in = jnp.exp(jnp.where(causal, seg, -jnp.inf))
    cb = jnp.einsum('bclgn,bcsgn->bclsg', cm, bm)
    w = cb[..., None] * decay_in * dt[:, :, None]
    y_diag = jnp.einsum('bclsgh,bcsghp->bclghp', w, x)
    last = cum[:, :, -1:]
    states = jnp.einsum('bclgn,bclgh,bclghp->cbghpn', bm, jnp.exp(last - cum) * dt, x)
    chunk_decay = jnp.moveaxis(jnp.exp(last[:, :, 0]), 1, 0)
    c_scan = jnp.moveaxis(cm, 1, 0)
    in_decay = jnp.moveaxis(jnp.exp(cum), 1, 0)

    def step(h, xs):
        st_c, dec_c, c_c, e_c = xs
        y_c = jnp.einsum('blgn,bghpn,blgh->blghp', c_c, h, e_c)
        return dec_c[..., None, None] * h + st_c, y_c

    init = jnp.zeros((b_, g_, hg, p_, n_), x.dtype)
    _, y_off = lax.scan(step, init, (states, chunk_decay, c_scan, in_decay))
    y = y_diag + jnp.moveaxis(y_off, 0, 1)
    return y.reshape(b_, s_, g_, hg, p_)


def rg_lru(x, w_a, b_a, w_x, b_x, lam):
    b_, s_, w_ = x.shape
    xb = x.reshape(b_, s_, LRU_BLOCKS, LRU_BLOCK_W)
    r = jax.nn.sigmoid(jnp.einsum('bsnw,nwv->bsnv', xb, w_a).reshape(b_, s_, w_) + b_a)
    i = jax.nn.sigmoid(jnp.einsum('bsnw,nwv->bsnv', xb, w_x).reshape(b_, s_, w_) + b_x)
    log_a = -LRU_C * r * jax.nn.softplus(-lam.astype(jnp.float32))
    a = jnp.exp(log_a)
    u = jnp.sqrt(-jnp.expm1(2.0 * log_a)) * (i * x)

    def combine(left, right):
        a_l, u_l = left
        a_r, u_r = right
        return a_l * a_r, a_r * u_l + u_r

    _, h = lax.associative_scan(combine, (a, u), axis=1)
    return h


def rope(x, cos, sin):
    x1, x2 = jnp.split(x, 2, axis=-1)
    return jnp.concatenate([x1 * cos - x2 * sin, x1 * sin + x2 * cos], axis=-1)


def blocked_causal_attention(q, k, v):
    s_ = q.shape[1]
    scale = q.shape[-1] ** -0.5
    outs = []
    for blk in range(s_ // Q_BLOCK):
        q0, kv_len = blk * Q_BLOCK, (blk + 1) * Q_BLOCK
        sc = jnp.einsum('bqhd,bkhd->bhqk', q[:, q0:kv_len], k[:, :kv_len]).astype(jnp.float32) * scale
        mask = jnp.arange(kv_len)[None, :] <= (q0 + jnp.arange(Q_BLOCK))[:, None]
        p = jax.nn.softmax(jnp.where(mask, sc, -jnp.inf), axis=-1)
        outs.append(jnp.einsum('bhqk,bkhd->bqhd', p.astype(v.dtype), v[:, :kv_len]))
    return jnp.concatenate(outs, axis=1)


def even_mixer(h, w_in, gla_w_g2, gla_b_g2, gla_onorm, ssd_conv_w, ssd_conv_b,
               ssd_dt_bias, ssd_a_log, ssd_d, ssd_norm, w_out):
    f32 = jnp.float32
    b_, s_, _ = h.shape
    u = h @ w_in
    q, k, v, g_lr, og, z, xbc, dt = _split(u, EVEN_SPLITS)
    log_a = jax.nn.log_sigmoid((g_lr @ gla_w_g2 + gla_b_g2).astype(f32)) / GLA_GATE_NORM
    hd = lambda t, d: t.astype(f32).reshape(b_, s_, GLA_HEADS, d)
    o_gla = gla_chunked(hd(q, GLA_DK), hd(k, GLA_DK), hd(v, GLA_DV), hd(log_a, GLA_DK))
    o_gla = rms_norm(o_gla, gla_onorm.reshape(GLA_HEADS, GLA_DV)).reshape(b_, s_, -1)
    o_gla = o_gla.astype(h.dtype) * jax.nn.silu(og)
    xbc = jax.nn.silu(causal_dwconv(xbc, ssd_conv_w, ssd_conv_b))
    xs, bm, cm = _split(xbc, (SSD_INNER, SSD_GROUPS * SSD_STATE, SSD_GROUPS * SSD_STATE))
    dt = jax.nn.softplus(dt.astype(f32) + ssd_dt_bias)
    a = -jnp.exp(ssd_a_log.astype(f32))
    xs_h = xs.astype(f32).reshape(b_, s_, SSD_GROUPS, SSD_HPG, SSD_HEADDIM)
    y = ssd_chunked(xs_h, dt.reshape(b_, s_, SSD_GROUPS, SSD_HPG), a.reshape(SSD_GROUPS, SSD_HPG),
                    bm.astype(f32).reshape(b_, s_, SSD_GROUPS, SSD_STATE),
                    cm.astype(f32).reshape(b_, s_, SSD_GROUPS, SSD_STATE))
    y = y + ssd_d.reshape(SSD_GROUPS, SSD_HPG)[..., None] * xs_h
    y = y.reshape(b_, s_, SSD_INNER) * jax.nn.silu(z.astype(f32))
    y = rms_norm(y.reshape(b_, s_, SSD_GROUPS, SSD_INNER // SSD_GROUPS),
                 ssd_norm.reshape(SSD_GROUPS, -1)).reshape(b_, s_, SSD_INNER)
    mix = jnp.concatenate([o_gla, y.astype(h.dtype)], axis=-1)
    return mix @ w_out


def odd_mixer(h, positions, w_in, lru_conv_w, lru_conv_b, lru_w_a, lru_b_a, lru_w_x, lru_b_x,
              lru_lambda, mla_q_norm, mla_w_q_up, mla_kv_norm, mla_w_kv_up, mla_q_qknorm,
              mla_k_qknorm, w_out):
    f32 = jnp.float32
    b_, s_, _ = h.shape
    u = h @ w_in
    gate, xr, u_q, u_kv, k_r = _split(u, ODD_SPLITS)
    xr = causal_dwconv(xr, lru_conv_w, lru_conv_b)
    hr = rg_lru(xr.astype(f32), lru_w_a, lru_b_a, lru_w_x, lru_b_x, lru_lambda)
    o_lru = hr.astype(h.dtype) * jax.nn.gelu(gate)
    q = (rms_norm(u_q, mla_q_norm) @ mla_w_q_up).reshape(b_, s_, MLA_HEADS, MLA_QK_DIM)
    kv = (rms_norm(u_kv, mla_kv_norm) @ mla_w_kv_up).reshape(b_, s_, MLA_HEADS, MLA_NOPE + MLA_V)
    k_nope, v = jnp.split(kv, [MLA_NOPE], axis=-1)
    k = jnp.concatenate([k_nope, jnp.broadcast_to(k_r[:, :, None, :], (b_, s_, MLA_HEADS, MLA_ROPE))], axis=-1)
    q = rms_norm(q, mla_q_qknorm)
    k = rms_norm(k, mla_k_qknorm)
    freqs = ROPE_THETA ** (-jnp.arange(0, MLA_ROPE, 2, dtype=f32) / MLA_ROPE)
    ang = positions.astype(f32)[..., None] * freqs
    cos = jnp.cos(ang)[:, :, None, :].astype(q.dtype)
    sin = jnp.sin(ang)[:, :, None, :].astype(q.dtype)
    q = jnp.concatenate([q[..., :MLA_NOPE], rope(q[..., MLA_NOPE:], cos, sin)], axis=-1)
    k = jnp.concatenate([k[..., :MLA_NOPE], rope(k[..., MLA_NOPE:], cos, sin)], axis=-1)
    o_mla = blocked_causal_attention(q, k, v).reshape(b_, s_, MLA_HEADS * MLA_V)
    mix = jnp.concatenate([o_lru, o_mla], axis=-1)
    return mix @ w_out


def moe(h, router_w, router_bias, w_gate, w_up, w_down):
    f32 = jnp.float32
    b_, s_, d_ = h.shape
    xt = h.reshape(-1, d_)
    scores = jax.nn.sigmoid(xt.astype(f32) @ router_w.astype(f32))
    sel = scores + router_bias.astype(f32)
    grp = sel.reshape(-1, N_EXPERT_GROUPS, EXPERTS_PER_GROUP)
    grp_score = lax.top_k(grp, 2)[0].sum(axis=-1)
    best = jnp.argmax(grp_score, axis=-1)
    in_group = (jnp.arange(N_EXPERTS) // EXPERTS_PER_GROUP)[None, :] == best[:, None]
    _, idx = lax.top_k(jnp.where(in_group, sel, -jnp.inf), TOP_K)
    wts = jnp.take_along_axis(scores, idx, axis=-1)
    wts = wts / jnp.sum(wts, axis=-1, keepdims=True)
    combine = jnp.einsum('tk,tke->te', wts, jax.nn.one_hot(idx, N_EXPERTS, dtype=f32))
    out = jnp.zeros_like(xt)
    for e in range(N_EXPERTS):
        hid = jax.nn.silu(xt @ w_gate[e]) * (xt @ w_up[e])
        out = out + combine[:, e:e + 1].astype(xt.dtype) * (hid @ w_down[e])
    return out.reshape(b_, s_, d_)


def setup_inputs(seed: int = 0) -> dict:
    key = jax.random.key(seed)
    ks = iter(jax.random.split(key, 64))
    f32 = jnp.float32

    def nrm(shape, scale):
        return scale * jax.random.normal(next(ks), shape, f32)

    def gain(shape):
        return 1.0 + 0.05 * jax.random.normal(next(ks), shape, f32)

    def unif(shape, lo, hi):
        return jax.random.uniform(next(ks), shape, f32, lo, hi)

    D = D_MODEL
    x = nrm((BATCH, SEQ, D), 1.0)
    c = nrm((BATCH, D), 1.0)
    offsets = jax.random.randint(next(ks), (BATCH, 1), 0, 1024, dtype=jnp.int32)
    positions = offsets + jnp.arange(SEQ, dtype=jnp.int32)[None, :]
    router_w = nrm((D, N_EXPERTS), D ** -0.5)
    router_bias = nrm((N_EXPERTS,), 0.01)
    ada_w = nrm((DEPTH, D, 6 * D), 0.5 * D ** -0.5)
    ada_b = nrm((DEPTH, 6 * D), 0.02)
    norm_mix = gain((DEPTH, D))
    norm_ffn = gain((DEPTH, D))
    moe_w_gate = nrm((DEPTH, N_EXPERTS, D, D_EXPERT), D ** -0.5)
    moe_w_up = nrm((DEPTH, N_EXPERTS, D, D_EXPERT), D ** -0.5)
    moe_w_down = nrm((DEPTH, N_EXPERTS, D_EXPERT, D), D_EXPERT ** -0.5)
    ev_w_in = nrm((N_EVEN, D, EVEN_IN), D ** -0.5)
    gla_w_g2 = nrm((N_EVEN, GLA_GATE_RANK, GLA_HEADS * GLA_DK), GLA_GATE_RANK ** -0.5)
    gla_b_g2 = nrm((N_EVEN, GLA_HEADS * GLA_DK), 0.1)
    gla_onorm = gain((N_EVEN, GLA_HEADS * GLA_DV))
    ssd_conv_w = nrm((N_EVEN, CONV_K, SSD_CONV_DIM), CONV_K ** -0.5)
    ssd_conv_b = nrm((N_EVEN, SSD_CONV_DIM), 0.02)
    dt0 = jnp.exp(unif((N_EVEN, SSD_HEADS), math.log(1e-3), math.log(1e-1)))
    ssd_dt_bias = dt0 + jnp.log(-jnp.expm1(-dt0))
    ssd_a_log = jnp.log(unif((N_EVEN, SSD_HEADS), 1.0, 16.0))
    ssd_d = gain((N_EVEN, SSD_HEADS))
    ssd_norm = gain((N_EVEN, SSD_INNER))
    ev_w_out = nrm((N_EVEN, EVEN_MIX, D), EVEN_MIX ** -0.5)
    od_w_in = nrm((N_ODD, D, ODD_IN), D ** -0.5)
    lru_conv_w = nrm((N_ODD, CONV_K, LRU_WIDTH), CONV_K ** -0.5)
    lru_conv_b = nrm((N_ODD, LRU_WIDTH), 0.02)
    lru_w_a = nrm((N_ODD, LRU_BLOCKS, LRU_BLOCK_W, LRU_BLOCK_W), LRU_BLOCK_W ** -0.5)
    lru_b_a = nrm((N_ODD, LRU_WIDTH), 0.1)
    lru_w_x = nrm((N_ODD, LRU_BLOCKS, LRU_BLOCK_W, LRU_BLOCK_W), LRU_BLOCK_W ** -0.5)
    lru_b_x = nrm((N_ODD, LRU_WIDTH), 0.1)
    a0 = unif((N_ODD, LRU_WIDTH), 0.9, 0.999) ** (1.0 / LRU_C)
    lru_lambda = jnp.log(a0) - jnp.log1p(-a0)
    mla_q_norm = gain((N_ODD, MLA_Q_LORA))
    mla_w_q_up = nrm((N_ODD, MLA_Q_LORA, MLA_HEADS * MLA_QK_DIM), MLA_Q_LORA ** -0.5)
    mla_kv_norm = gain((N_ODD, MLA_KV_LORA))
    mla_w_kv_up = nrm((N_ODD, MLA_KV_LORA, MLA_HEADS * (MLA_NOPE + MLA_V)), MLA_KV_LORA ** -0.5)
    mla_q_qknorm = gain((N_ODD, MLA_QK_DIM))
    mla_k_qknorm = gain((N_ODD, MLA_QK_DIM))
    od_w_out = nrm((N_ODD, ODD_MIX, D), ODD_MIX ** -0.5)
    return {'x': x, 'c': c, 'positions': positions, 'router_w': router_w, 'router_bias': router_bias,
            'ada_w': ada_w, 'ada_b': ada_b, 'norm_mix': norm_mix, 'norm_ffn': norm_ffn,
            'moe_w_gate': moe_w_gate, 'moe_w_up': moe_w_up, 'moe_w_down': moe_w_down,
            'ev_w_in': ev_w_in, 'gla_w_g2': gla_w_g2, 'gla_b_g2': gla_b_g2, 'gla_onorm': gla_onorm,
            'ssd_conv_w': ssd_conv_w, 'ssd_conv_b': ssd_conv_b, 'ssd_dt_bias': ssd_dt_bias,
            'ssd_a_log': ssd_a_log, 'ssd_d': ssd_d, 'ssd_norm': ssd_norm, 'ev_w_out': ev_w_out,
            'od_w_in': od_w_in, 'lru_conv_w': lru_conv_w, 'lru_conv_b': lru_conv_b,
            'lru_w_a': lru_w_a, 'lru_b_a': lru_b_a, 'lru_w_x': lru_w_x, 'lru_b_x': lru_b_x,
            'lru_lambda': lru_lambda, 'mla_q_norm': mla_q_norm, 'mla_w_q_up': mla_w_q_up,
            'mla_kv_norm': mla_kv_norm, 'mla_w_kv_up': mla_w_kv_up, 'mla_q_qknorm': mla_q_qknorm,
            'mla_k_qknorm': mla_k_qknorm, 'od_w_out': od_w_out}


def reference(x, c, positions, router_w, router_bias, ada_w, ada_b, norm_mix, norm_ffn,
              moe_w_gate, moe_w_up, moe_w_down, ev_w_in, gla_w_g2, gla_b_g2, gla_onorm,
              ssd_conv_w, ssd_conv_b, ssd_dt_bias, ssd_a_log, ssd_d, ssd_norm, ev_w_out,
              od_w_in, lru_conv_w, lru_conv_b, lru_w_a, lru_b_a, lru_w_x, lru_b_x, lru_lambda,
              mla_q_norm, mla_w_q_up, mla_kv_norm, mla_w_kv_up, mla_q_qknorm, mla_k_qknorm,
              od_w_out):
    h = x
    c_act = jax.nn.silu(c)
    for layer in range(DEPTH):
        mod = (c_act @ ada_w[layer] + ada_b[layer])[:, None, :]
        sh1, sc1, g1, sh2, sc2, g2 = jnp.split(mod, 6, axis=-1)
        hm = modulate(h, norm_mix[layer], sh1, sc1)
        i = layer // 2
        if layer % 2 == 0:
            y = even_mixer(hm, ev_w_in[i], gla_w_g2[i], gla_b_g2[i], gla_onorm[i], ssd_conv_w[i],
                           ssd_conv_b[i], ssd_dt_bias[i], ssd_a_log[i], ssd_d[i], ssd_norm[i],
                           ev_w_out[i])
        else:
            y = odd_mixer(hm, positions, od_w_in[i], lru_conv_w[i], lru_conv_b[i], lru_w_a[i],
                          lru_b_a[i], lru_w_x[i], lru_b_x[i], lru_lambda[i], mla_q_norm[i],
                          mla_w_q_up[i], mla_kv_norm[i], mla_w_kv_up[i], mla_q_qknorm[i],
                          mla_k_qknorm[i], od_w_out[i])
        h = h + g1 * y
        hf = modulate(h, norm_ffn[layer], sh2, sc2)
        h = h + g2 * moe(hf, router_w, router_bias, moe_w_gate[layer], moe_w_up[layer],
                         moe_w_down[layer])
    return h
```

```python
import functools
import math

import jax
import jax.numpy as jnp
from jax import lax
from jax.experimental import pallas as pl
from jax.experimental.pallas import tpu as pltpu

F32 = jnp.float32
BF16 = jnp.bfloat16

EPS = 1e-6
CHUNK = 64
CONV_K = 4
GLA_HEADS, GLA_DK, GLA_DV = 4, 64, 128
GLA_GATE_RANK, GLA_GATE_NORM = 16, 16.0
SSD_HEADS, SSD_HEADDIM, SSD_GROUPS, SSD_STATE = 16, 64, 2, 128
SSD_INNER = SSD_HEADS * SSD_HEADDIM
SSD_GINNER = SSD_INNER // SSD_GROUPS
LRU_BLOCKS, LRU_C = 16, 8.0
MLA_HEADS, MLA_NOPE, MLA_ROPE, MLA_V = 8, 128, 64, 128
MLA_QK = MLA_NOPE + MLA_ROPE
MLA_QK_PAD = 256
MLA_Q_LORA, MLA_KV_LORA = 384, 256
ROPE_THETA = 10000.0
N_EXPERTS, N_GROUPS, D_EXPERT = 16, 4, 512
EXPERTS_PER_GROUP = N_EXPERTS // N_GROUPS

TM_SEQ = 256
TM_MM = 512
TM_MOE = 1024
TQ_ATTN = 512

LANES = 128
VMEM_LIMIT = 48 * 1024 * 1024

NT_DIMS = (((1,), (1,)), ((), ()))
TN_DIMS = (((0,), (0,)), ((), ()))


def _params(*sem):
    return pltpu.CompilerParams(dimension_semantics=sem, vmem_limit_bytes=VMEM_LIMIT)


def _dot(a, b):
    return jnp.dot(a, b, preferred_element_type=F32)


def _dot_nt(a, b):
    return lax.dot_general(a, b, NT_DIMS, preferred_element_type=F32)


def _dot_tn(a, b):
    return lax.dot_general(a, b, TN_DIMS, preferred_element_type=F32)


def _split3(a):
    hi = a.astype(BF16)
    r1 = a - hi.astype(F32)
    mid = r1.astype(BF16)
    lo = (r1 - mid.astype(F32)).astype(BF16)
    return hi, mid, lo


def _dot_exact_rhs(a, b_bf16):
    hi, mid, lo = _split3(a)
    return _dot(hi, b_bf16) + _dot(mid, b_bf16) + _dot(lo, b_bf16)


def _dot_exact_lhs(a_bf16, b):
    hi, mid, lo = _split3(b)
    return _dot(a_bf16, hi) + _dot(a_bf16, mid) + _dot(a_bf16, lo)


def _softplus(x):
    return jnp.maximum(x, 0.0) + jnp.log1p(jnp.exp(-jnp.abs(x)))


def _silu(x):
    return x * jax.nn.sigmoid(x)


def _lane(shape):
    return lax.broadcasted_iota(jnp.int32, shape, len(shape) - 1)


def _row(shape):
    return lax.broadcasted_iota(jnp.int32, shape, len(shape) - 2)


def _ada_kernel(c_ref, w_ref, b_ref, o_ref):
    c = c_ref[...]
    a_hi, a_mid, a_lo = _split3(_silu(c))
    w_hi, w_mid, w_lo = _split3(w_ref[0])
    acc = (_dot(a_hi, w_hi) + _dot(a_hi, w_mid) + _dot(a_mid, w_hi)
           + _dot(a_hi, w_lo) + _dot(a_lo, w_hi) + _dot(a_mid, w_mid))
    o_ref[0] = acc + b_ref[0]


def ada_mod(c, ada_w, ada_b):
    depth, d, n = ada_w.shape
    b = c.shape[0]
    bp = 8
    cp = jnp.zeros((bp, d), F32).at[:b].set(c)
    tn = 1536
    out = pl.pallas_call(
        _ada_kernel,
        grid=(depth, n // tn),
        in_specs=[pl.BlockSpec((bp, d), lambda l, j: (0, 0)),
                  pl.BlockSpec((1, d, tn), lambda l, j: (l, 0, j)),
                  pl.BlockSpec((1, 1, tn), lambda l, j: (l, 0, j))],
        out_specs=pl.BlockSpec((1, bp, tn), lambda l, j: (l, 0, j)),
        out_shape=jax.ShapeDtypeStruct((depth, bp, n), F32),
        compiler_params=_params("arbitrary", "arbitrary"),
        name="ada_mod",
    )(cp, ada_w, ada_b.reshape(depth, 1, n))
    return out[:, :b]


def _inproj_kernel(x_ref, g_ref, sh_ref, sc_ref, w_ref, o_ref, hm_ref):
    @pl.when(pl.program_id(1) == 0)
    def _():
        x = x_ref[...]
        ms = jnp.mean(x * x, axis=-1, keepdims=True)
        y = x * lax.rsqrt(ms + EPS) * g_ref[...]
        hm_ref[...] = (y * (1.0 + sc_ref[0]) + sh_ref[0]).astype(BF16)

    o_ref[...] = _dot(hm_ref[...], w_ref[...]).astype(o_ref.dtype)


def inproj(h2d, gain, shift, scale, w, seq, tm, tn):
    t, d = h2d.shape
    n = w.shape[1]
    tpb = seq // tm
    return pl.pallas_call(
        _inproj_kernel,
        grid=(t // tm, n // tn),
        in_specs=[pl.BlockSpec((tm, d), lambda i, j: (i, 0)),
                  pl.BlockSpec((1, d), lambda i, j: (0, 0)),
                  pl.BlockSpec((1, 1, d), lambda i, j: (i // tpb, 0, 0)),
                  pl.BlockSpec((1, 1, d), lambda i, j: (i // tpb, 0, 0)),
                  pl.BlockSpec((d, tn), lambda i, j: (0, j))],
        out_specs=pl.BlockSpec((tm, tn), lambda i, j: (i, j)),
        out_shape=jax.ShapeDtypeStruct((t, n), BF16),
        scratch_shapes=[pltpu.VMEM((tm, d), BF16)],
        compiler_params=_params("arbitrary", "arbitrary"),
        name="inproj",
    )(h2d, gain.reshape(1, d), shift[:, None, :], scale[:, None, :], w)


def _gla_kernel(q_ref, k_ref, v_ref, og_ref, misc_ref, wg_ref, bg_ref, on_ref, tri_ref,
                o_ref, st_ref, cum_ref):
    tm = q_ref.shape[0]

    @pl.when(pl.program_id(1) == 0)
    def _():
        st_ref[...] = jnp.zeros_like(st_ref)

    g = _dot(misc_ref[...], wg_ref[...]) + bg_ref[...]
    la = (jnp.minimum(g, 0.0) - jnp.log1p(jnp.exp(-jnp.abs(g)))) * (1.0 / GLA_GATE_NORM)
    la = jnp.where((_lane(la.shape) & (LANES - 1)) < GLA_DK, la, 0.0)
    cum_ref[...] = _dot_exact_lhs(tri_ref[...], la)

    causal = _row((CHUNK, CHUNK)) >= _lane((CHUNK, CHUNK))
    for c in range(tm // CHUNK):
        rows = pl.ds(c * CHUNK, CHUNK)
        for h in range(GLA_HEADS):
            cols = pl.ds(h * LANES, LANES)
            cu = cum_ref[rows, cols]
            last = cu[CHUNK - 1:CHUNK, :]
            qh = q_ref[rows, cols].astype(F32) * (GLA_DK ** -0.5)
            kh = k_ref[rows, cols].astype(F32)
            q_dec = (qh * jnp.exp(cu)).astype(BF16)
            k_inv = (kh * jnp.exp(-cu)).astype(BF16)
            k_end = (kh * jnp.exp(last - cu)).astype(BF16)
            vh = v_ref[rows, cols]
            att = jnp.where(causal, _dot_nt(q_dec, k_inv), 0.0)
            st = st_ref[h]
            o = _dot(att.astype(BF16), vh) + _dot_nt(q_dec, st.astype(BF16))
            st_ref[h] = st * jnp.exp(last) + _dot_tn(vh, k_end)
            ms = jnp.mean(o * o, axis=-1, keepdims=True)
            on = o * lax.rsqrt(ms + EPS) * on_ref[:, cols]
            o_ref[rows, cols] = (on * _silu(og_ref[rows, cols].astype(F32))).astype(o_ref.dtype)


def gla(u, cb, wg, bg, onorm, tri, batch, seq, tm):
    t = u.shape[0]
    tpb = seq // tm
    w512 = GLA_HEADS * LANES
    row = lambda b, i: b * tpb + i
    return pl.pallas_call(
        _gla_kernel,
        grid=(batch, tpb),
        in_specs=[pl.BlockSpec((tm, w512), lambda b, i: (row(b, i), cb["q"])),
                  pl.BlockSpec((tm, w512), lambda b, i: (row(b, i), cb["k"])),
                  pl.BlockSpec((tm, w512), lambda b, i: (row(b, i), cb["v"])),
                  pl.BlockSpec((tm, w512), lambda b, i: (row(b, i), cb["og"])),
                  pl.BlockSpec((tm, LANES), lambda b, i: (row(b, i), cb["glr"])),
                  pl.BlockSpec((LANES, w512), lambda b, i: (0, 0)),
                  pl.BlockSpec((1, w512), lambda b, i: (0, 0)),
                  pl.BlockSpec((1, w512), lambda b, i: (0, 0)),
                  pl.BlockSpec((tm, tm), lambda b, i: (0, 0))],
        out_specs=pl.BlockSpec((tm, w512), lambda b, i: (row(b, i), 0)),
        out_shape=jax.ShapeDtypeStruct((t, w512), BF16),
        scratch_shapes=[pltpu.VMEM((GLA_HEADS, GLA_DV, LANES), F32),
                        pltpu.VMEM((tm, w512), F32)],
        compiler_params=_params("arbitrary", "arbitrary"),
        name="gla",
    )(u, u, u, u, u, wg, bg, onorm, tri)


def _causal_conv(ext_ref, x, w_ref, b_ref, first):
    tm = x.shape[0]

    @pl.when(first)
    def _():
        ext_ref[0:8, :] = jnp.zeros((8, ext_ref.shape[1]), F32)

    ext_ref[8:8 + tm, :] = x
    y = b_ref[...] + w_ref[CONV_K - 1:CONV_K, :] * x
    for kk in range(CONV_K - 1):
        off = 8 - (CONV_K - 1) + kk
        y = y + w_ref[kk:kk + 1, :] * ext_ref[off:off + tm, :]
    ext_ref[0:8, :] = ext_ref[tm:tm + 8, :]
    return y


def _ssd_kernel(xs_ref, z_ref, bc_ref, dte_ref, dto_ref,
                cwx_ref, cbx_ref, cwb_ref, cbb_ref, dtb_ref, alog_ref, dexp_ref, ng_ref,
                expand_ref, tri_ref, o_ref,
                extx_ref, extb_ref, st_ref, cume_ref, cumo_ref, cumx_ref, xdt_ref, bcc_ref, xsc_ref):
    tm = xs_ref.shape[0]
    first = pl.program_id(1) == 0

    @pl.when(first)
    def _():
        st_ref[...] = jnp.zeros_like(st_ref)

    xs = _silu(_causal_conv(extx_ref, xs_ref[...].astype(F32), cwx_ref, cbx_ref, first))
    bcv = _silu(_causal_conv(extb_ref, bc_ref[...].astype(F32), cwb_ref, cbb_ref, first))
    xsc_ref[...] = xs
    bcc_ref[...] = bcv.astype(BF16)

    npair = SSD_HEADS // 2
    lane = _lane((tm, LANES))
    valid = lane < npair
    a_e = -jnp.exp(alog_ref[0:1, :])
    a_o = -jnp.exp(alog_ref[1:2, :])
    dt_e = jnp.where(valid, _softplus(dte_ref[...].astype(F32) + dtb_ref[0:1, :]), 0.0)
    dt_o = jnp.where(valid, _softplus(dto_ref[...].astype(F32) + dtb_ref[1:2, :]), 0.0)
    tri = tri_ref[...]
    cum_e = _dot_exact_lhs(tri, dt_e * a_e)
    cum_o = _dot_exact_lhs(tri, dt_o * a_o)
    cume_ref[...] = cum_e
    cumo_ref[...] = cum_o
    ex_e = expand_ref[0]
    ex_o = expand_ref[1]
    cumx_ref[...] = _dot_exact_rhs(cum_e, ex_e) + _dot_exact_rhs(cum_o, ex_o)
    xdt_ref[...] = xs * (_dot_exact_rhs(dt_e, ex_e) + _dot_exact_rhs(dt_o, ex_o))

    rr = _row((CHUNK, LANES))
    ll = _lane((CHUNK, LANES))
    causal2 = rr >= (ll & (CHUNK - 1))
    left = ll < CHUNK
    gw = SSD_GINNER
    for c in range(tm // CHUNK):
        rows = pl.ds(c * CHUNK, CHUNK)
        cumx = cumx_ref[rows, :]
        lastx = cumx[CHUNK - 1:CHUNK, :]
        xdt = xdt_ref[rows, :]
        xw = (xdt * jnp.exp(lastx - cumx)).astype(BF16)
        ecum = jnp.exp(cumx)
        pt = jnp.concatenate([cume_ref[rows, :], cumo_ref[rows, :]], axis=0).T
        ys = []
        for g in range(SSD_GROUPS):
            bg = bcc_ref[rows, pl.ds(g * SSD_STATE, SSD_STATE)]
            cg = bcc_ref[rows, pl.ds((SSD_GROUPS + g) * SSD_STATE, SSD_STATE)]
            st = st_ref[g]
            y_off = _dot(cg, st.astype(BF16)) * ecum[:, g * gw:(g + 1) * gw]
            cbcb = _dot_nt(cg, jnp.concatenate([bg, bg], axis=0))
            parts = []
            for j in range(npair // SSD_GROUPS):
                jp = g * (npair // SSD_GROUPS) + j
                colp = cumx[:, jp * LANES:(jp + 1) * LANES]
                seg = colp - pt[jp:jp + 1, :]
                dec = jnp.exp(jnp.where(causal2, seg, -jnp.inf))
                w = (cbcb * dec).astype(BF16)
                xp = xdt[:, jp * LANES:(jp + 1) * LANES]
                x2 = jnp.concatenate([jnp.where(left, xp, 0.0), jnp.where(left, 0.0, xp)],
                                     axis=0).astype(BF16)
                parts.append(_dot(w, x2))
            ys.append(jnp.concatenate(parts, axis=1) + y_off)
            st_ref[g] = (st * jnp.exp(lastx[:, g * gw:(g + 1) * gw])
                         + _dot_tn(bg, xw[:, g * gw:(g + 1) * gw]))
        y = jnp.concatenate(ys, axis=1)
        y = y + dexp_ref[...] * xsc_ref[rows, :]
        y = y * _silu(z_ref[rows, :].astype(F32))
        for g in range(SSD_GROUPS):
            yg = y[:, g * gw:(g + 1) * gw]
            ms = jnp.mean(yg * yg, axis=-1, keepdims=True)
            o_ref[rows, pl.ds(g * gw, gw)] = (
                yg * lax.rsqrt(ms + EPS) * ng_ref[:, g * gw:(g + 1) * gw]).astype(o_ref.dtype)


def ssd(u, cb, cwx, cbx, cwb, cbb, dtb, alog, dexp, ng, expand, tri, batch, seq, tm):
    t = u.shape[0]
    tpb = seq // tm
    row = lambda b, i: b * tpb + i
    bcw = 2 * SSD_GROUPS * SSD_STATE
    const = lambda shape: pl.BlockSpec(shape, lambda b, i: tuple(0 for _ in shape))
    return pl.pallas_call(
        _ssd_kernel,
        grid=(batch, tpb),
        in_specs=[pl.BlockSpec((tm, SSD_INNER), lambda b, i: (row(b, i), cb["xs"])),
                  pl.BlockSpec((tm, SSD_INNER), lambda b, i: (row(b, i), cb["z"])),
                  pl.BlockSpec((tm, bcw), lambda b, i: (row(b, i), cb["bc"])),
                  pl.BlockSpec((tm, LANES), lambda b, i: (row(b, i), cb["dte"])),
                  pl.BlockSpec((tm, LANES), lambda b, i: (row(b, i), cb["dto"])),
                  const((CONV_K, SSD_INNER)), const((1, SSD_INNER)),
                  const((CONV_K, bcw)), const((1, bcw)),
                  const((2, LANES)), const((2, LANES)),
                  const((1, SSD_INNER)), const((1, SSD_INNER)),
                  const((2, LANES, SSD_INNER)), const((tm, tm))],
        out_specs=pl.BlockSpec((tm, SSD_INNER), lambda b, i: (row(b, i), 0)),
        out_shape=jax.ShapeDtypeStruct((t, SSD_INNER), BF16),
        scratch_shapes=[pltpu.VMEM((tm + 8, SSD_INNER), F32),
                        pltpu.VMEM((tm + 8, bcw), F32),
                        pltpu.VMEM((SSD_GROUPS, SSD_STATE, SSD_GINNER), F32),
                        pltpu.VMEM((tm, LANES), F32),
                        pltpu.VMEM((tm, LANES), F32),
                        pltpu.VMEM((tm, SSD_INNER), F32),
                        pltpu.VMEM((tm, SSD_INNER), F32),
                        pltpu.VMEM((tm, bcw), BF16),
                        pltpu.VMEM((tm, SSD_INNER), F32)],
        compiler_params=_params("arbitrary", "arbitrary"),
        name="ssd",
    )(u, u, u, u, u, cwx, cbx, cwb, cbb, dtb, alog, dexp, ng, expand, tri)


def _gelu_tanh(x):
    return 0.5 * x * (1.0 + jnp.tanh(math.sqrt(2.0 / math.pi) * (x + 0.044715 * (x * x * x))))


def _lru_kernel(gate_ref, xr_ref, cw_ref, cb_ref, wa_ref, ba_ref, wx_ref, bx_ref, lam_ref,
                o_ref, ext_ref, carry_ref):
    tm, width = xr_ref.shape
    first = pl.program_id(1) == 0

    @pl.when(first)
    def _():
        carry_ref[...] = jnp.zeros_like(carry_ref)

    x = _causal_conv(ext_ref, xr_ref[...].astype(F32), cw_ref, cb_ref, first)
    xb = x.astype(BF16)
    nblk = wa_ref.shape[0]
    bw = width // nblk
    ra = jnp.concatenate([_dot(xb[:, n * bw:(n + 1) * bw], wa_ref[n]) for n in range(nblk)], axis=1)
    rx = jnp.concatenate([_dot(xb[:, n * bw:(n + 1) * bw], wx_ref[n]) for n in range(nblk)], axis=1)
    r = jax.nn.sigmoid(ra + ba_ref[...])
    ig = jax.nn.sigmoid(rx + bx_ref[...])
    log_a = (-LRU_C) * r * _softplus(-lam_ref[...])
    a = jnp.exp(log_a)
    u = jnp.sqrt(1.0 - jnp.exp(2.0 * log_a)) * (ig * x)

    rowi = _row((tm, width))
    d = 1
    while d < tm:
        m = rowi >= d
        a_s = pltpu.roll(a, d, 0)
        u_s = pltpu.roll(u, d, 0)
        u = jnp.where(m, a * u_s + u, u)
        a = jnp.where(m, a * a_s, a)
        d *= 2
    hseq = a * carry_ref[0:1, :] + u
    carry_ref[...] = jnp.broadcast_to(hseq[tm - 1:tm, :], carry_ref.shape)
    o_ref[...] = (hseq * _gelu_tanh(gate_ref[...].astype(F32))).astype(o_ref.dtype)


def lru(u, cb, cw, cbias, wa, ba, wx, bx, lam, batch, seq, tm):
    t = u.shape[0]
    width = cw.shape[1]
    tpb = seq // tm
    row = lambda b, i: b * tpb + i
    const = lambda shape: pl.BlockSpec(shape, lambda b, i: tuple(0 for _ in shape))
    return pl.pallas_call(
        _lru_kernel,
        grid=(batch, tpb),
        in_specs=[pl.BlockSpec((tm, width), lambda b, i: (row(b, i), cb["gate"])),
                  pl.BlockSpec((tm, width), lambda b, i: (row(b, i), cb["xr"])),
                  const((CONV_K, width)), const((1, width)),
                  const(wa.shape), const((1, width)),
                  const(wx.shape), const((1, width)), const((1, width))],
        out_specs=pl.BlockSpec((tm, width), lambda b, i: (row(b, i), 0)),
        out_shape=jax.ShapeDtypeStruct((t, width), BF16),
        scratch_shapes=[pltpu.VMEM((tm + 8, width), F32),
                        pltpu.VMEM((8, width), F32)],
        compiler_params=_params("arbitrary", "arbitrary"),
        name="lru",
    )(u, u, cw, cbias, wa, ba, wx, bx, lam)


def _mla_prep_kernel(uq_ref, ukv_ref, krr_ref, pos_ref, freq_ref, qn_ref, kvn_ref,
                     wq_ref, wkv_ref, qg_ref, kg_ref, q_ref, k_ref, v_ref):
    tm = uq_ref.shape[0]
    lane = _lane((tm, LANES))
    lo_half = lane < MLA_ROPE

    ang = pos_ref[...].astype(F32) * freq_ref[...]
    cs = jnp.where(lo_half, jnp.cos(ang),
                   jnp.where(lane < MLA_ROPE + MLA_ROPE // 2, -jnp.sin(ang), jnp.sin(ang)))

    def latent_norm(ref, g_ref):
        x = ref[...].astype(F32)
        ms = jnp.mean(x * x, axis=-1, keepdims=True)
        return (x * lax.rsqrt(ms + EPS) * g_ref[...]).astype(BF16)

    qf = _dot(latent_norm(uq_ref, qn_ref), wq_ref[...])
    kvf = _dot(latent_norm(ukv_ref, kvn_ref), wkv_ref[...])
    krr = krr_ref[...].astype(F32)
    kr_ss = jnp.sum(jnp.where(lo_half, krr * krr, 0.0), axis=-1, keepdims=True)
    scale = MLA_QK ** -0.5

    def rope_half(y2):
        t = y2 * cs
        return jnp.where(lo_half, t + pltpu.roll(t, MLA_ROPE, 1), 0.0)

    for h in range(MLA_HEADS):
        x1 = qf[:, h * MLA_QK_PAD:h * MLA_QK_PAD + LANES]
        x2 = qf[:, h * MLA_QK_PAD + LANES:(h + 1) * MLA_QK_PAD]
        ss = (jnp.sum(x1 * x1, axis=-1, keepdims=True)
              + jnp.sum(jnp.where(lo_half, x2 * x2, 0.0), axis=-1, keepdims=True))
        r = lax.rsqrt(ss * (1.0 / MLA_QK) + EPS) * scale
        q_ref[:, pl.ds(h * MLA_QK_PAD, LANES)] = (x1 * r * qg_ref[:, 0:LANES]).astype(q_ref.dtype)
        q_ref[:, pl.ds(h * MLA_QK_PAD + LANES, LANES)] = rope_half(
            x2 * r * qg_ref[:, LANES:2 * LANES]).astype(q_ref.dtype)

        kn = kvf[:, h * LANES:(h + 1) * LANES]
        ssk = jnp.sum(kn * kn, axis=-1, keepdims=True) + kr_ss
        rk = lax.rsqrt(ssk * (1.0 / MLA_QK) + EPS)
        k_ref[:, pl.ds(h * MLA_QK_PAD, LANES)] = (kn * rk * kg_ref[:, 0:LANES]).astype(k_ref.dtype)
        k_ref[:, pl.ds(h * MLA_QK_PAD + LANES, LANES)] = rope_half(
            krr * rk * kg_ref[:, LANES:2 * LANES]).astype(k_ref.dtype)
        v_ref[:, pl.ds(h * MLA_V, MLA_V)] = kvf[:, (MLA_HEADS + h) * LANES:
                                                (MLA_HEADS + h + 1) * LANES].astype(v_ref.dtype)


def mla_prep(u, cb, pos2d, freq, qn, kvn, wq, wkv, qg, kg, tm):
    t = u.shape[0]
    const = lambda shape: pl.BlockSpec(shape, lambda i: tuple(0 for _ in shape))
    hq = MLA_HEADS * MLA_QK_PAD
    hv = MLA_HEADS * MLA_V
    return pl.pallas_call(
        _mla_prep_kernel,
        grid=(t // tm,),
        in_specs=[pl.BlockSpec((tm, MLA_Q_LORA), lambda i: (i, cb["uq"])),
                  pl.BlockSpec((tm, MLA_KV_LORA), lambda i: (i, cb["ukv"])),
                  pl.BlockSpec((tm, LANES), lambda i: (i, cb["krr"])),
                  pl.BlockSpec((tm, 1), lambda i: (i, 0)),
                  const((1, LANES)), const((1, MLA_Q_LORA)), const((1, MLA_KV_LORA)),
                  const(wq.shape), const(wkv.shape), const((1, 2 * LANES)), const((1, 2 * LANES))],
        out_specs=[pl.BlockSpec((tm, hq), lambda i: (i, 0)),
                   pl.BlockSpec((tm, hq), lambda i: (i, 0)),
                   pl.BlockSpec((tm, hv), lambda i: (i, 0))],
        out_shape=[jax.ShapeDtypeStruct((t, hq), BF16),
                   jax.ShapeDtypeStruct((t, hq), BF16),
                   jax.ShapeDtypeStruct((t, hv), BF16)],
        compiler_params=_params("arbitrary"),
        name="mla_prep",
    )(u, u, u, pos2d, freq, qn, kvn, wq, wkv, qg, kg)


def _attn_kernel(qi_ref, kj_ref, q_ref, k_ref, v_ref, o_ref, m_ref, l_ref, acc_ref):
    p_idx = pl.program_id(2)
    i = qi_ref[p_idx]
    j = kj_ref[p_idx]
    tq, tk = q_ref.shape[0], k_ref.shape[0]

    @pl.when(j == 0)
    def _():
        m_ref[...] = jnp.full_like(m_ref, -jnp.inf)
        l_ref[...] = jnp.zeros_like(l_ref)
        acc_ref[...] = jnp.zeros_like(acc_ref)

    def step(masked):
        s = _dot_nt(q_ref[...], k_ref[...])
        if masked:
            s = jnp.where(_row((tq, tk)) >= _lane((tq, tk)), s, -jnp.inf)
        m_prev = m_ref[...]
        m_new = jnp.maximum(m_prev, jnp.max(s, axis=-1, keepdims=True))
        p = jnp.exp(s - m_new)
        alpha = jnp.exp(m_prev - m_new)
        l_ref[...] = alpha * l_ref[...] + jnp.sum(p, axis=-1, keepdims=True)
        acc_ref[...] = alpha * acc_ref[...] + _dot(p.astype(BF16), v_ref[...])
        m_ref[...] = m_new

    @pl.when(j < i)
    def _():
        step(False)

    @pl.when(j == i)
    def _():
        step(True)
        o_ref[...] = (acc_ref[...] / l_ref[...]).astype(o_ref.dtype)


def attention(q, k, v, batch, seq, tq):
    t = q.shape[0]
    nq = seq // tq
    pairs = [(i, j) for i in range(nq) for j in range(i + 1)]
    qi = jnp.array([p[0] for p in pairs], jnp.int32)
    kj = jnp.array([p[1] for p in pairs], jnp.int32)
    grid_spec = pltpu.PrefetchScalarGridSpec(
        num_scalar_prefetch=2,
        grid=(batch, MLA_HEADS, len(pairs)),
        in_specs=[pl.BlockSpec((tq, MLA_QK_PAD), lambda b, h, p, qi, kj: (b * nq + qi[p], h)),
                  pl.BlockSpec((tq, MLA_QK_PAD), lambda b, h, p, qi, kj: (b * nq + kj[p], h)),
                  pl.BlockSpec((tq, MLA_V), lambda b, h, p, qi, kj: (b * nq + kj[p], h))],
        out_specs=pl.BlockSpec((tq, MLA_V), lambda b, h, p, qi, kj: (b * nq + qi[p], h)),
        scratch_shapes=[pltpu.VMEM((tq, 1), F32), pltpu.VMEM((tq, 1), F32),
                        pltpu.VMEM((tq, MLA_V), F32)])
    return pl.pallas_call(
        _attn_kernel,
        grid_spec=grid_spec,
        out_shape=jax.ShapeDtypeStruct((t, MLA_HEADS * MLA_V), BF16),
        compiler_params=_params("arbitrary", "arbitrary", "arbitrary"),
        name="attention",
    )(qi, kj, q, k, v)


def _outproj_kernel(oa_ref, ob_ref, w1_ref, w2_ref, h_ref, g1_ref, nf_ref, sh_ref, sc_ref,
                    rwt_ref, hn_ref, hf_ref, lg_ref):
    y = _dot(oa_ref[...], w1_ref[...]) + _dot(ob_ref[...], w2_ref[...])
    hn = h_ref[...] + g1_ref[0] * y
    hn_ref[...] = hn
    ms = jnp.mean(hn * hn, axis=-1, keepdims=True)
    hf = hn * lax.rsqrt(ms + EPS) * nf_ref[...] * (1.0 + sc_ref[0]) + sh_ref[0]
    hf_ref[...] = hf.astype(BF16)
    r_hi, r_mid, r_lo = _split3(rwt_ref[...])
    f_hi, f_mid, f_lo = _split3(hf)
    lg_ref[...] = (_dot_nt(r_hi, f_hi) + _dot_nt(r_hi, f_mid) + _dot_nt(r_mid, f_hi)
                   + _dot_nt(r_hi, f_lo) + _dot_nt(r_lo, f_hi) + _dot_nt(r_mid, f_mid))


def outproj(oa, ob, w1, w2, h2d, g1, nf, sh2, sc2, rwt, seq, tm):
    t, d = h2d.shape
    tpb = seq // tm
    const = lambda shape: pl.BlockSpec(shape, lambda i: tuple(0 for _ in shape))
    bvec = pl.BlockSpec((1, 1, d), lambda i: (i // tpb, 0, 0))
    return pl.pallas_call(
        _outproj_kernel,
        grid=(t // tm,),
        in_specs=[pl.BlockSpec((tm, oa.shape[1]), lambda i: (i, 0)),
                  pl.BlockSpec((tm, ob.shape[1]), lambda i: (i, 0)),
                  const(w1.shape), const(w2.shape),
                  pl.BlockSpec((tm, d), lambda i: (i, 0)),
                  bvec, const((1, d)), bvec, bvec, const(rwt.shape)],
        out_specs=[pl.BlockSpec((tm, d), lambda i: (i, 0)),
                   pl.BlockSpec((tm, d), lambda i: (i, 0)),
                   pl.BlockSpec((N_EXPERTS, tm), lambda i: (0, i))],
        out_shape=[jax.ShapeDtypeStruct((t, d), F32),
                   jax.ShapeDtypeStruct((t, d), BF16),
                   jax.ShapeDtypeStruct((N_EXPERTS, t), F32)],
        compiler_params=_params("arbitrary"),
        name="outproj",
    )(oa, ob, w1, w2, h2d, g1[:, None, :], nf.reshape(1, d), sh2[:, None, :], sc2[:, None, :], rwt)


def _route_kernel(lg_ref, bias_ref, o_ref):
    scores = jax.nn.sigmoid(lg_ref[...])
    sel = scores + bias_ref[...]
    tm = sel.shape[1]
    eidx = _row((N_EXPERTS, tm))
    neg = -jnp.inf

    best_score = None
    best_grp = None
    for g in range(N_GROUPS):
        m = [sel[g * EXPERTS_PER_GROUP + i:g * EXPERTS_PER_GROUP + i + 1, :]
             for i in range(EXPERTS_PER_GROUP)]
        gs = None
        for a in range(EXPERTS_PER_GROUP):
            for b in range(a + 1, EXPERTS_PER_GROUP):
                pair = m[a] + m[b]
                gs = pair if gs is None else jnp.maximum(gs, pair)
        if best_score is None:
            best_score, best_grp = gs, jnp.zeros_like(gs, dtype=jnp.int32)
        else:
            better = gs > best_score
            best_score = jnp.where(better, gs, best_score)
            best_grp = jnp.where(better, g, best_grp)

    masked = jnp.where((eidx >> 2) == best_grp, sel, neg)
    m1 = jnp.max(masked, axis=0, keepdims=True)
    i1 = jnp.min(jnp.where(masked == m1, eidx, N_EXPERTS), axis=0, keepdims=True)
    masked2 = jnp.where(eidx == i1, neg, masked)
    m2 = jnp.max(masked2, axis=0, keepdims=True)
    i2 = jnp.min(jnp.where(masked2 == m2, eidx, N_EXPERTS), axis=0, keepdims=True)
    pick1 = eidx == i1
    pick2 = eidx == i2
    s1 = jnp.sum(jnp.where(pick1, scores, 0.0), axis=0, keepdims=True)
    s2 = jnp.sum(jnp.where(pick2, scores, 0.0), axis=0, keepdims=True)
    tot = s1 + s2
    o_ref[...] = jnp.where(pick1, s1 / tot, 0.0) + jnp.where(pick2, s2 / tot, 0.0)


def route(logits_t, router_bias, tm):
    e, t = logits_t.shape
    return pl.pallas_call(
        _route_kernel,
        grid=(t // tm,),
        in_specs=[pl.BlockSpec((e, tm), lambda i: (0, i)),
                  pl.BlockSpec((e, 1), lambda i: (0, 0))],
        out_specs=pl.BlockSpec((e, tm), lambda i: (0, i)),
        out_shape=jax.ShapeDtypeStruct((e, t), F32),
        compiler_params=_params("arbitrary"),
        name="route",
    )(logits_t, router_bias.reshape(e, 1).astype(F32))


def _moe_kernel(x_ref, comb_ref, wgu_ref, wd_ref, h_ref, g2_ref, o_ref, acc_ref):
    e = pl.program_id(1)

    @pl.when(e == 0)
    def _():
        acc_ref[...] = jnp.zeros_like(acc_ref)

    gu = _dot(x_ref[...], wgu_ref[0])
    de = gu.shape[1] // 2
    comb = comb_ref[...]
    cw = jnp.sum(jnp.where(_lane(comb.shape) == e, comb, 0.0), axis=-1, keepdims=True)
    hid = _silu(gu[:, :de]) * gu[:, de:] * cw
    acc_ref[...] += _dot(hid.astype(BF16), wd_ref[0])

    @pl.when(e == pl.num_programs(1) - 1)
    def _():
        o_ref[...] = h_ref[...] + g2_ref[0] * acc_ref[...]


def moe_dense(hf, comb, wgu, wd, h2d, g2, seq, tm):
    t, d = h2d.shape
    ne = wgu.shape[0]
    tpb = seq // tm
    return pl.pallas_call(
        _moe_kernel,
        grid=(t // tm, ne),
        in_specs=[pl.BlockSpec((tm, d), lambda i, e: (i, 0)),
                  pl.BlockSpec((tm, LANES), lambda i, e: (i, 0)),
                  pl.BlockSpec((1,) + wgu.shape[1:], lambda i, e: (e, 0, 0)),
                  pl.BlockSpec((1,) + wd.shape[1:], lambda i, e: (e, 0, 0)),
                  pl.BlockSpec((tm, d), lambda i, e: (i, 0)),
                  pl.BlockSpec((1, 1, d), lambda i, e: (i // tpb, 0, 0))],
        out_specs=pl.BlockSpec((tm, d), lambda i, e: (i, 0)),
        out_shape=jax.ShapeDtypeStruct((t, d), F32),
        scratch_shapes=[pltpu.VMEM((tm, d), F32)],
        compiler_params=_params("arbitrary", "arbitrary"),
        name="moe",
    )(hf, comb, wgu, wd, h2d, g2[:, None, :])


def _pad_heads(w, heads, dh):
    lead = w.shape[:-1]
    w = w.reshape(lead + (heads, dh))
    w = jnp.pad(w, [(0, 0)] * len(lead) + [(0, 0), (0, LANES - dh)])
    return w.reshape(lead + (heads * LANES,))


def _pad_cols(w, n):
    return jnp.pad(w, [(0, 0)] * (w.ndim - 1) + [(0, n - w.shape[-1])])


def _tri_blocks(tm):
    r = jnp.arange(tm)
    return ((r[:, None] >= r[None, :]) & (r[:, None] // CHUNK == r[None, :] // CHUNK)).astype(BF16)


def _even_layout(w_in):
    sizes = (GLA_HEADS * GLA_DK, GLA_HEADS * GLA_DK, GLA_HEADS * GLA_DV, GLA_GATE_RANK,
             GLA_HEADS * GLA_DV, SSD_INNER, SSD_INNER + 2 * SSD_GROUPS * SSD_STATE, SSD_HEADS)
    offs = [0]
    for s in sizes:
        offs.append(offs[-1] + s)
    seg = lambda i: w_in[:, offs[i]:offs[i + 1]]
    q, k, v, glr, og, z, xbc, dt = (seg(i) for i in range(8))
    xs, bc = xbc[:, :SSD_INNER], xbc[:, SSD_INNER:]
    cols = [xs, z, v, og, _pad_heads(q, GLA_HEADS, GLA_DK), _pad_heads(k, GLA_HEADS, GLA_DK), bc,
            _pad_cols(glr, LANES), _pad_cols(dt[:, 0::2], LANES), _pad_cols(dt[:, 1::2], LANES)]
    w = jnp.concatenate(cols, axis=1).astype(BF16)
    cb = {"xs": 0, "z": 1, "v": 4, "og": 5, "q": 6, "k": 7, "bc": 8, "glr": 36, "dte": 37, "dto": 38}
    return w, cb


def _odd_layout(w_in):
    d = w_in.shape[0]
    o = [0, d, 2 * d, 2 * d + MLA_Q_LORA, 2 * d + MLA_Q_LORA + MLA_KV_LORA,
         2 * d + MLA_Q_LORA + MLA_KV_LORA + MLA_ROPE]
    gate, xr, uq, ukv, kr = (w_in[:, o[i]:o[i + 1]] for i in range(5))
    half = MLA_ROPE // 2
    kr_sw = jnp.concatenate([kr[:, half:], kr[:, :half]], axis=1)
    w = jnp.concatenate([gate, xr, ukv, uq, kr, kr_sw], axis=1).astype(BF16)
    cb = {"gate": 0, "xr": 1, "ukv": 8, "uq": 6, "krr": 21}
    return w, cb


def _swap_halves(x):
    half = x.shape[-1] // 2
    return jnp.concatenate([x[..., half:], x[..., :half]], axis=-1)


def _block_diag(w, per):
    nb, bw, _ = w.shape
    w = w.reshape(nb // per, per, bw, bw)
    eye = jnp.eye(per, dtype=w.dtype)
    out = jnp.einsum("gpij,pq->gpiqj", w, eye)
    return out.reshape(nb // per, per * bw, per * bw)


def kernel(x, c, positions, router_w, router_bias, ada_w, ada_b, norm_mix, norm_ffn, moe_w_gate, moe_w_up, moe_w_down, ev_w_in, gla_w_g2, gla_b_g2, gla_onorm, ssd_conv_w, ssd_conv_b, ssd_dt_bias, ssd_a_log, ssd_d, ssd_norm, ev_w_out, od_w_in, lru_conv_w, lru_conv_b, lru_w_a, lru_b_a, lru_w_x, lru_b_x, lru_lambda, mla_q_norm, mla_w_q_up, mla_kv_norm, mla_w_kv_up, mla_q_qknorm, mla_k_qknorm, od_w_out):
    batch, seq, d = x.shape
    t = batch * seq
    depth = ada_w.shape[0]
    tm_seq = min(TM_SEQ, seq)
    tm_mm = min(TM_MM, seq)
    tm_moe = min(TM_MOE, seq)
    tq = min(TQ_ATTN, seq)

    mod = ada_mod(c, ada_w, ada_b)
    rwt = router_w.T.astype(F32)
    tri = _tri_blocks(tm_seq)
    h = x.reshape(t, d)

    for layer in range(depth):
        sh1, sc1, g1, sh2, sc2, g2 = (mod[layer, :, i * d:(i + 1) * d] for i in range(6))
        i = layer // 2
        if layer % 2 == 0:
            w_in, cb = _even_layout(ev_w_in[i])
            u = inproj(h, norm_mix[layer], sh1, sc1, w_in, seq, tm_mm, w_in.shape[1] // 3)
            wg = jnp.pad(_pad_heads(gla_w_g2[i], GLA_HEADS, GLA_DK),
                         ((0, LANES - GLA_GATE_RANK), (0, 0))).astype(BF16)
            bg = _pad_heads(gla_b_g2[i][None, :], GLA_HEADS, GLA_DK)
            oa = gla(u, cb, wg, bg, gla_onorm[i][None, :], tri, batch, seq, tm_seq)
            cw, cbias = ssd_conv_w[i], ssd_conv_b[i][None, :]
            perm = lambda v: jnp.stack([_pad_cols(v[0::2], LANES), _pad_cols(v[1::2], LANES)])
            npair = SSD_HEADS // 2
            hp = jnp.arange(SSD_INNER) // SSD_HEADDIM
            ex_e = (jnp.arange(LANES)[:, None] * 2 == hp[None, :]) & (jnp.arange(LANES)[:, None] < npair)
            ex_o = (jnp.arange(LANES)[:, None] * 2 + 1 == hp[None, :]) & (jnp.arange(LANES)[:, None] < npair)
            expand = jnp.stack([ex_e, ex_o]).astype(BF16)
            ob = ssd(u, cb, cw[:, :SSD_INNER], cbias[:, :SSD_INNER], cw[:, SSD_INNER:],
                     cbias[:, SSD_INNER:], perm(ssd_dt_bias[i]), perm(ssd_a_log[i]),
                     jnp.repeat(ssd_d[i], SSD_HEADDIM)[None, :], ssd_norm[i][None, :],
                     expand, tri, batch, seq, tm_seq)
            w_out = ev_w_out[i].astype(BF16)
            w1, w2 = w_out[:GLA_HEADS * GLA_DV], w_out[GLA_HEADS * GLA_DV:]
        else:
            w_in, cb = _odd_layout(od_w_in[i])
            u = inproj(h, norm_mix[layer], sh1, sc1, w_in, seq, tm_mm, w_in.shape[1] // 2)
            per = 4
            oa = lru(u, cb, lru_conv_w[i], lru_conv_b[i][None, :],
                     _block_diag(lru_w_a[i], per).astype(BF16), lru_b_a[i][None, :],
                     _block_diag(lru_w_x[i], per).astype(BF16), lru_b_x[i][None, :],
                     lru_lambda[i][None, :], batch, seq, tm_seq)
            wq = mla_w_q_up[i].reshape(MLA_Q_LORA, MLA_HEADS, MLA_QK)
            wq = jnp.concatenate([wq, _swap_halves(wq[..., MLA_NOPE:])], axis=-1)
            wq = wq.reshape(MLA_Q_LORA, MLA_HEADS * MLA_QK_PAD).astype(BF16)
            wkv = mla_w_kv_up[i].reshape(MLA_KV_LORA, MLA_HEADS, MLA_NOPE + MLA_V)
            wkv = jnp.concatenate([wkv[..., :MLA_NOPE].reshape(MLA_KV_LORA, -1),
                                   wkv[..., MLA_NOPE:].reshape(MLA_KV_LORA, -1)], axis=1).astype(BF16)
            ext_gain = lambda gq: jnp.concatenate([gq, _swap_halves(gq[MLA_NOPE:])])[None, :]
            fr = ROPE_THETA ** (-jnp.arange(0, MLA_ROPE, 2, dtype=F32) / MLA_ROPE)
            freq = jnp.tile(fr, 4)[None, :]
            q, k, v = mla_prep(u, cb, positions.reshape(t, 1), freq, mla_q_norm[i][None, :],
                               mla_kv_norm[i][None, :], wq, wkv, ext_gain(mla_q_qknorm[i]),
                               ext_gain(mla_k_qknorm[i]), tm_mm)
            ob = attention(q, k, v, batch, seq, tq)
            w_out = od_w_out[i].astype(BF16)
            w1, w2 = w_out[:d], w_out[d:]
        h, hf, logits_t = outproj(oa, ob, w1, w2, h, g1, norm_ffn[layer], sh2, sc2, rwt, seq, tm_mm)
        comb_t = route(logits_t, router_bias, tm_mm)
        comb = _pad_cols(comb_t.T, LANES)
        wgu = jnp.concatenate([moe_w_gate[layer], moe_w_up[layer]], axis=-1).astype(BF16)
        h = moe_dense(hf, comb, wgu, moe_w_down[layer].astype(BF16), h, g2, seq, tm_moe)
    return h.reshape(batch, seq, d)
```

```python
import functools
import math

import jax
import jax.numpy as jnp
from jax import lax
from jax.experimental import pallas as pl
from jax.experimental.pallas import tpu as pltpu

F32 = jnp.float32
BF16 = jnp.bfloat16

EPS = 1e-6
CHUNK = 64
CONV_K = 4
GLA_HEADS, GLA_DK, GLA_DV = 4, 64, 128
GLA_GATE_RANK, GLA_GATE_NORM = 16, 16.0
SSD_HEADS, SSD_HEADDIM, SSD_GROUPS, SSD_STATE = 16, 64, 2, 128
SSD_INNER = SSD_HEADS * SSD_HEADDIM
SSD_GINNER = SSD_INNER // SSD_GROUPS
LRU_BLOCKS, LRU_C = 16, 8.0
MLA_HEADS, MLA_NOPE, MLA_ROPE, MLA_V = 8, 128, 64, 128
MLA_QK = MLA_NOPE + MLA_ROPE
MLA_QK_PAD = 256
MLA_Q_LORA, MLA_KV_LORA = 384, 256
ROPE_THETA = 10000.0
N_EXPERTS, N_GROUPS, D_EXPERT = 16, 4, 512
EXPERTS_PER_GROUP = N_EXPERTS // N_GROUPS

TM_SEQ = 256
TM_MM = 512
TM_MOE = 512
TL_MOE = 512
ROW_CHUNK = 16
TQ_ATTN = 512
ATTN_HEADS_PER_STEP = 2

LANES = 128
VMEM_LIMIT = 48 * 1024 * 1024

NT_DIMS = (((1,), (1,)), ((), ()))
TN_DIMS = (((0,), (0,)), ((), ()))


def _params(*sem):
    return pltpu.CompilerParams(dimension_semantics=sem, vmem_limit_bytes=VMEM_LIMIT)


def _dot(a, b):
    return jnp.dot(a, b, preferred_element_type=F32)


def _dot_nt(a, b):
    return lax.dot_general(a, b, NT_DIMS, preferred_element_type=F32)


def _dot_tn(a, b):
    return lax.dot_general(a, b, TN_DIMS, preferred_element_type=F32)


def _split3(a):
    hi = a.astype(BF16)
    r1 = a - hi.astype(F32)
    mid = r1.astype(BF16)
    lo = (r1 - mid.astype(F32)).astype(BF16)
    return hi, mid, lo


def _dot_exact_rhs(a, b_bf16):
    hi, mid, lo = _split3(a)
    return _dot(hi, b_bf16) + _dot(mid, b_bf16) + _dot(lo, b_bf16)


def _dot_exact_lhs(a_bf16, b):
    hi, mid, lo = _split3(b)
    return _dot(a_bf16, hi) + _dot(a_bf16, mid) + _dot(a_bf16, lo)


def _softplus(x):
    return jnp.maximum(x, 0.0) + jnp.log1p(jnp.exp(-jnp.abs(x)))


def _silu(x):
    return x * jax.nn.sigmoid(x)


def _lane(shape):
    return lax.broadcasted_iota(jnp.int32, shape, len(shape) - 1)


def _row(shape):
    return lax.broadcasted_iota(jnp.int32, shape, len(shape) - 2)


def _ada_kernel(c_ref, w_ref, b_ref, o_ref):
    c = c_ref[...]
    a_hi, a_mid, a_lo = _split3(_silu(c))
    w_hi, w_mid, w_lo = _split3(w_ref[0])
    acc = (_dot(a_hi, w_hi) + _dot(a_hi, w_mid) + _dot(a_mid, w_hi)
           + _dot(a_hi, w_lo) + _dot(a_lo, w_hi) + _dot(a_mid, w_mid))
    o_ref[0] = acc + b_ref[0]


def ada_mod(c, ada_w, ada_b):
    depth, d, n = ada_w.shape
    b = c.shape[0]
    bp = 8
    cp = jnp.zeros((bp, d), F32).at[:b].set(c)
    tn = 1536
    out = pl.pallas_call(
        _ada_kernel,
        grid=(depth, n // tn),
        in_specs=[pl.BlockSpec((bp, d), lambda l, j: (0, 0)),
                  pl.BlockSpec((1, d, tn), lambda l, j: (l, 0, j)),
                  pl.BlockSpec((1, 1, tn), lambda l, j: (l, 0, j))],
        out_specs=pl.BlockSpec((1, bp, tn), lambda l, j: (l, 0, j)),
        out_shape=jax.ShapeDtypeStruct((depth, bp, n), F32),
        compiler_params=_params("arbitrary", "arbitrary"),
        name="ada_mod",
    )(cp, ada_w, ada_b.reshape(depth, 1, n))
    return out[:, :b]


def _inproj_kernel(x_ref, g_ref, sh_ref, sc_ref, w_ref, o_ref, hm_ref):
    @pl.when(pl.program_id(1) == 0)
    def _():
        x = x_ref[...]
        ms = jnp.mean(x * x, axis=-1, keepdims=True)
        y = x * lax.rsqrt(ms + EPS) * g_ref[...]
        hm_ref[...] = (y * (1.0 + sc_ref[0]) + sh_ref[0]).astype(BF16)

    o_ref[...] = _dot(hm_ref[...], w_ref[...]).astype(o_ref.dtype)


def inproj(h2d, gain, shift, scale, w, seq, tm, tn):
    t, d = h2d.shape
    n = w.shape[1]
    tpb = seq // tm
    return pl.pallas_call(
        _inproj_kernel,
        grid=(t // tm, n // tn),
        in_specs=[pl.BlockSpec((tm, d), lambda i, j: (i, 0)),
                  pl.BlockSpec((1, d), lambda i, j: (0, 0)),
                  pl.BlockSpec((1, 1, d), lambda i, j: (i // tpb, 0, 0)),
                  pl.BlockSpec((1, 1, d), lambda i, j: (i // tpb, 0, 0)),
                  pl.BlockSpec((d, tn), lambda i, j: (0, j))],
        out_specs=pl.BlockSpec((tm, tn), lambda i, j: (i, j)),
        out_shape=jax.ShapeDtypeStruct((t, n), BF16),
        scratch_shapes=[pltpu.VMEM((tm, d), BF16)],
        compiler_params=_params("arbitrary", "arbitrary"),
        name="inproj",
    )(h2d, gain.reshape(1, d), shift[:, None, :], scale[:, None, :], w)


def _gla_kernel(q_ref, k_ref, v_ref, og_ref, misc_ref, wg_ref, bg_ref, on_ref, tri_ref,
                o_ref, st_ref, cum_ref):
    tm = q_ref.shape[0]

    @pl.when(pl.program_id(1) == 0)
    def _():
        st_ref[...] = jnp.zeros_like(st_ref)

    g = _dot(misc_ref[...], wg_ref[...]) + bg_ref[...]
    la = (jnp.minimum(g, 0.0) - jnp.log1p(jnp.exp(-jnp.abs(g)))) * (1.0 / GLA_GATE_NORM)
    la = jnp.where((_lane(la.shape) & (LANES - 1)) < GLA_DK, la, 0.0)
    cum_ref[...] = _dot_exact_lhs(tri_ref[...], la)

    causal = _row((CHUNK, CHUNK)) >= _lane((CHUNK, CHUNK))
    for c in range(tm // CHUNK):
        rows = pl.ds(c * CHUNK, CHUNK)
        for h in range(GLA_HEADS):
            cols = pl.ds(h * LANES, LANES)
            cu = cum_ref[rows, cols]
            last = cu[CHUNK - 1:CHUNK, :]
            qh = q_ref[rows, cols].astype(F32) * (GLA_DK ** -0.5)
            kh = k_ref[rows, cols].astype(F32)
            q_dec = (qh * jnp.exp(cu)).astype(BF16)
            k_inv = (kh * jnp.exp(-cu)).astype(BF16)
            k_end = (kh * jnp.exp(last - cu)).astype(BF16)
            vh = v_ref[rows, cols]
            att = jnp.where(causal, _dot_nt(q_dec, k_inv), 0.0)
            st = st_ref[h]
            o = _dot(att.astype(BF16), vh) + _dot_nt(q_dec, st.astype(BF16))
            st_ref[h] = st * jnp.exp(last) + _dot_tn(vh, k_end)
            ms = jnp.mean(o * o, axis=-1, keepdims=True)
            on = o * lax.rsqrt(ms + EPS) * on_ref[:, cols]
            o_ref[rows, cols] = (on * _silu(og_ref[rows, cols].astype(F32))).astype(o_ref.dtype)


def gla(u, cb, wg, bg, onorm, tri, batch, seq, tm):
    t = u.shape[0]
    tpb = seq // tm
    w512 = GLA_HEADS * LANES
    row = lambda b, i: b * tpb + i
    return pl.pallas_call(
        _gla_kernel,
        grid=(batch, tpb),
        in_specs=[pl.BlockSpec((tm, w512), lambda b, i: (row(b, i), cb["q"])),
                  pl.BlockSpec((tm, w512), lambda b, i: (row(b, i), cb["k"])),
                  pl.BlockSpec((tm, w512), lambda b, i: (row(b, i), cb["v"])),
                  pl.BlockSpec((tm, w512), lambda b, i: (row(b, i), cb["og"])),
                  pl.BlockSpec((tm, LANES), lambda b, i: (row(b, i), cb["glr"])),
                  pl.BlockSpec((LANES, w512), lambda b, i: (0, 0)),
                  pl.BlockSpec((1, w512), lambda b, i: (0, 0)),
                  pl.BlockSpec((1, w512), lambda b, i: (0, 0)),
                  pl.BlockSpec((tm, tm), lambda b, i: (0, 0))],
        out_specs=pl.BlockSpec((tm, w512), lambda b, i: (row(b, i), 0)),
        out_shape=jax.ShapeDtypeStruct((t, w512), BF16),
        scratch_shapes=[pltpu.VMEM((GLA_HEADS, GLA_DV, LANES), F32),
                        pltpu.VMEM((tm, w512), F32)],
        compiler_params=_params("arbitrary", "arbitrary"),
        name="gla",
    )(u, u, u, u, u, wg, bg, onorm, tri)


def _causal_conv(ext_ref, x, w_ref, b_ref, first):
    tm = x.shape[0]

    @pl.when(first)
    def _():
        ext_ref[0:8, :] = jnp.zeros((8, ext_ref.shape[1]), F32)

    ext_ref[8:8 + tm, :] = x
    y = b_ref[...] + w_ref[CONV_K - 1:CONV_K, :] * x
    for kk in range(CONV_K - 1):
        off = 8 - (CONV_K - 1) + kk
        y = y + w_ref[kk:kk + 1, :] * ext_ref[off:off + tm, :]
    ext_ref[0:8, :] = ext_ref[tm:tm + 8, :]
    return y


def _ssd_kernel(xs_ref, z_ref, bc_ref, dte_ref, dto_ref,
                cwx_ref, cbx_ref, cwb_ref, cbb_ref, dtb_ref, alog_ref, dexp_ref, ng_ref,
                expand_ref, tri_ref, o_ref,
                extx_ref, extb_ref, st_ref, cume_ref, cumo_ref, cumx_ref, xdt_ref, bcc_ref, xsc_ref):
    tm = xs_ref.shape[0]
    first = pl.program_id(1) == 0

    @pl.when(first)
    def _():
        st_ref[...] = jnp.zeros_like(st_ref)

    xs = _silu(_causal_conv(extx_ref, xs_ref[...].astype(F32), cwx_ref, cbx_ref, first))
    bcv = _silu(_causal_conv(extb_ref, bc_ref[...].astype(F32), cwb_ref, cbb_ref, first))
    xsc_ref[...] = xs
    bcc_ref[...] = bcv.astype(BF16)

    npair = SSD_HEADS // 2
    lane = _lane((tm, LANES))
    valid = lane < npair
    a_e = -jnp.exp(alog_ref[0:1, :])
    a_o = -jnp.exp(alog_ref[1:2, :])
    dt_e = jnp.where(valid, _softplus(dte_ref[...].astype(F32) + dtb_ref[0:1, :]), 0.0)
    dt_o = jnp.where(valid, _softplus(dto_ref[...].astype(F32) + dtb_ref[1:2, :]), 0.0)
    tri = tri_ref[...]
    cum_e = _dot_exact_lhs(tri, dt_e * a_e)
    cum_o = _dot_exact_lhs(tri, dt_o * a_o)
    cume_ref[...] = cum_e
    cumo_ref[...] = cum_o
    ex_e = expand_ref[0]
    ex_o = expand_ref[1]
    cumx_ref[...] = _dot_exact_rhs(cum_e, ex_e) + _dot_exact_rhs(cum_o, ex_o)
    xdt_ref[...] = xs * (_dot_exact_rhs(dt_e, ex_e) + _dot_exact_rhs(dt_o, ex_o))

    rr = _row((CHUNK, LANES))
    ll = _lane((CHUNK, LANES))
    causal2 = rr >= (ll & (CHUNK - 1))
    left = ll < CHUNK
    gw = SSD_GINNER
    for c in range(tm // CHUNK):
        rows = pl.ds(c * CHUNK, CHUNK)
        cumx = cumx_ref[rows, :]
        lastx = cumx[CHUNK - 1:CHUNK, :]
        xdt = xdt_ref[rows, :]
        xw = (xdt * jnp.exp(lastx - cumx)).astype(BF16)
        ecum = jnp.exp(cumx)
        pt = jnp.concatenate([cume_ref[rows, :], cumo_ref[rows, :]], axis=0).T
        ys = []
        for g in range(SSD_GROUPS):
            bg = bcc_ref[rows, pl.ds(g * SSD_STATE, SSD_STATE)]
            cg = bcc_ref[rows, pl.ds((SSD_GROUPS + g) * SSD_STATE, SSD_STATE)]
            st = st_ref[g]
            y_off = _dot(cg, st.astype(BF16)) * ecum[:, g * gw:(g + 1) * gw]
            cbcb = _dot_nt(cg, jnp.concatenate([bg, bg], axis=0))
            parts = []
            for j in range(npair // SSD_GROUPS):
                jp = g * (npair // SSD_GROUPS) + j
                colp = cumx[:, jp * LANES:(jp + 1) * LANES]
                seg = colp - pt[jp:jp + 1, :]
                dec = jnp.exp(jnp.where(causal2, seg, -jnp.inf))
                w = (cbcb * dec).astype(BF16)
                xp = xdt[:, jp * LANES:(jp + 1) * LANES]
                x2 = jnp.concatenate([jnp.where(left, xp, 0.0), jnp.where(left, 0.0, xp)],
                                     axis=0).astype(BF16)
                parts.append(_dot(w, x2))
            ys.append(jnp.concatenate(parts, axis=1) + y_off)
            st_ref[g] = (st * jnp.exp(lastx[:, g * gw:(g + 1) * gw])
                         + _dot_tn(bg, xw[:, g * gw:(g + 1) * gw]))
        y = jnp.concatenate(ys, axis=1)
        y = y + dexp_ref[...] * xsc_ref[rows, :]
        y = y * _silu(z_ref[rows, :].astype(F32))
        for g in range(SSD_GROUPS):
            yg = y[:, g * gw:(g + 1) * gw]
            ms = jnp.mean(yg * yg, axis=-1, keepdims=True)
            o_ref[rows, pl.ds(g * gw, gw)] = (
                yg * lax.rsqrt(ms + EPS) * ng_ref[:, g * gw:(g + 1) * gw]).astype(o_ref.dtype)


def ssd(u, cb, cwx, cbx, cwb, cbb, dtb, alog, dexp, ng, expand, tri, batch, seq, tm):
    t = u.shape[0]
    tpb = seq // tm
    row = lambda b, i: b * tpb + i
    bcw = 2 * SSD_GROUPS * SSD_STATE
    const = lambda shape: pl.BlockSpec(shape, lambda b, i: tuple(0 for _ in shape))
    return pl.pallas_call(
        _ssd_kernel,
        grid=(batch, tpb),
        in_specs=[pl.BlockSpec((tm, SSD_INNER), lambda b, i: (row(b, i), cb["xs"])),
                  pl.BlockSpec((tm, SSD_INNER), lambda b, i: (row(b, i), cb["z"])),
                  pl.BlockSpec((tm, bcw), lambda b, i: (row(b, i), cb["bc"])),
                  pl.BlockSpec((tm, LANES), lambda b, i: (row(b, i), cb["dte"])),
                  pl.BlockSpec((tm, LANES), lambda b, i: (row(b, i), cb["dto"])),
                  const((CONV_K, SSD_INNER)), const((1, SSD_INNER)),
                  const((CONV_K, bcw)), const((1, bcw)),
                  const((2, LANES)), const((2, LANES)),
                  const((1, SSD_INNER)), const((1, SSD_INNER)),
                  const((2, LANES, SSD_INNER)), const((tm, tm))],
        out_specs=pl.BlockSpec((tm, SSD_INNER), lambda b, i: (row(b, i), 0)),
        out_shape=jax.ShapeDtypeStruct((t, SSD_INNER), BF16),
        scratch_shapes=[pltpu.VMEM((tm + 8, SSD_INNER), F32),
                        pltpu.VMEM((tm + 8, bcw), F32),
                        pltpu.VMEM((SSD_GROUPS, SSD_STATE, SSD_GINNER), F32),
                        pltpu.VMEM((tm, LANES), F32),
                        pltpu.VMEM((tm, LANES), F32),
                        pltpu.VMEM((tm, SSD_INNER), F32),
                        pltpu.VMEM((tm, SSD_INNER), F32),
                        pltpu.VMEM((tm, bcw), BF16),
                        pltpu.VMEM((tm, SSD_INNER), F32)],
        compiler_params=_params("arbitrary", "arbitrary"),
        name="ssd",
    )(u, u, u, u, u, cwx, cbx, cwb, cbb, dtb, alog, dexp, ng, expand, tri)


def _gelu_tanh(x):
    return 0.5 * x * (1.0 + jnp.tanh(math.sqrt(2.0 / math.pi) * (x + 0.044715 * (x * x * x))))


def _lru_kernel(gate_ref, xr_ref, cw_ref, cb_ref, wa_ref, ba_ref, wx_ref, bx_ref, lam_ref,
                o_ref, ext_ref, carry_ref):
    tm, width = xr_ref.shape
    first = pl.program_id(1) == 0

    @pl.when(first)
    def _():
        carry_ref[...] = jnp.zeros_like(carry_ref)

    x = _causal_conv(ext_ref, xr_ref[...].astype(F32), cw_ref, cb_ref, first)
    xb = x.astype(BF16)
    nblk = wa_ref.shape[0]
    bw = width // nblk
    ra = jnp.concatenate([_dot(xb[:, n * bw:(n + 1) * bw], wa_ref[n]) for n in range(nblk)], axis=1)
    rx = jnp.concatenate([_dot(xb[:, n * bw:(n + 1) * bw], wx_ref[n]) for n in range(nblk)], axis=1)
    r = jax.nn.sigmoid(ra + ba_ref[...])
    ig = jax.nn.sigmoid(rx + bx_ref[...])
    log_a = (-LRU_C) * r * _softplus(-lam_ref[...])
    a = jnp.exp(log_a)
    u = jnp.sqrt(1.0 - jnp.exp(2.0 * log_a)) * (ig * x)

    rowi = _row((tm, width))
    d = 1
    while d < tm:
        m = rowi >= d
        a_s = pltpu.roll(a, d, 0)
        u_s = pltpu.roll(u, d, 0)
        u = jnp.where(m, a * u_s + u, u)
        a = jnp.where(m, a * a_s, a)
        d *= 2
    hseq = a * carry_ref[0:1, :] + u
    carry_ref[...] = jnp.broadcast_to(hseq[tm - 1:tm, :], carry_ref.shape)
    o_ref[...] = (hseq * _gelu_tanh(gate_ref[...].astype(F32))).astype(o_ref.dtype)


def lru(u, cb, cw, cbias, wa, ba, wx, bx, lam, batch, seq, tm):
    t = u.shape[0]
    width = cw.shape[1]
    tpb = seq // tm
    row = lambda b, i: b * tpb + i
    const = lambda shape: pl.BlockSpec(shape, lambda b, i: tuple(0 for _ in shape))
    return pl.pallas_call(
        _lru_kernel,
        grid=(batch, tpb),
        in_specs=[pl.BlockSpec((tm, width), lambda b, i: (row(b, i), cb["gate"])),
                  pl.BlockSpec((tm, width), lambda b, i: (row(b, i), cb["xr"])),
                  const((CONV_K, width)), const((1, width)),
                  const(wa.shape), const((1, width)),
                  const(wx.shape), const((1, width)), const((1, width))],
        out_specs=pl.BlockSpec((tm, width), lambda b, i: (row(b, i), 0)),
        out_shape=jax.ShapeDtypeStruct((t, width), BF16),
        scratch_shapes=[pltpu.VMEM((tm + 8, width), F32),
                        pltpu.VMEM((8, width), F32)],
        compiler_params=_params("arbitrary", "arbitrary"),
        name="lru",
    )(u, u, cw, cbias, wa, ba, wx, bx, lam)


def _mla_prep_kernel(uq_ref, ukv_ref, krr_ref, pos_ref, freq_ref, qn_ref, kvn_ref,
                     wq_ref, wkv_ref, qg_ref, kg_ref, q_ref, k_ref, v_ref):
    tm = uq_ref.shape[0]
    lane = _lane((tm, LANES))
    lo_half = lane < MLA_ROPE

    ang = pos_ref[...].astype(F32) * freq_ref[...]
    cs = jnp.where(lo_half, jnp.cos(ang),
                   jnp.where(lane < MLA_ROPE + MLA_ROPE // 2, -jnp.sin(ang), jnp.sin(ang)))

    def latent_norm(ref, g_ref):
        x = ref[...].astype(F32)
        ms = jnp.mean(x * x, axis=-1, keepdims=True)
        return (x * lax.rsqrt(ms + EPS) * g_ref[...]).astype(BF16)

    qf = _dot(latent_norm(uq_ref, qn_ref), wq_ref[...])
    kvf = _dot(latent_norm(ukv_ref, kvn_ref), wkv_ref[...])
    krr = krr_ref[...].astype(F32)
    kr_ss = jnp.sum(jnp.where(lo_half, krr * krr, 0.0), axis=-1, keepdims=True)
    scale = MLA_QK ** -0.5

    def rope_half(y2):
        t = y2 * cs
        return jnp.where(lo_half, t + pltpu.roll(t, MLA_ROPE, 1), 0.0)

    for h in range(MLA_HEADS):
        x1 = qf[:, h * MLA_QK_PAD:h * MLA_QK_PAD + LANES]
        x2 = qf[:, h * MLA_QK_PAD + LANES:(h + 1) * MLA_QK_PAD]
        ss = (jnp.sum(x1 * x1, axis=-1, keepdims=True)
              + jnp.sum(jnp.where(lo_half, x2 * x2, 0.0), axis=-1, keepdims=True))
        r = lax.rsqrt(ss * (1.0 / MLA_QK) + EPS) * scale
        q_ref[:, pl.ds(h * MLA_QK_PAD, LANES)] = (x1 * r * qg_ref[:, 0:LANES]).astype(q_ref.dtype)
        q_ref[:, pl.ds(h * MLA_QK_PAD + LANES, LANES)] = rope_half(
            x2 * r * qg_ref[:, LANES:2 * LANES]).astype(q_ref.dtype)

        kn = kvf[:, h * LANES:(h + 1) * LANES]
        ssk = jnp.sum(kn * kn, axis=-1, keepdims=True) + kr_ss
        rk = lax.rsqrt(ssk * (1.0 / MLA_QK) + EPS)
        k_ref[:, pl.ds(h * MLA_QK_PAD, LANES)] = (kn * rk * kg_ref[:, 0:LANES]).astype(k_ref.dtype)
        k_ref[:, pl.ds(h * MLA_QK_PAD + LANES, LANES)] = rope_half(
            krr * rk * kg_ref[:, LANES:2 * LANES]).astype(k_ref.dtype)
        v_ref[:, pl.ds(h * MLA_V, MLA_V)] = kvf[:, (MLA_HEADS + h) * LANES:
                                                (MLA_HEADS + h + 1) * LANES].astype(v_ref.dtype)


def mla_prep(u, cb, pos2d, freq, qn, kvn, wq, wkv, qg, kg, tm):
    t = u.shape[0]
    const = lambda shape: pl.BlockSpec(shape, lambda i: tuple(0 for _ in shape))
    hq = MLA_HEADS * MLA_QK_PAD
    hv = MLA_HEADS * MLA_V
    return pl.pallas_call(
        _mla_prep_kernel,
        grid=(t // tm,),
        in_specs=[pl.BlockSpec((tm, MLA_Q_LORA), lambda i: (i, cb["uq"])),
                  pl.BlockSpec((tm, MLA_KV_LORA), lambda i: (i, cb["ukv"])),
                  pl.BlockSpec((tm, LANES), lambda i: (i, cb["krr"])),
                  pl.BlockSpec((tm, 1), lambda i: (i, 0)),
                  const((1, LANES)), const((1, MLA_Q_LORA)), const((1, MLA_KV_LORA)),
                  const(wq.shape), const(wkv.shape), const((1, 2 * LANES)), const((1, 2 * LANES))],
        out_specs=[pl.BlockSpec((tm, hq), lambda i: (i, 0)),
                   pl.BlockSpec((tm, hq), lambda i: (i, 0)),
                   pl.BlockSpec((tm, hv), lambda i: (i, 0))],
        out_shape=[jax.ShapeDtypeStruct((t, hq), BF16),
                   jax.ShapeDtypeStruct((t, hq), BF16),
                   jax.ShapeDtypeStruct((t, hv), BF16)],
        compiler_params=_params("arbitrary"),
        name="mla_prep",
    )(u, u, u, pos2d, freq, qn, kvn, wq, wkv, qg, kg)


def _attn_kernel(q_ref, k_ref, v_ref, o_ref, m_ref, l_ref, acc_ref):
    i = pl.program_id(2)
    tq = q_ref.shape[0]
    nh = q_ref.shape[1] // MLA_QK_PAD
    nc = tq // LANES

    m_ref[...] = jnp.full_like(m_ref, -jnp.inf)
    l_ref[...] = jnp.zeros_like(l_ref)
    acc_ref[...] = jnp.zeros_like(acc_ref)

    def update(g, s, v):
        m_prev = m_ref[g]
        m_cur = s[:, 0:LANES]
        for c in range(1, nc):
            m_cur = jnp.maximum(m_cur, s[:, c * LANES:(c + 1) * LANES])
        m_new = jnp.maximum(m_prev, jnp.max(m_cur, axis=-1, keepdims=True))
        alpha = jnp.exp(m_prev - m_new)
        ps = [jnp.exp(s[:, c * LANES:(c + 1) * LANES] - m_new) for c in range(nc)]
        psum = ps[0]
        for c in range(1, nc):
            psum = psum + ps[c]
        l_ref[g] = alpha * l_ref[g] + psum
        p = jnp.concatenate(ps, axis=1).astype(BF16)
        acc_ref[g] = alpha * acc_ref[g] + _dot(p, v)
        m_ref[g] = m_new

    def block(j, masked):
        rows = pl.ds(pl.multiple_of(j * tq, tq), tq)
        for g in range(nh):
            s = _dot_nt(q_ref[:, g * MLA_QK_PAD:(g + 1) * MLA_QK_PAD],
                        k_ref[rows, pl.ds(g * MLA_QK_PAD, MLA_QK_PAD)])
            if masked:
                s = jnp.where(_row((tq, tq)) >= _lane((tq, tq)), s, -jnp.inf)
            update(g, s, v_ref[rows, pl.ds(g * MLA_V, MLA_V)])

    def body(j, carry):
        block(j, False)
        return carry

    lax.fori_loop(0, i, body, 0)
    block(i, True)
    for g in range(nh):
        l = jnp.sum(l_ref[g], axis=-1, keepdims=True)
        o_ref[:, pl.ds(g * MLA_V, MLA_V)] = (acc_ref[g] / l).astype(o_ref.dtype)


def attention(q, k, v, batch, seq, tq):
    t = q.shape[0]
    nq = seq // tq
    nh = ATTN_HEADS_PER_STEP
    return pl.pallas_call(
        _attn_kernel,
        grid=(batch, MLA_HEADS // nh, nq),
        in_specs=[pl.BlockSpec((tq, nh * MLA_QK_PAD), lambda b, h, i: (b * nq + i, h)),
                  pl.BlockSpec((seq, nh * MLA_QK_PAD), lambda b, h, i: (b, h)),
                  pl.BlockSpec((seq, nh * MLA_V), lambda b, h, i: (b, h))],
        out_specs=pl.BlockSpec((tq, nh * MLA_V), lambda b, h, i: (b * nq + i, h)),
        out_shape=jax.ShapeDtypeStruct((t, MLA_HEADS * MLA_V), BF16),
        scratch_shapes=[pltpu.VMEM((nh, tq, LANES), F32), pltpu.VMEM((nh, tq, LANES), F32),
                        pltpu.VMEM((nh, tq, MLA_V), F32)],
        compiler_params=_params("arbitrary", "arbitrary", "arbitrary"),
        name="attention",
    )(q, k, v)


def _outproj_kernel(oa_ref, ob_ref, w1_ref, w2_ref, h_ref, g1_ref, nf_ref, sh_ref, sc_ref,
                    rwt_ref, hn_ref, hf_ref, lg_ref):
    y = _dot(oa_ref[...], w1_ref[...]) + _dot(ob_ref[...], w2_ref[...])
    hn = h_ref[...] + g1_ref[0] * y
    hn_ref[...] = hn
    ms = jnp.mean(hn * hn, axis=-1, keepdims=True)
    hf = hn * lax.rsqrt(ms + EPS) * nf_ref[...] * (1.0 + sc_ref[0]) + sh_ref[0]
    hf_ref[...] = hf.astype(BF16)
    r_hi, r_mid, r_lo = _split3(rwt_ref[...])
    f_hi, f_mid, f_lo = _split3(hf)
    lg_ref[...] = (_dot_nt(r_hi, f_hi) + _dot_nt(r_hi, f_mid) + _dot_nt(r_mid, f_hi)
                   + _dot_nt(r_hi, f_lo) + _dot_nt(r_lo, f_hi) + _dot_nt(r_mid, f_mid))


def outproj(oa, ob, w1, w2, h2d, g1, nf, sh2, sc2, rwt, seq, tm):
    t, d = h2d.shape
    tpb = seq // tm
    const = lambda shape: pl.BlockSpec(shape, lambda i: tuple(0 for _ in shape))
    bvec = pl.BlockSpec((1, 1, d), lambda i: (i // tpb, 0, 0))
    return pl.pallas_call(
        _outproj_kernel,
        grid=(t // tm,),
        in_specs=[pl.BlockSpec((tm, oa.shape[1]), lambda i: (i, 0)),
                  pl.BlockSpec((tm, ob.shape[1]), lambda i: (i, 0)),
                  const(w1.shape), const(w2.shape),
                  pl.BlockSpec((tm, d), lambda i: (i, 0)),
                  bvec, const((1, d)), bvec, bvec, const(rwt.shape)],
        out_specs=[pl.BlockSpec((tm, d), lambda i: (i, 0)),
                   pl.BlockSpec((tm, d), lambda i: (i, 0)),
                   pl.BlockSpec((N_EXPERTS, tm), lambda i: (0, i))],
        out_shape=[jax.ShapeDtypeStruct((t, d), F32),
                   jax.ShapeDtypeStruct((t, d), BF16),
                   jax.ShapeDtypeStruct((N_EXPERTS, t), F32)],
        compiler_params=_params("arbitrary"),
        name="outproj",
    )(oa, ob, w1, w2, h2d, g1[:, None, :], nf.reshape(1, d), sh2[:, None, :], sc2[:, None, :], rwt)


def _route_sort_kernel(lg_ref, bias_ref, hf_ref, utri_ref, ltri_ref,
                       xs_ref, lpos_ref, wts_ref, cnt_ref):
    scores = jax.nn.sigmoid(lg_ref[...])
    sel = scores + bias_ref[...]
    tm = sel.shape[1]
    eidx = _row((N_EXPERTS, tm))
    neg = -jnp.inf

    best_score = None
    best_grp = None
    for g in range(N_GROUPS):
        m = [sel[g * EXPERTS_PER_GROUP + i:g * EXPERTS_PER_GROUP + i + 1, :]
             for i in range(EXPERTS_PER_GROUP)]
        gs = None
        for a in range(EXPERTS_PER_GROUP):
            for b in range(a + 1, EXPERTS_PER_GROUP):
                pair = m[a] + m[b]
                gs = pair if gs is None else jnp.maximum(gs, pair)
        if best_score is None:
            best_score, best_grp = gs, jnp.zeros_like(gs, dtype=jnp.int32)
        else:
            better = gs > best_score
            best_score = jnp.where(better, gs, best_score)
            best_grp = jnp.where(better, g, best_grp)

    masked = jnp.where((eidx >> 2) == best_grp, sel, neg)
    m1 = jnp.max(masked, axis=0, keepdims=True)
    i1 = jnp.min(jnp.where(masked == m1, eidx, N_EXPERTS), axis=0, keepdims=True)
    masked2 = jnp.where(eidx == i1, neg, masked)
    m2 = jnp.max(masked2, axis=0, keepdims=True)
    i2 = jnp.min(jnp.where(masked2 == m2, eidx, N_EXPERTS), axis=0, keepdims=True)
    pick1 = eidx == i1
    pick2 = eidx == i2
    s1 = jnp.sum(jnp.where(pick1, scores, 0.0), axis=0, keepdims=True)
    s2 = jnp.sum(jnp.where(pick2, scores, 0.0), axis=0, keepdims=True)
    tot = s1 + s2
    wts_ref[...] = jnp.concatenate([s1 / tot, s2 / tot], axis=0)

    nr = xs_ref.shape[0]
    picks = jnp.where(pick1, 1.0, jnp.where(pick2, 1.0, 0.0))
    csum = _dot(picks.astype(BF16), utri_ref[...])
    cnt = jnp.sum(picks, axis=1, keepdims=True)
    cnt_ref[0] = jnp.broadcast_to(cnt, (N_EXPERTS, LANES))
    cnt_pad = jnp.floor((cnt + (ROW_CHUNK - 1.0)) * (1.0 / ROW_CHUNK)) * ROW_CHUNK
    seg_off = _dot(ltri_ref[...], jnp.broadcast_to(cnt_pad, (N_EXPERTS, LANES)).astype(BF16))[:, 0:1]
    lposmat = seg_off + csum - 1.0
    lp1 = jnp.sum(jnp.where(pick1, lposmat, 0.0), axis=0, keepdims=True).astype(jnp.int32)
    lp2 = jnp.sum(jnp.where(pick2, lposmat, 0.0), axis=0, keepdims=True).astype(jnp.int32)
    lpos_ref[...] = jnp.concatenate([lp1, lp2], axis=0)
    rowi = _row((nr, tm))
    onehot = jnp.where(rowi == lp1, 1.0, jnp.where(rowi == lp2, 1.0, 0.0)).astype(BF16)
    xs_ref[...] = _dot(onehot, hf_ref[...]).astype(xs_ref.dtype)


def route_sort(logits_t, router_bias, hf, tl, nr):
    e, t = logits_t.shape
    d = hf.shape[1]
    r = jnp.arange(tl)
    utri = (r[:, None] <= r[None, :]).astype(BF16)
    re = jnp.arange(e)
    ltri = (re[None, :] < re[:, None]).astype(BF16)
    const = lambda shape: pl.BlockSpec(shape, lambda i: tuple(0 for _ in shape))
    return pl.pallas_call(
        _route_sort_kernel,
        grid=(t // tl,),
        in_specs=[pl.BlockSpec((e, tl), lambda i: (0, i)),
                  const((e, 1)),
                  pl.BlockSpec((tl, d), lambda i: (i, 0)),
                  const((tl, tl)), const((e, e))],
        out_specs=[pl.BlockSpec((nr, d), lambda i: (i, 0)),
                   pl.BlockSpec((2, tl), lambda i: (0, i)),
                   pl.BlockSpec((2, tl), lambda i: (0, i)),
                   pl.BlockSpec((1, e, LANES), lambda i: (i, 0, 0))],
        out_shape=[jax.ShapeDtypeStruct((t // tl * nr, d), BF16),
                   jax.ShapeDtypeStruct((2, t), jnp.int32),
                   jax.ShapeDtypeStruct((2, t), F32),
                   jax.ShapeDtypeStruct((t // tl, e, LANES), F32)],
        compiler_params=_params("arbitrary"),
        name="route_sort",
    )(logits_t, router_bias.reshape(e, 1).astype(F32), hf, utri, ltri)


def _moe_plan(cnt, nr, tmg, nt_max):
    ntile, ne = cnt.shape
    cpt = tmg // ROW_CHUNK
    nch = (cnt + ROW_CHUNK - 1) // ROW_CHUNK
    seg_off = jnp.cumsum(nch, axis=1) - nch
    cum = jnp.cumsum(nch, axis=0)
    tot = cum[-1]
    padded = (tot + cpt - 1) // cpt * cpt
    gend = jnp.cumsum(padded)
    gstart = gend - padded
    c = jnp.arange(nt_max * cpt, dtype=jnp.int32)
    e_raw = jnp.sum(c[:, None] >= gend[None, :], axis=1).astype(jnp.int32)
    e_c = jnp.minimum(e_raw, ne - 1)
    lc = c - gstart[e_c]
    valid = (e_raw < ne) & (lc < tot[e_c])
    tile_c = jnp.minimum(jnp.sum(lc[:, None] >= cum.T[e_c], axis=1), ntile - 1).astype(jnp.int32)
    k = lc - (cum[tile_c, e_c] - nch[tile_c, e_c])
    src = jnp.where(valid, tile_c * (nr // ROW_CHUNK) + seg_off[tile_c, e_c] + k, 0) * ROW_CHUNK
    first = jnp.arange(nt_max, dtype=jnp.int32) * cpt
    tile_on = (e_raw[first] < ne).astype(jnp.int32)
    return src.astype(jnp.int32), valid.astype(jnp.int32), e_c[first], tile_on


def _moe_group_kernel(src_ref, val_ref, te_ref, on_ref, xs_hbm, wgu_ref, wd_ref, yinit_hbm,
                      y_hbm, xbuf, ybuf, in_sem, out_sem):
    del te_ref, yinit_hbm
    g = pl.program_id(0)
    ng = pl.num_programs(0)
    slot = lax.rem(g, 2)
    tmg = xbuf.shape[1]
    cpt = tmg // ROW_CHUNK

    def in_copy(tile, sl, k):
        row = pl.multiple_of(src_ref[tile * cpt + k], ROW_CHUNK)
        return pltpu.make_async_copy(xs_hbm.at[pl.ds(row, ROW_CHUNK), :],
                                     xbuf.at[sl, pl.ds(k * ROW_CHUNK, ROW_CHUNK), :], in_sem.at[sl])

    def out_copy(tile, sl, k):
        row = pl.multiple_of(src_ref[tile * cpt + k], ROW_CHUNK)
        return pltpu.make_async_copy(ybuf.at[sl, pl.ds(k * ROW_CHUNK, ROW_CHUNK), :],
                                     y_hbm.at[pl.ds(row, ROW_CHUNK), :], out_sem.at[sl])

    def start_in(tile, sl):
        for k in range(cpt):
            in_copy(tile, sl, k).start()

    def wait_in(tile, sl):
        for k in range(cpt):
            in_copy(tile, sl, k).wait()

    def for_valid_out(tile, sl, fn):
        for k in range(cpt):
            @pl.when(val_ref[tile * cpt + k] == 1)
            def _():
                fn(out_copy(tile, sl, k))

    @pl.when((g >= 2) & (on_ref[jnp.maximum(g - 2, 0)] == 1))
    def _():
        for_valid_out(g - 2, slot, lambda cp: cp.wait())

    @pl.when((g == 0) & (on_ref[0] == 1))
    def _():
        start_in(0, 0)

    @pl.when((g + 1 < ng) & (on_ref[jnp.minimum(g + 1, ng - 1)] == 1))
    def _():
        start_in(g + 1, 1 - slot)

    @pl.when(on_ref[g] == 1)
    def _():
        wait_in(g, slot)
        gu = _dot(xbuf[slot], wgu_ref[0])
        de = gu.shape[1] // 2
        hid = (_silu(gu[:, :de]) * gu[:, de:]).astype(BF16)
        ybuf[slot] = _dot(hid, wd_ref[0]).astype(ybuf.dtype)
        for_valid_out(g, slot, lambda cp: cp.start())

    @pl.when(g == ng - 1)
    def _():
        @pl.when((ng >= 2) & (on_ref[jnp.maximum(g - 1, 0)] == 1))
        def _():
            for_valid_out(g - 1, 1 - slot, lambda cp: cp.wait())

        @pl.when(on_ref[g] == 1)
        def _():
            for_valid_out(g, slot, lambda cp: cp.wait())


def moe_group(xs, wgu, wd, plan, tmg, nt_max):
    src, valid, tile_e, tile_on = plan
    rows, d = xs.shape
    grid_spec = pltpu.PrefetchScalarGridSpec(
        num_scalar_prefetch=4,
        grid=(nt_max,),
        in_specs=[pl.BlockSpec(memory_space=pl.ANY),
                  pl.BlockSpec((1,) + wgu.shape[1:], lambda g, s, v, te, on: (te[g], 0, 0)),
                  pl.BlockSpec((1,) + wd.shape[1:], lambda g, s, v, te, on: (te[g], 0, 0)),
                  pl.BlockSpec(memory_space=pl.ANY)],
        out_specs=pl.BlockSpec(memory_space=pl.ANY),
        scratch_shapes=[pltpu.VMEM((2, tmg, d), BF16), pltpu.VMEM((2, tmg, d), BF16),
                        pltpu.SemaphoreType.DMA((2,)), pltpu.SemaphoreType.DMA((2,))])
    return pl.pallas_call(
        _moe_group_kernel,
        grid_spec=grid_spec,
        out_shape=jax.ShapeDtypeStruct((rows, d), BF16),
        input_output_aliases={7: 0},
        compiler_params=_params("arbitrary"),
        name="moe_group",
    )(src, valid, tile_e, tile_on, xs, wgu, wd, jnp.zeros((rows, d), BF16))


def _moe_combine_kernel(y_ref, lpos_ref, wts_ref, h_ref, g2_ref, o_ref):
    nr = y_ref.shape[0]
    tl = h_ref.shape[0]
    rowi = _row((nr, tl))
    wc = (jnp.where(rowi == lpos_ref[0:1, :], wts_ref[0:1, :], 0.0)
          + jnp.where(rowi == lpos_ref[1:2, :], wts_ref[1:2, :], 0.0)).astype(BF16)
    o_ref[...] = h_ref[...] + g2_ref[0] * _dot_tn(wc, y_ref[...])


def moe_combine(y, lpos, wts, h2d, g2, seq, tl, nr):
    t, d = h2d.shape
    tpb = seq // tl
    return pl.pallas_call(
        _moe_combine_kernel,
        grid=(t // tl,),
        in_specs=[pl.BlockSpec((nr, d), lambda i: (i, 0)),
                  pl.BlockSpec((2, tl), lambda i: (0, i)),
                  pl.BlockSpec((2, tl), lambda i: (0, i)),
                  pl.BlockSpec((tl, d), lambda i: (i, 0)),
                  pl.BlockSpec((1, 1, d), lambda i: (i // tpb, 0, 0))],
        out_specs=pl.BlockSpec((tl, d), lambda i: (i, 0)),
        out_shape=jax.ShapeDtypeStruct((t, d), F32),
        compiler_params=_params("arbitrary"),
        name="moe_combine",
    )(y, lpos, wts, h2d, g2[:, None, :])


def _pad_heads(w, heads, dh):
    lead = w.shape[:-1]
    w = w.reshape(lead + (heads, dh))
    w = jnp.pad(w, [(0, 0)] * len(lead) + [(0, 0), (0, LANES - dh)])
    return w.reshape(lead + (heads * LANES,))


def _pad_cols(w, n):
    return jnp.pad(w, [(0, 0)] * (w.ndim - 1) + [(0, n - w.shape[-1])])


def _tri_blocks(tm):
    r = jnp.arange(tm)
    return ((r[:, None] >= r[None, :]) & (r[:, None] // CHUNK == r[None, :] // CHUNK)).astype(BF16)


def _even_layout(w_in):
    sizes = (GLA_HEADS * GLA_DK, GLA_HEADS * GLA_DK, GLA_HEADS * GLA_DV, GLA_GATE_RANK,
             GLA_HEADS * GLA_DV, SSD_INNER, SSD_INNER + 2 * SSD_GROUPS * SSD_STATE, SSD_HEADS)
    offs = [0]
    for s in sizes:
        offs.append(offs[-1] + s)
    seg = lambda i: w_in[:, offs[i]:offs[i + 1]]
    q, k, v, glr, og, z, xbc, dt = (seg(i) for i in range(8))
    xs, bc = xbc[:, :SSD_INNER], xbc[:, SSD_INNER:]
    cols = [xs, z, v, og, _pad_heads(q, GLA_HEADS, GLA_DK), _pad_heads(k, GLA_HEADS, GLA_DK), bc,
            _pad_cols(glr, LANES), _pad_cols(dt[:, 0::2], LANES), _pad_cols(dt[:, 1::2], LANES)]
    w = jnp.concatenate(cols, axis=1).astype(BF16)
    cb = {"xs": 0, "z": 1, "v": 4, "og": 5, "q": 6, "k": 7, "bc": 8, "glr": 36, "dte": 37, "dto": 38}
    return w, cb


def _odd_layout(w_in):
    d = w_in.shape[0]
    o = [0, d, 2 * d, 2 * d + MLA_Q_LORA, 2 * d + MLA_Q_LORA + MLA_KV_LORA,
         2 * d + MLA_Q_LORA + MLA_KV_LORA + MLA_ROPE]
    gate, xr, uq, ukv, kr = (w_in[:, o[i]:o[i + 1]] for i in range(5))
    half = MLA_ROPE // 2
    kr_sw = jnp.concatenate([kr[:, half:], kr[:, :half]], axis=1)
    w = jnp.concatenate([gate, xr, ukv, uq, kr, kr_sw], axis=1).astype(BF16)
    cb = {"gate": 0, "xr": 1, "ukv": 8, "uq": 6, "krr": 21}
    return w, cb


def _swap_halves(x):
    half = x.shape[-1] // 2
    return jnp.concatenate([x[..., half:], x[..., :half]], axis=-1)


def _block_diag(w, per):
    nb, bw, _ = w.shape
    w = w.reshape(nb // per, per, bw, bw)
    eye = jnp.eye(per, dtype=w.dtype)
    out = jnp.einsum("gpij,pq->gpiqj", w, eye)
    return out.reshape(nb // per, per * bw, per * bw)


def kernel(x, c, positions, router_w, router_bias, ada_w, ada_b, norm_mix, norm_ffn, moe_w_gate, moe_w_up, moe_w_down, ev_w_in, gla_w_g2, gla_b_g2, gla_onorm, ssd_conv_w, ssd_conv_b, ssd_dt_bias, ssd_a_log, ssd_d, ssd_norm, ev_w_out, od_w_in, lru_conv_w, lru_conv_b, lru_w_a, lru_b_a, lru_w_x, lru_b_x, lru_lambda, mla_q_norm, mla_w_q_up, mla_kv_norm, mla_w_kv_up, mla_q_qknorm, mla_k_qknorm, od_w_out):
    batch, seq, d = x.shape
    t = batch * seq
    depth = ada_w.shape[0]
    tm_seq = min(TM_SEQ, seq)
    tm_mm = min(TM_MM, seq)
    tq = min(TQ_ATTN, seq)
    tl = min(TL_MOE, seq)
    nr = 2 * tl + N_EXPERTS * ROW_CHUNK
    tmg = TM_MOE
    chunks_max = 2 * t // ROW_CHUNK + (t // tl) * N_EXPERTS + N_EXPERTS * (tmg // ROW_CHUNK - 1)
    nt_max = -(-chunks_max // (tmg // ROW_CHUNK))

    mod = ada_mod(c, ada_w, ada_b)
    rwt = router_w.T.astype(F32)
    tri = _tri_blocks(tm_seq)
    h = x.reshape(t, d)

    for layer in range(depth):
        sh1, sc1, g1, sh2, sc2, g2 = (mod[layer, :, i * d:(i + 1) * d] for i in range(6))
        i = layer // 2
        if layer % 2 == 0:
            w_in, cb = _even_layout(ev_w_in[i])
            u = inproj(h, norm_mix[layer], sh1, sc1, w_in, seq, tm_mm, w_in.shape[1] // 3)
            wg = jnp.pad(_pad_heads(gla_w_g2[i], GLA_HEADS, GLA_DK),
                         ((0, LANES - GLA_GATE_RANK), (0, 0))).astype(BF16)
            bg = _pad_heads(gla_b_g2[i][None, :], GLA_HEADS, GLA_DK)
            oa = gla(u, cb, wg, bg, gla_onorm[i][None, :], tri, batch, seq, tm_seq)
            cw, cbias = ssd_conv_w[i], ssd_conv_b[i][None, :]
            perm = lambda v: jnp.stack([_pad_cols(v[0::2], LANES), _pad_cols(v[1::2], LANES)])
            npair = SSD_HEADS // 2
            hp = jnp.arange(SSD_INNER) // SSD_HEADDIM
            ex_e = (jnp.arange(LANES)[:, None] * 2 == hp[None, :]) & (jnp.arange(LANES)[:, None] < npair)
            ex_o = (jnp.arange(LANES)[:, None] * 2 + 1 == hp[None, :]) & (jnp.arange(LANES)[:, None] < npair)
            expand = jnp.stack([ex_e, ex_o]).astype(BF16)
            ob = ssd(u, cb, cw[:, :SSD_INNER], cbias[:, :SSD_INNER], cw[:, SSD_INNER:],
                     cbias[:, SSD_INNER:], perm(ssd_dt_bias[i]), perm(ssd_a_log[i]),
                     jnp.repeat(ssd_d[i], SSD_HEADDIM)[None, :], ssd_norm[i][None, :],
                     expand, tri, batch, seq, tm_seq)
            w_out = ev_w_out[i].astype(BF16)
            w1, w2 = w_out[:GLA_HEADS * GLA_DV], w_out[GLA_HEADS * GLA_DV:]
        else:
            w_in, cb = _odd_layout(od_w_in[i])
            u = inproj(h, norm_mix[layer], sh1, sc1, w_in, seq, tm_mm, w_in.shape[1] // 2)
            per = 4
            oa = lru(u, cb, lru_conv_w[i], lru_conv_b[i][None, :],
                     _block_diag(lru_w_a[i], per).astype(BF16), lru_b_a[i][None, :],
                     _block_diag(lru_w_x[i], per).astype(BF16), lru_b_x[i][None, :],
                     lru_lambda[i][None, :], batch, seq, tm_seq)
            wq = mla_w_q_up[i].reshape(MLA_Q_LORA, MLA_HEADS, MLA_QK)
            wq = jnp.concatenate([wq, _swap_halves(wq[..., MLA_NOPE:])], axis=-1)
            wq = wq.reshape(MLA_Q_LORA, MLA_HEADS * MLA_QK_PAD).astype(BF16)
            wkv = mla_w_kv_up[i].reshape(MLA_KV_LORA, MLA_HEADS, MLA_NOPE + MLA_V)
            wkv = jnp.concatenate([wkv[..., :MLA_NOPE].reshape(MLA_KV_LORA, -1),
                                   wkv[..., MLA_NOPE:].reshape(MLA_KV_LORA, -1)], axis=1).astype(BF16)
            ext_gain = lambda gq: jnp.concatenate([gq, _swap_halves(gq[MLA_NOPE:])])[None, :]
            fr = ROPE_THETA ** (-jnp.arange(0, MLA_ROPE, 2, dtype=F32) / MLA_ROPE)
            freq = jnp.tile(fr, 4)[None, :]
            q, k, v = mla_prep(u, cb, positions.reshape(t, 1), freq, mla_q_norm[i][None, :],
                               mla_kv_norm[i][None, :], wq, wkv, ext_gain(mla_q_qknorm[i]),
                               ext_gain(mla_k_qknorm[i]), tm_mm)
            ob = attention(q, k, v, batch, seq, tq)
            w_out = od_w_out[i].astype(BF16)
            w1, w2 = w_out[:d], w_out[d:]
        h, hf, logits_t = outproj(oa, ob, w1, w2, h, g1, norm_ffn[layer], sh2, sc2, rwt, seq, tm_mm)
        xs, lpos, wts, cnt = route_sort(logits_t, router_bias, hf, tl, nr)
        plan = _moe_plan(cnt[:, :, 0].astype(jnp.int32), nr, tmg, nt_max)
        wgu = jnp.concatenate([moe_w_gate[layer], moe_w_up[layer]], axis=-1).astype(BF16)
        y = moe_group(xs, wgu, moe_w_down[layer].astype(BF16), plan, tmg, nt_max)
        h = moe_combine(y, lpos, wts, h, g2, seq, tl, nr)
    return h.reshape(batch, seq, d)
```

```python
import functools
import math

import jax
import jax.numpy as jnp
from jax import lax
from jax.experimental import pallas as pl
from jax.experimental.pallas import tpu as pltpu

F32 = jnp.float32
BF16 = jnp.bfloat16

EPS = 1e-6
CHUNK = 64
CONV_K = 4
GLA_HEADS, GLA_DK, GLA_DV = 4, 64, 128
GLA_GATE_RANK, GLA_GATE_NORM = 16, 16.0
SSD_HEADS, SSD_HEADDIM, SSD_GROUPS, SSD_STATE = 16, 64, 2, 128
SSD_INNER = SSD_HEADS * SSD_HEADDIM
SSD_GINNER = SSD_INNER // SSD_GROUPS
LRU_BLOCKS, LRU_C = 16, 8.0
MLA_HEADS, MLA_NOPE, MLA_ROPE, MLA_V = 8, 128, 64, 128
MLA_QK = MLA_NOPE + MLA_ROPE
MLA_QK_PAD = 256
MLA_Q_LORA, MLA_KV_LORA = 384, 256
ROPE_THETA = 10000.0
N_EXPERTS, N_GROUPS, D_EXPERT = 16, 4, 512
EXPERTS_PER_GROUP = N_EXPERTS // N_GROUPS

TM_SEQ = 256
TM_MM = 512
TM_MOE = 512
TL_MOE = 512
ROW_CHUNK = 16
TQ_ATTN = 512
ATTN_HEADS_PER_STEP = 2

LANES = 128
SUBLANES = 8
VMEM_LIMIT = 48 * 1024 * 1024

NT_DIMS = (((1,), (1,)), ((), ()))
TN_DIMS = (((0,), (0,)), ((), ()))


def _params(*sem):
    return pltpu.CompilerParams(dimension_semantics=sem, vmem_limit_bytes=VMEM_LIMIT)


def _dot(a, b):
    return jnp.dot(a, b, preferred_element_type=F32)


def _dot_nt(a, b):
    return lax.dot_general(a, b, NT_DIMS, preferred_element_type=F32)


def _dot_tn(a, b):
    return lax.dot_general(a, b, TN_DIMS, preferred_element_type=F32)


def _split3(a):
    hi = a.astype(BF16)
    r1 = a - hi.astype(F32)
    mid = r1.astype(BF16)
    lo = (r1 - mid.astype(F32)).astype(BF16)
    return hi, mid, lo


def _dot_exact_rhs(a, b_bf16):
    hi, mid, lo = _split3(a)
    return _dot(hi, b_bf16) + _dot(mid, b_bf16) + _dot(lo, b_bf16)


def _dot_exact_lhs(a_bf16, b):
    hi, mid, lo = _split3(b)
    return _dot(a_bf16, hi) + _dot(a_bf16, mid) + _dot(a_bf16, lo)


def _softplus(x):
    return jnp.maximum(x, 0.0) + jnp.log1p(jnp.exp(-jnp.abs(x)))


def _silu(x):
    return x * jax.nn.sigmoid(x)


def _lane(shape):
    return lax.broadcasted_iota(jnp.int32, shape, len(shape) - 1)


def _row(shape):
    return lax.broadcasted_iota(jnp.int32, shape, len(shape) - 2)


def _ada_kernel(c_ref, w_ref, b_ref, o_ref):
    c = c_ref[...]
    a_hi, a_mid, a_lo = _split3(_silu(c))
    w_hi, w_mid, w_lo = _split3(w_ref[0])
    acc = (_dot(a_hi, w_hi) + _dot(a_hi, w_mid) + _dot(a_mid, w_hi)
           + _dot(a_hi, w_lo) + _dot(a_lo, w_hi) + _dot(a_mid, w_mid))
    o_ref[0] = acc + b_ref[0]


def ada_mod(c, ada_w, ada_b):
    depth, d, n = ada_w.shape
    b = c.shape[0]
    bp = 8
    cp = jnp.zeros((bp, d), F32).at[:b].set(c)
    tn = 1536
    out = pl.pallas_call(
        _ada_kernel,
        grid=(depth, n // tn),
        in_specs=[pl.BlockSpec((bp, d), lambda l, j: (0, 0)),
                  pl.BlockSpec((1, d, tn), lambda l, j: (l, 0, j)),
                  pl.BlockSpec((1, 1, tn), lambda l, j: (l, 0, j))],
        out_specs=pl.BlockSpec((1, bp, tn), lambda l, j: (l, 0, j)),
        out_shape=jax.ShapeDtypeStruct((depth, bp, n), F32),
        compiler_params=_params("arbitrary", "arbitrary"),
        name="ada_mod",
    )(cp, ada_w, ada_b.reshape(depth, 1, n))
    return out[:, :b]


def _inproj_kernel(x_ref, g_ref, sh_ref, sc_ref, w_ref, o_ref, *, tn):
    x = x_ref[...]
    ms = jnp.mean(x * x, axis=-1, keepdims=True)
    y = x * lax.rsqrt(ms + EPS) * g_ref[...]
    hm = (y * (1.0 + sc_ref[0]) + sh_ref[0]).astype(BF16)
    for j in range(w_ref.shape[1] // tn):
        o_ref[:, j * tn:(j + 1) * tn] = _dot(hm, w_ref[:, j * tn:(j + 1) * tn]).astype(o_ref.dtype)


def inproj(h2d, gain, shift, scale, w, seq, tm, tn):
    t, d = h2d.shape
    n = w.shape[1]
    tpb = seq // tm
    return pl.pallas_call(
        functools.partial(_inproj_kernel, tn=tn),
        grid=(t // tm,),
        in_specs=[pl.BlockSpec((tm, d), lambda i: (i, 0)),
                  pl.BlockSpec((1, d), lambda i: (0, 0)),
                  pl.BlockSpec((1, 1, d), lambda i: (i // tpb, 0, 0)),
                  pl.BlockSpec((1, 1, d), lambda i: (i // tpb, 0, 0)),
                  pl.BlockSpec((d, n), lambda i: (0, 0), pipeline_mode=pl.Buffered(1))],
        out_specs=pl.BlockSpec((tm, n), lambda i: (i, 0)),
        out_shape=jax.ShapeDtypeStruct((t, n), BF16),
        compiler_params=_params("arbitrary"),
        name="inproj",
    )(h2d, gain.reshape(1, d), shift[:, None, :], scale[:, None, :], w)


def _gla_kernel(q_ref, k_ref, v_ref, og_ref, misc_ref, wg_ref, bg_ref, on_ref, tri_ref,
                o_ref, st_ref, cum_ref):
    tm = q_ref.shape[0]

    @pl.when(pl.program_id(1) == 0)
    def _():
        st_ref[...] = jnp.zeros_like(st_ref)

    g = _dot(misc_ref[...], wg_ref[...]) + bg_ref[...]
    la = (jnp.minimum(g, 0.0) - jnp.log1p(jnp.exp(-jnp.abs(g)))) * (1.0 / GLA_GATE_NORM)
    la = jnp.where((_lane(la.shape) & (LANES - 1)) < GLA_DK, la, 0.0)
    cum_ref[...] = _dot_exact_lhs(tri_ref[...], la)

    causal = _row((CHUNK, CHUNK)) >= _lane((CHUNK, CHUNK))
    for c in range(tm // CHUNK):
        rows = pl.ds(c * CHUNK, CHUNK)
        for h in range(GLA_HEADS):
            cols = pl.ds(h * LANES, LANES)
            cu = cum_ref[rows, cols]
            last = cu[CHUNK - 1:CHUNK, :]
            qh = q_ref[rows, cols].astype(F32) * (GLA_DK ** -0.5)
            kh = k_ref[rows, cols].astype(F32)
            q_dec = (qh * jnp.exp(cu)).astype(BF16)
            k_inv = (kh * jnp.exp(-cu)).astype(BF16)
            k_end = (kh * jnp.exp(last - cu)).astype(BF16)
            vh = v_ref[rows, cols]
            att = jnp.where(causal, _dot_nt(q_dec, k_inv), 0.0)
            st = st_ref[h]
            o = _dot(att.astype(BF16), vh) + _dot_nt(q_dec, st.astype(BF16))
            st_ref[h] = st * jnp.exp(last) + _dot_tn(vh, k_end)
            ms = jnp.mean(o * o, axis=-1, keepdims=True)
            on = o * lax.rsqrt(ms + EPS) * on_ref[:, cols]
            o_ref[rows, cols] = (on * _silu(og_ref[rows, cols].astype(F32))).astype(o_ref.dtype)


def gla(u, cb, wg, bg, onorm, tri, batch, seq, tm):
    t = u.shape[0]
    tpb = seq // tm
    w512 = GLA_HEADS * LANES
    row = lambda b, i: b * tpb + i
    return pl.pallas_call(
        _gla_kernel,
        grid=(batch, tpb),
        in_specs=[pl.BlockSpec((tm, w512), lambda b, i: (row(b, i), cb["q"])),
                  pl.BlockSpec((tm, w512), lambda b, i: (row(b, i), cb["k"])),
                  pl.BlockSpec((tm, w512), lambda b, i: (row(b, i), cb["v"])),
                  pl.BlockSpec((tm, w512), lambda b, i: (row(b, i), cb["og"])),
                  pl.BlockSpec((tm, LANES), lambda b, i: (row(b, i), cb["glr"])),
                  pl.BlockSpec((LANES, w512), lambda b, i: (0, 0)),
                  pl.BlockSpec((1, w512), lambda b, i: (0, 0)),
                  pl.BlockSpec((1, w512), lambda b, i: (0, 0)),
                  pl.BlockSpec((tm, tm), lambda b, i: (0, 0))],
        out_specs=pl.BlockSpec((tm, w512), lambda b, i: (row(b, i), 0)),
        out_shape=jax.ShapeDtypeStruct((t, w512), BF16),
        scratch_shapes=[pltpu.VMEM((GLA_HEADS, GLA_DV, LANES), F32),
                        pltpu.VMEM((tm, w512), F32)],
        compiler_params=_params("arbitrary", "arbitrary"),
        name="gla",
    )(u, u, u, u, u, wg, bg, onorm, tri)


def _causal_conv(ext_ref, x, w_ref, b_ref, first):
    tm = x.shape[0]

    @pl.when(first)
    def _():
        ext_ref[0:8, :] = jnp.zeros((8, ext_ref.shape[1]), F32)

    ext_ref[8:8 + tm, :] = x
    y = b_ref[...] + w_ref[CONV_K - 1:CONV_K, :] * x
    for kk in range(CONV_K - 1):
        off = 8 - (CONV_K - 1) + kk
        y = y + w_ref[kk:kk + 1, :] * ext_ref[off:off + tm, :]
    ext_ref[0:8, :] = ext_ref[tm:tm + 8, :]
    return y


def _ssd_kernel(xs_ref, z_ref, bc_ref, dte_ref, dto_ref,
                cwx_ref, cbx_ref, cwb_ref, cbb_ref, dtb_ref, alog_ref, dexp_ref, ng_ref,
                expand_ref, tri_ref, o_ref,
                extx_ref, extb_ref, st_ref, cume_ref, cumo_ref, cumx_ref, xdt_ref, bcc_ref, xsc_ref):
    tm = xs_ref.shape[0]
    first = pl.program_id(1) == 0

    @pl.when(first)
    def _():
        st_ref[...] = jnp.zeros_like(st_ref)

    xs = _silu(_causal_conv(extx_ref, xs_ref[...].astype(F32), cwx_ref, cbx_ref, first))
    bcv = _silu(_causal_conv(extb_ref, bc_ref[...].astype(F32), cwb_ref, cbb_ref, first))
    xsc_ref[...] = xs
    bcc_ref[...] = bcv.astype(BF16)

    npair = SSD_HEADS // 2
    lane = _lane((tm, LANES))
    valid = lane < npair
    a_e = -jnp.exp(alog_ref[0:1, :])
    a_o = -jnp.exp(alog_ref[1:2, :])
    dt_e = jnp.where(valid, _softplus(dte_ref[...].astype(F32) + dtb_ref[0:1, :]), 0.0)
    dt_o = jnp.where(valid, _softplus(dto_ref[...].astype(F32) + dtb_ref[1:2, :]), 0.0)
    tri = tri_ref[...]
    cum_e = _dot_exact_lhs(tri, dt_e * a_e)
    cum_o = _dot_exact_lhs(tri, dt_o * a_o)
    cume_ref[...] = cum_e
    cumo_ref[...] = cum_o
    ex_e = expand_ref[0]
    ex_o = expand_ref[1]
    cumx_ref[...] = _dot_exact_rhs(cum_e, ex_e) + _dot_exact_rhs(cum_o, ex_o)
    xdt_ref[...] = xs * (_dot_exact_rhs(dt_e, ex_e) + _dot_exact_rhs(dt_o, ex_o))

    rr = _row((CHUNK, LANES))
    ll = _lane((CHUNK, LANES))
    causal2 = rr >= (ll & (CHUNK - 1))
    left = ll < CHUNK
    gw = SSD_GINNER
    for c in range(tm // CHUNK):
        rows = pl.ds(c * CHUNK, CHUNK)
        cumx = cumx_ref[rows, :]
        lastx = cumx[CHUNK - 1:CHUNK, :]
        xdt = xdt_ref[rows, :]
        xw = (xdt * jnp.exp(lastx - cumx)).astype(BF16)
        ecum = jnp.exp(cumx)
        pt = jnp.concatenate([cume_ref[rows, :], cumo_ref[rows, :]], axis=0).T
        ys = []
        for g in range(SSD_GROUPS):
            bg = bcc_ref[rows, pl.ds(g * SSD_STATE, SSD_STATE)]
            cg = bcc_ref[rows, pl.ds((SSD_GROUPS + g) * SSD_STATE, SSD_STATE)]
            st = st_ref[g]
            y_off = _dot(cg, st.astype(BF16)) * ecum[:, g * gw:(g + 1) * gw]
            cbcb = _dot_nt(cg, jnp.concatenate([bg, bg], axis=0))
            parts = []
            for j in range(npair // SSD_GROUPS):
                jp = g * (npair // SSD_GROUPS) + j
                colp = cumx[:, jp * LANES:(jp + 1) * LANES]
                seg = colp - pt[jp:jp + 1, :]
                dec = jnp.exp(jnp.where(causal2, seg, -jnp.inf))
                w = (cbcb * dec).astype(BF16)
                xp = xdt[:, jp * LANES:(jp + 1) * LANES]
                x2 = jnp.concatenate([jnp.where(left, xp, 0.0), jnp.where(left, 0.0, xp)],
                                     axis=0).astype(BF16)
                parts.append(_dot(w, x2))
            ys.append(jnp.concatenate(parts, axis=1) + y_off)
            st_ref[g] = (st * jnp.exp(lastx[:, g * gw:(g + 1) * gw])
                         + _dot_tn(bg, xw[:, g * gw:(g + 1) * gw]))
        y = jnp.concatenate(ys, axis=1)
        y = y + dexp_ref[...] * xsc_ref[rows, :]
        y = y * _silu(z_ref[rows, :].astype(F32))
        for g in range(SSD_GROUPS):
            yg = y[:, g * gw:(g + 1) * gw]
            ms = jnp.mean(yg * yg, axis=-1, keepdims=True)
            o_ref[rows, pl.ds(g * gw, gw)] = (
                yg * lax.rsqrt(ms + EPS) * ng_ref[:, g * gw:(g + 1) * gw]).astype(o_ref.dtype)


def ssd(u, cb, cwx, cbx, cwb, cbb, dtb, alog, dexp, ng, expand, tri, batch, seq, tm):
    t = u.shape[0]
    tpb = seq // tm
    row = lambda b, i: b * tpb + i
    bcw = 2 * SSD_GROUPS * SSD_STATE
    const = lambda shape: pl.BlockSpec(shape, lambda b, i: tuple(0 for _ in shape))
    return pl.pallas_call(
        _ssd_kernel,
        grid=(batch, tpb),
        in_specs=[pl.BlockSpec((tm, SSD_INNER), lambda b, i: (row(b, i), cb["xs"])),
                  pl.BlockSpec((tm, SSD_INNER), lambda b, i: (row(b, i), cb["z"])),
                  pl.BlockSpec((tm, bcw), lambda b, i: (row(b, i), cb["bc"])),
                  pl.BlockSpec((tm, LANES), lambda b, i: (row(b, i), cb["dte"])),
                  pl.BlockSpec((tm, LANES), lambda b, i: (row(b, i), cb["dto"])),
                  const((CONV_K, SSD_INNER)), const((1, SSD_INNER)),
                  const((CONV_K, bcw)), const((1, bcw)),
                  const((2, LANES)), const((2, LANES)),
                  const((1, SSD_INNER)), const((1, SSD_INNER)),
                  const((2, LANES, SSD_INNER)), const((tm, tm))],
        out_specs=pl.BlockSpec((tm, SSD_INNER), lambda b, i: (row(b, i), 0)),
        out_shape=jax.ShapeDtypeStruct((t, SSD_INNER), BF16),
        scratch_shapes=[pltpu.VMEM((tm + 8, SSD_INNER), F32),
                        pltpu.VMEM((tm + 8, bcw), F32),
                        pltpu.VMEM((SSD_GROUPS, SSD_STATE, SSD_GINNER), F32),
                        pltpu.VMEM((tm, LANES), F32),
                        pltpu.VMEM((tm, LANES), F32),
                        pltpu.VMEM((tm, SSD_INNER), F32),
                        pltpu.VMEM((tm, SSD_INNER), F32),
                        pltpu.VMEM((tm, bcw), BF16),
                        pltpu.VMEM((tm, SSD_INNER), F32)],
        compiler_params=_params("arbitrary", "arbitrary"),
        name="ssd",
    )(u, u, u, u, u, cwx, cbx, cwb, cbb, dtb, alog, dexp, ng, expand, tri)


def _gelu_tanh(x):
    return 0.5 * x * (1.0 + jnp.tanh(math.sqrt(2.0 / math.pi) * (x + 0.044715 * (x * x * x))))


def _lru_kernel(gate_ref, xr_ref, cw_ref, cb_ref, wa_ref, ba_ref, wx_ref, bx_ref, lam_ref,
                o_ref, ext_ref, carry_ref):
    tm, width = xr_ref.shape
    first = pl.program_id(1) == 0

    @pl.when(first)
    def _():
        carry_ref[...] = jnp.zeros_like(carry_ref)

    x = _causal_conv(ext_ref, xr_ref[...].astype(F32), cw_ref, cb_ref, first)
    xb = x.astype(BF16)
    nblk = wa_ref.shape[0]
    bw = width // nblk
    ra = jnp.concatenate([_dot(xb[:, n * bw:(n + 1) * bw], wa_ref[n]) for n in range(nblk)], axis=1)
    rx = jnp.concatenate([_dot(xb[:, n * bw:(n + 1) * bw], wx_ref[n]) for n in range(nblk)], axis=1)
    r = jax.nn.sigmoid(ra + ba_ref[...])
    ig = jax.nn.sigmoid(rx + bx_ref[...])
    log_a = (-LRU_C) * r * _softplus(-lam_ref[...])
    a = jnp.exp(log_a)
    u = jnp.sqrt(1.0 - jnp.exp(2.0 * log_a)) * (ig * x)

    ngrp = tm // SUBLANES
    a3 = a.reshape(ngrp, SUBLANES, width)
    u3 = u.reshape(ngrp, SUBLANES, width)
    sub = lax.broadcasted_iota(jnp.int32, a3.shape, 1)
    d = 1
    while d < SUBLANES:
        m = sub >= d
        a_s = pltpu.roll(a3, d, 1)
        u_s = pltpu.roll(u3, d, 1)
        u3 = jnp.where(m, a3 * u_s + u3, u3)
        a3 = jnp.where(m, a3 * a_s, a3)
        d *= 2
    carry = carry_ref[0:1, :]
    groups = []
    for j in range(ngrp):
        hj = a3[j] * carry + u3[j]
        groups.append(hj)
        carry = hj[SUBLANES - 1:SUBLANES, :]
    hseq = jnp.concatenate(groups, axis=0)
    carry_ref[...] = jnp.broadcast_to(carry, carry_ref.shape)
    o_ref[...] = (hseq * _gelu_tanh(gate_ref[...].astype(F32))).astype(o_ref.dtype)


def lru(u, cb, cw, cbias, wa, ba, wx, bx, lam, batch, seq, tm):
    t = u.shape[0]
    width = cw.shape[1]
    tpb = seq // tm
    row = lambda b, i: b * tpb + i
    const = lambda shape: pl.BlockSpec(shape, lambda b, i: tuple(0 for _ in shape))
    return pl.pallas_call(
        _lru_kernel,
        grid=(batch, tpb),
        in_specs=[pl.BlockSpec((tm, width), lambda b, i: (row(b, i), cb["gate"])),
                  pl.BlockSpec((tm, width), lambda b, i: (row(b, i), cb["xr"])),
                  const((CONV_K, width)), const((1, width)),
                  const(wa.shape), const((1, width)),
                  const(wx.shape), const((1, width)), const((1, width))],
        out_specs=pl.BlockSpec((tm, width), lambda b, i: (row(b, i), 0)),
        out_shape=jax.ShapeDtypeStruct((t, width), BF16),
        scratch_shapes=[pltpu.VMEM((tm + 8, width), F32),
                        pltpu.VMEM((8, width), F32)],
        compiler_params=_params("arbitrary", "arbitrary"),
        name="lru",
    )(u, u, cw, cbias, wa, ba, wx, bx, lam)


def _mla_prep_kernel(uq_ref, ukv_ref, krr_ref, pos_ref, freq_ref, qn_ref, kvn_ref,
                     wq_ref, wkv_ref, qg_ref, kg_ref, q_ref, k_ref, v_ref):
    tm = uq_ref.shape[0]
    lane = _lane((tm, LANES))
    lo_half = lane < MLA_ROPE

    ang = pos_ref[...].astype(F32) * freq_ref[...]
    cs = jnp.where(lo_half, jnp.cos(ang),
                   jnp.where(lane < MLA_ROPE + MLA_ROPE // 2, -jnp.sin(ang), jnp.sin(ang)))

    def latent_norm(ref, g_ref):
        x = ref[...].astype(F32)
        ms = jnp.mean(x * x, axis=-1, keepdims=True)
        return (x * lax.rsqrt(ms + EPS) * g_ref[...]).astype(BF16)

    qf = _dot(latent_norm(uq_ref, qn_ref), wq_ref[...])
    kvf = _dot(latent_norm(ukv_ref, kvn_ref), wkv_ref[...])
    krr = krr_ref[...].astype(F32)
    kr_ss = jnp.sum(jnp.where(lo_half, krr * krr, 0.0), axis=-1, keepdims=True)
    scale = MLA_QK ** -0.5 * math.log2(math.e)

    def rope_half(y2):
        t = y2 * cs
        return jnp.where(lo_half, t + pltpu.roll(t, MLA_ROPE, 1), 0.0)

    for h in range(MLA_HEADS):
        x1 = qf[:, h * MLA_QK_PAD:h * MLA_QK_PAD + LANES]
        x2 = qf[:, h * MLA_QK_PAD + LANES:(h + 1) * MLA_QK_PAD]
        ss = (jnp.sum(x1 * x1, axis=-1, keepdims=True)
              + jnp.sum(jnp.where(lo_half, x2 * x2, 0.0), axis=-1, keepdims=True))
        r = lax.rsqrt(ss * (1.0 / MLA_QK) + EPS) * scale
        q_ref[:, pl.ds(h * MLA_QK_PAD, LANES)] = (x1 * r * qg_ref[:, 0:LANES]).astype(q_ref.dtype)
        q_ref[:, pl.ds(h * MLA_QK_PAD + LANES, LANES)] = rope_half(
            x2 * r * qg_ref[:, LANES:2 * LANES]).astype(q_ref.dtype)

        kn = kvf[:, h * LANES:(h + 1) * LANES]
        ssk = jnp.sum(kn * kn, axis=-1, keepdims=True) + kr_ss
        rk = lax.rsqrt(ssk * (1.0 / MLA_QK) + EPS)
        k_ref[:, pl.ds(h * MLA_QK_PAD, LANES)] = (kn * rk * kg_ref[:, 0:LANES]).astype(k_ref.dtype)
        k_ref[:, pl.ds(h * MLA_QK_PAD + LANES, LANES)] = rope_half(
            krr * rk * kg_ref[:, LANES:2 * LANES]).astype(k_ref.dtype)
        v_ref[:, pl.ds(h * MLA_V, MLA_V)] = kvf[:, (MLA_HEADS + h) * LANES:
                                                (MLA_HEADS + h + 1) * LANES].astype(v_ref.dtype)


def mla_prep(u, cb, pos2d, freq, qn, kvn, wq, wkv, qg, kg, tm):
    t = u.shape[0]
    const = lambda shape: pl.BlockSpec(shape, lambda i: tuple(0 for _ in shape))
    hq = MLA_HEADS * MLA_QK_PAD
    hv = MLA_HEADS * MLA_V
    return pl.pallas_call(
        _mla_prep_kernel,
        grid=(t // tm,),
        in_specs=[pl.BlockSpec((tm, MLA_Q_LORA), lambda i: (i, cb["uq"])),
                  pl.BlockSpec((tm, MLA_KV_LORA), lambda i: (i, cb["ukv"])),
                  pl.BlockSpec((tm, LANES), lambda i: (i, cb["krr"])),
                  pl.BlockSpec((tm, 1), lambda i: (i, 0)),
                  const((1, LANES)), const((1, MLA_Q_LORA)), const((1, MLA_KV_LORA)),
                  const(wq.shape), const(wkv.shape), const((1, 2 * LANES)), const((1, 2 * LANES))],
        out_specs=[pl.BlockSpec((tm, hq), lambda i: (i, 0)),
                   pl.BlockSpec((tm, hq), lambda i: (i, 0)),
                   pl.BlockSpec((tm, hv), lambda i: (i, 0))],
        out_shape=[jax.ShapeDtypeStruct((t, hq), BF16),
                   jax.ShapeDtypeStruct((t, hq), BF16),
                   jax.ShapeDtypeStruct((t, hv), BF16)],
        compiler_params=_params("arbitrary"),
        name="mla_prep",
    )(u, u, u, pos2d, freq, qn, kvn, wq, wkv, qg, kg)


def _attn_kernel(q_ref, k_ref, v_ref, o_ref, m_ref, acc_ref):
    i = pl.program_id(2)
    tq = q_ref.shape[0]
    nh = q_ref.shape[1] // MLA_QK_PAD
    nc = tq // LANES

    m_ref[...] = jnp.full_like(m_ref, -jnp.inf)
    acc_ref[...] = jnp.zeros_like(acc_ref)
    ones = jnp.ones((tq, LANES), BF16)

    def update(g, s, v):
        m_prev = m_ref[g]
        m_cur = s[:, 0:LANES]
        for c in range(1, nc):
            m_cur = jnp.maximum(m_cur, s[:, c * LANES:(c + 1) * LANES])
        m_new = jnp.maximum(m_prev, jnp.max(m_cur, axis=-1, keepdims=True))
        alpha = jnp.exp2(m_prev - m_new)
        p = jnp.concatenate(
            [jnp.exp2((s[:, c * LANES:(c + 1) * LANES] - m_new).astype(BF16)) for c in range(nc)],
            axis=1)
        pv = _dot(p, jnp.concatenate([v, ones], axis=1))
        acc_ref[g] = jnp.concatenate([alpha, alpha], axis=1) * acc_ref[g] + pv
        m_ref[g] = m_new

    def block(j, masked):
        rows = pl.ds(pl.multiple_of(j * tq, tq), tq)
        for g in range(nh):
            s = _dot_nt(q_ref[:, g * MLA_QK_PAD:(g + 1) * MLA_QK_PAD],
                        k_ref[rows, pl.ds(g * MLA_QK_PAD, MLA_QK_PAD)])
            if masked:
                s = jnp.where(_row((tq, tq)) >= _lane((tq, tq)), s, -jnp.inf)
            update(g, s, v_ref[rows, pl.ds(g * MLA_V, MLA_V)])

    def body(jj, carry):
        block(2 * jj, False)
        block(2 * jj + 1, False)
        return carry

    lax.fori_loop(0, i >> 1, body, 0)

    @pl.when((i & 1) == 1)
    def _():
        block(i - 1, False)

    block(i, True)
    for g in range(nh):
        acc = acc_ref[g]
        o_ref[:, pl.ds(g * MLA_V, MLA_V)] = (acc[:, :MLA_V] / acc[:, MLA_V:]).astype(o_ref.dtype)


def attention(q, k, v, batch, seq, tq):
    t = q.shape[0]
    nq = seq // tq
    nh = ATTN_HEADS_PER_STEP
    return pl.pallas_call(
        _attn_kernel,
        grid=(batch, MLA_HEADS // nh, nq),
        in_specs=[pl.BlockSpec((tq, nh * MLA_QK_PAD), lambda b, h, i: (b * nq + i, h)),
                  pl.BlockSpec((seq, nh * MLA_QK_PAD), lambda b, h, i: (b, h)),
                  pl.BlockSpec((seq, nh * MLA_V), lambda b, h, i: (b, h))],
        out_specs=pl.BlockSpec((tq, nh * MLA_V), lambda b, h, i: (b * nq + i, h)),
        out_shape=jax.ShapeDtypeStruct((t, MLA_HEADS * MLA_V), BF16),
        scratch_shapes=[pltpu.VMEM((nh, tq, LANES), F32),
                        pltpu.VMEM((nh, tq, MLA_V + LANES), F32)],
        compiler_params=_params("arbitrary", "arbitrary", "arbitrary"),
        name="attention",
    )(q, k, v)


def _outproj_kernel(oa_ref, ob_ref, w1_ref, w2_ref, h_ref, g1_ref, nf_ref, sh_ref, sc_ref,
                    rwt_ref, hn_ref, hf_ref, lg_ref):
    y = _dot(oa_ref[...], w1_ref[...]) + _dot(ob_ref[...], w2_ref[...])
    hn = h_ref[...] + g1_ref[0] * y
    hn_ref[...] = hn
    ms = jnp.mean(hn * hn, axis=-1, keepdims=True)
    hf = hn * lax.rsqrt(ms + EPS) * nf_ref[...] * (1.0 + sc_ref[0]) + sh_ref[0]
    hf_ref[...] = hf.astype(BF16)
    r_hi, r_mid, r_lo = _split3(rwt_ref[...])
    f_hi, f_mid, f_lo = _split3(hf)
    lg_ref[...] = (_dot_nt(r_hi, f_hi) + _dot_nt(r_hi, f_mid) + _dot_nt(r_mid, f_hi)
                   + _dot_nt(r_hi, f_lo) + _dot_nt(r_lo, f_hi) + _dot_nt(r_mid, f_mid))


def outproj(oa, ob, w1, w2, h2d, g1, nf, sh2, sc2, rwt, seq, tm):
    t, d = h2d.shape
    tpb = seq // tm
    const = lambda shape: pl.BlockSpec(shape, lambda i: tuple(0 for _ in shape))
    bvec = pl.BlockSpec((1, 1, d), lambda i: (i // tpb, 0, 0))
    return pl.pallas_call(
        _outproj_kernel,
        grid=(t // tm,),
        in_specs=[pl.BlockSpec((tm, oa.shape[1]), lambda i: (i, 0)),
                  pl.BlockSpec((tm, ob.shape[1]), lambda i: (i, 0)),
                  const(w1.shape), const(w2.shape),
                  pl.BlockSpec((tm, d), lambda i: (i, 0)),
                  bvec, const((1, d)), bvec, bvec, const(rwt.shape)],
        out_specs=[pl.BlockSpec((tm, d), lambda i: (i, 0)),
                   pl.BlockSpec((tm, d), lambda i: (i, 0)),
                   pl.BlockSpec((N_EXPERTS, tm), lambda i: (0, i))],
        out_shape=[jax.ShapeDtypeStruct((t, d), F32),
                   jax.ShapeDtypeStruct((t, d), BF16),
                   jax.ShapeDtypeStruct((N_EXPERTS, t), F32)],
        compiler_params=_params("arbitrary"),
        name="outproj",
    )(oa, ob, w1, w2, h2d, g1[:, None, :], nf.reshape(1, d), sh2[:, None, :], sc2[:, None, :], rwt)


def _route_sort_kernel(lg_ref, bias_ref, hf_ref, utri_ref, ltri_ref,
                       xs_ref, lpos_ref, wts_ref, cnt_ref):
    scores = jax.nn.sigmoid(lg_ref[...])
    sel = scores + bias_ref[...]
    tm = sel.shape[1]
    eidx = _row((N_EXPERTS, tm))
    neg = -jnp.inf

    best_score = None
    best_grp = None
    for g in range(N_GROUPS):
        m = [sel[g * EXPERTS_PER_GROUP + i:g * EXPERTS_PER_GROUP + i + 1, :]
             for i in range(EXPERTS_PER_GROUP)]
        gs = None
        for a in range(EXPERTS_PER_GROUP):
            for b in range(a + 1, EXPERTS_PER_GROUP):
                pair = m[a] + m[b]
                gs = pair if gs is None else jnp.maximum(gs, pair)
        if best_score is None:
            best_score, best_grp = gs, jnp.zeros_like(gs, dtype=jnp.int32)
        else:
            better = gs > best_score
            best_score = jnp.where(better, gs, best_score)
            best_grp = jnp.where(better, g, best_grp)

    masked = jnp.where((eidx >> 2) == best_grp, sel, neg)
    m1 = jnp.max(masked, axis=0, keepdims=True)
    i1 = jnp.min(jnp.where(masked == m1, eidx, N_EXPERTS), axis=0, keepdims=True)
    masked2 = jnp.where(eidx == i1, neg, masked)
    m2 = jnp.max(masked2, axis=0, keepdims=True)
    i2 = jnp.min(jnp.where(masked2 == m2, eidx, N_EXPERTS), axis=0, keepdims=True)
    pick1 = eidx == i1
    pick2 = eidx == i2
    s1 = jnp.sum(jnp.where(pick1, scores, 0.0), axis=0, keepdims=True)
    s2 = jnp.sum(jnp.where(pick2, scores, 0.0), axis=0, keepdims=True)
    tot = s1 + s2
    wts_ref[...] = jnp.concatenate([s1 / tot, s2 / tot], axis=0)

    nr = xs_ref.shape[0]
    picks = jnp.where(pick1, 1.0, jnp.where(pick2, 1.0, 0.0))
    csum = _dot(picks.astype(BF16), utri_ref[...])
    cnt = jnp.sum(picks, axis=1, keepdims=True)
    cnt_ref[0] = jnp.broadcast_to(cnt, (N_EXPERTS, LANES))
    cnt_pad = jnp.floor((cnt + (ROW_CHUNK - 1.0)) * (1.0 / ROW_CHUNK)) * ROW_CHUNK
    seg_off = _dot(ltri_ref[...], jnp.broadcast_to(cnt_pad, (N_EXPERTS, LANES)).astype(BF16))[:, 0:1]
    lposmat = seg_off + csum - 1.0
    lp1 = jnp.sum(jnp.where(pick1, lposmat, 0.0), axis=0, keepdims=True).astype(jnp.int32)
    lp2 = jnp.sum(jnp.where(pick2, lposmat, 0.0), axis=0, keepdims=True).astype(jnp.int32)
    lpos_ref[...] = jnp.concatenate([lp1, lp2], axis=0)
    rowi = _row((nr, tm))
    onehot = jnp.where(rowi == lp1, 1.0, jnp.where(rowi == lp2, 1.0, 0.0)).astype(BF16)
    xs_ref[...] = _dot(onehot, hf_ref[...]).astype(xs_ref.dtype)


def route_sort(logits_t, router_bias, hf, tl, nr):
    e, t = logits_t.shape
    d = hf.shape[1]
    r = jnp.arange(tl)
    utri = (r[:, None] <= r[None, :]).astype(BF16)
    re = jnp.arange(e)
    ltri = (re[None, :] < re[:, None]).astype(BF16)
    const = lambda shape: pl.BlockSpec(shape, lambda i: tuple(0 for _ in shape))
    return pl.pallas_call(
        _route_sort_kernel,
        grid=(t // tl,),
        in_specs=[pl.BlockSpec((e, tl), lambda i: (0, i)),
                  const((e, 1)),
                  pl.BlockSpec((tl, d), lambda i: (i, 0)),
                  const((tl, tl)), const((e, e))],
        out_specs=[pl.BlockSpec((nr, d), lambda i: (i, 0)),
                   pl.BlockSpec((2, tl), lambda i: (0, i)),
                   pl.BlockSpec((2, tl), lambda i: (0, i)),
                   pl.BlockSpec((1, e, LANES), lambda i: (i, 0, 0))],
        out_shape=[jax.ShapeDtypeStruct((t // tl * nr, d), BF16),
                   jax.ShapeDtypeStruct((2, t), jnp.int32),
                   jax.ShapeDtypeStruct((2, t), F32),
                   jax.ShapeDtypeStruct((t // tl, e, LANES), F32)],
        compiler_params=_params("arbitrary"),
        name="route_sort",
    )(logits_t, router_bias.reshape(e, 1).astype(F32), hf, utri, ltri)


def _moe_plan(cnt, nr, tmg, nt_max):
    ntile, ne = cnt.shape
    cpt = tmg // ROW_CHUNK
    nch = (cnt + ROW_CHUNK - 1) // ROW_CHUNK
    seg_off = jnp.cumsum(nch, axis=1) - nch
    cum = jnp.cumsum(nch, axis=0)
    tot = cum[-1]
    padded = (tot + cpt - 1) // cpt * cpt
    gend = jnp.cumsum(padded)
    gstart = gend - padded
    start = (gstart[None, :] + cum - nch).T.reshape(-1)
    end = start + nch.T.reshape(-1)
    base = (jnp.arange(ntile, dtype=jnp.int32)[:, None] * (nr // ROW_CHUNK) + seg_off).T.reshape(-1)
    c = jnp.arange(nt_max * cpt, dtype=jnp.int32)[:, None]
    hit = (c >= start[None, :]) & (c < end[None, :])
    valid = jnp.any(hit, axis=1)
    src = jnp.sum(jnp.where(hit, base[None, :] - start[None, :] + c, 0), axis=1) * ROW_CHUNK
    first = jnp.arange(nt_max, dtype=jnp.int32) * cpt
    e_first = jnp.sum(first[:, None] >= gend[None, :], axis=1).astype(jnp.int32)
    tile_on = (e_first < ne).astype(jnp.int32)
    return src.astype(jnp.int32), valid.astype(jnp.int32), jnp.minimum(e_first, ne - 1), tile_on


def _moe_group_kernel(src_ref, val_ref, te_ref, on_ref, xs_hbm, wg_ref, wu_ref, wd_ref, yinit_hbm,
                      y_hbm, xbuf, ybuf, wgb, wub, wdb, in_sem, out_sem):
    del yinit_hbm
    g = pl.program_id(0)
    ng = pl.num_programs(0)
    slot = lax.rem(g, 2)
    tmg = xbuf.shape[1]
    cpt = tmg // ROW_CHUNK

    def in_copy(tile, sl, k):
        row = pl.multiple_of(src_ref[tile * cpt + k], ROW_CHUNK)
        return pltpu.make_async_copy(xs_hbm.at[pl.ds(row, ROW_CHUNK), :],
                                     xbuf.at[sl, pl.ds(k * ROW_CHUNK, ROW_CHUNK), :], in_sem.at[sl])

    def out_copy(tile, sl, k):
        row = pl.multiple_of(src_ref[tile * cpt + k], ROW_CHUNK)
        return pltpu.make_async_copy(ybuf.at[sl, pl.ds(k * ROW_CHUNK, ROW_CHUNK), :],
                                     y_hbm.at[pl.ds(row, ROW_CHUNK), :], out_sem.at[sl])

    def start_in(tile, sl):
        for k in range(cpt):
            in_copy(tile, sl, k).start()

    def wait_in(tile, sl):
        for k in range(cpt):
            in_copy(tile, sl, k).wait()

    def for_valid_out(tile, sl, fn):
        for k in range(cpt):
            @pl.when(val_ref[tile * cpt + k] == 1)
            def _():
                fn(out_copy(tile, sl, k))

    @pl.when((g >= 2) & (on_ref[jnp.maximum(g - 2, 0)] == 1))
    def _():
        for_valid_out(g - 2, slot, lambda cp: cp.wait())

    @pl.when((g == 0) & (on_ref[0] == 1))
    def _():
        start_in(0, 0)

    @pl.when((g + 1 < ng) & (on_ref[jnp.minimum(g + 1, ng - 1)] == 1))
    def _():
        start_in(g + 1, 1 - slot)

    @pl.when(on_ref[g] == 1)
    def _():
        @pl.when((g == 0) | (te_ref[g] != te_ref[jnp.maximum(g - 1, 0)]))
        def _():
            wgb[...] = wg_ref[0, 0].astype(BF16)
            wub[...] = wu_ref[0, 0].astype(BF16)
            wdb[...] = wd_ref[0, 0].astype(BF16)

        wait_in(g, slot)
        x = xbuf[slot]
        hid = (_silu(_dot(x, wgb[...])) * _dot(x, wub[...])).astype(BF16)
        ybuf[slot] = _dot(hid, wdb[...]).astype(ybuf.dtype)
        for_valid_out(g, slot, lambda cp: cp.start())

    @pl.when(g == ng - 1)
    def _():
        @pl.when((ng >= 2) & (on_ref[jnp.maximum(g - 1, 0)] == 1))
        def _():
            for_valid_out(g - 1, 1 - slot, lambda cp: cp.wait())

        @pl.when(on_ref[g] == 1)
        def _():
            for_valid_out(g, slot, lambda cp: cp.wait())


def moe_group(xs, wg, wu, wd, layer, plan, tmg, nt_max):
    src, valid, tile_e, tile_on = plan
    rows, d = xs.shape
    wspec = lambda w: pl.BlockSpec((1, 1) + w.shape[2:], lambda g, s, v, te, on: (layer, te[g], 0, 0))
    grid_spec = pltpu.PrefetchScalarGridSpec(
        num_scalar_prefetch=4,
        grid=(nt_max,),
        in_specs=[pl.BlockSpec(memory_space=pl.ANY), wspec(wg), wspec(wu), wspec(wd),
                  pl.BlockSpec(memory_space=pl.ANY)],
        out_specs=pl.BlockSpec(memory_space=pl.ANY),
        scratch_shapes=[pltpu.VMEM((2, tmg, d), BF16), pltpu.VMEM((2, tmg, d), BF16),
                        pltpu.VMEM(wg.shape[2:], BF16), pltpu.VMEM(wu.shape[2:], BF16),
                        pltpu.VMEM(wd.shape[2:], BF16),
                        pltpu.SemaphoreType.DMA((2,)), pltpu.SemaphoreType.DMA((2,))])
    return pl.pallas_call(
        _moe_group_kernel,
        grid_spec=grid_spec,
        out_shape=jax.ShapeDtypeStruct((rows, d), BF16),
        input_output_aliases={8: 0},
        compiler_params=_params("arbitrary"),
        name="moe_group",
    )(src, valid, tile_e, tile_on, xs, wg, wu, wd, jnp.zeros((rows, d), BF16))


def _moe_combine_kernel(y_ref, lpos_ref, wts_ref, h_ref, g2_ref, o_ref):
    nr = y_ref.shape[0]
    tl = h_ref.shape[0]
    rowi = _row((nr, tl))
    wc = (jnp.where(rowi == lpos_ref[0:1, :], wts_ref[0:1, :], 0.0)
          + jnp.where(rowi == lpos_ref[1:2, :], wts_ref[1:2, :], 0.0)).astype(BF16)
    o_ref[...] = h_ref[...] + g2_ref[0] * _dot_tn(wc, y_ref[...])


def moe_combine(y, lpos, wts, h2d, g2, seq, tl, nr):
    t, d = h2d.shape
    tpb = seq // tl
    return pl.pallas_call(
        _moe_combine_kernel,
        grid=(t // tl,),
        in_specs=[pl.BlockSpec((nr, d), lambda i: (i, 0)),
                  pl.BlockSpec((2, tl), lambda i: (0, i)),
                  pl.BlockSpec((2, tl), lambda i: (0, i)),
                  pl.BlockSpec((tl, d), lambda i: (i, 0)),
                  pl.BlockSpec((1, 1, d), lambda i: (i // tpb, 0, 0))],
        out_specs=pl.BlockSpec((tl, d), lambda i: (i, 0)),
        out_shape=jax.ShapeDtypeStruct((t, d), F32),
        compiler_params=_params("arbitrary"),
        name="moe_combine",
    )(y, lpos, wts, h2d, g2[:, None, :])


def _pad_heads(w, heads, dh):
    lead = w.shape[:-1]
    w = w.reshape(lead + (heads, dh))
    w = jnp.pad(w, [(0, 0)] * len(lead) + [(0, 0), (0, LANES - dh)])
    return w.reshape(lead + (heads * LANES,))


def _pad_cols(w, n):
    return jnp.pad(w, [(0, 0)] * (w.ndim - 1) + [(0, n - w.shape[-1])])


def _tri_blocks(tm):
    r = jnp.arange(tm)
    return ((r[:, None] >= r[None, :]) & (r[:, None] // CHUNK == r[None, :] // CHUNK)).astype(BF16)


def _even_layout(w_in):
    sizes = (GLA_HEADS * GLA_DK, GLA_HEADS * GLA_DK, GLA_HEADS * GLA_DV, GLA_GATE_RANK,
             GLA_HEADS * GLA_DV, SSD_INNER, SSD_INNER + 2 * SSD_GROUPS * SSD_STATE, SSD_HEADS)
    offs = [0]
    for s in sizes:
        offs.append(offs[-1] + s)
    seg = lambda i: w_in[:, offs[i]:offs[i + 1]]
    q, k, v, glr, og, z, xbc, dt = (seg(i) for i in range(8))
    xs, bc = xbc[:, :SSD_INNER], xbc[:, SSD_INNER:]
    cols = [xs, z, v, og, _pad_heads(q, GLA_HEADS, GLA_DK), _pad_heads(k, GLA_HEADS, GLA_DK), bc,
            _pad_cols(glr, LANES), _pad_cols(dt[:, 0::2], LANES), _pad_cols(dt[:, 1::2], LANES)]
    w = jnp.concatenate(cols, axis=1).astype(BF16)
    cb = {"xs": 0, "z": 1, "v": 4, "og": 5, "q": 6, "k": 7, "bc": 8, "glr": 36, "dte": 37, "dto": 38}
    return w, cb


def _odd_layout(w_in):
    d = w_in.shape[0]
    o = [0, d, 2 * d, 2 * d + MLA_Q_LORA, 2 * d + MLA_Q_LORA + MLA_KV_LORA,
         2 * d + MLA_Q_LORA + MLA_KV_LORA + MLA_ROPE]
    gate, xr, uq, ukv, kr = (w_in[:, o[i]:o[i + 1]] for i in range(5))
    half = MLA_ROPE // 2
    kr_sw = jnp.concatenate([kr[:, half:], kr[:, :half]], axis=1)
    w = jnp.concatenate([gate, xr, ukv, uq, kr, kr_sw], axis=1).astype(BF16)
    cb = {"gate": 0, "xr": 1, "ukv": 8, "uq": 6, "krr": 21}
    return w, cb


def _swap_halves(x):
    half = x.shape[-1] // 2
    return jnp.concatenate([x[..., half:], x[..., :half]], axis=-1)


def _block_diag(w, per):
    nb, bw, _ = w.shape
    w = w.reshape(nb // per, per, bw, bw)
    eye = jnp.eye(per, dtype=w.dtype)
    out = jnp.einsum("gpij,pq->gpiqj", w, eye)
    return out.reshape(nb // per, per * bw, per * bw)


def kernel(x, c, positions, router_w, router_bias, ada_w, ada_b, norm_mix, norm_ffn, moe_w_gate, moe_w_up, moe_w_down, ev_w_in, gla_w_g2, gla_b_g2, gla_onorm, ssd_conv_w, ssd_conv_b, ssd_dt_bias, ssd_a_log, ssd_d, ssd_norm, ev_w_out, od_w_in, lru_conv_w, lru_conv_b, lru_w_a, lru_b_a, lru_w_x, lru_b_x, lru_lambda, mla_q_norm, mla_w_q_up, mla_kv_norm, mla_w_kv_up, mla_q_qknorm, mla_k_qknorm, od_w_out):
    batch, seq, d = x.shape
    t = batch * seq
    depth = ada_w.shape[0]
    tm_seq = min(TM_SEQ, seq)
    tm_mm = min(TM_MM, seq)
    tq = min(TQ_ATTN, seq)
    tl = min(TL_MOE, seq)
    nr = 2 * tl + N_EXPERTS * ROW_CHUNK
    tmg = TM_MOE
    chunks_max = 2 * t // ROW_CHUNK + (t // tl) * N_EXPERTS + N_EXPERTS * (tmg // ROW_CHUNK - 1)
    nt_max = -(-chunks_max // (tmg // ROW_CHUNK))

    mod = ada_mod(c, ada_w, ada_b)
    rwt = router_w.T.astype(F32)
    tri = _tri_blocks(tm_seq)
    h = x.reshape(t, d)

    for layer in range(depth):
        sh1, sc1, g1, sh2, sc2, g2 = (mod[layer, :, i * d:(i + 1) * d] for i in range(6))
        i = layer // 2
        if layer % 2 == 0:
            w_in, cb = _even_layout(ev_w_in[i])
            u = inproj(h, norm_mix[layer], sh1, sc1, w_in, seq, tm_mm, w_in.shape[1] // 3)
            wg = jnp.pad(_pad_heads(gla_w_g2[i], GLA_HEADS, GLA_DK),
                         ((0, LANES - GLA_GATE_RANK), (0, 0))).astype(BF16)
            bg = _pad_heads(gla_b_g2[i][None, :], GLA_HEADS, GLA_DK)
            oa = gla(u, cb, wg, bg, gla_onorm[i][None, :], tri, batch, seq, tm_seq)
            cw, cbias = ssd_conv_w[i], ssd_conv_b[i][None, :]
            perm = lambda v: jnp.stack([_pad_cols(v[0::2], LANES), _pad_cols(v[1::2], LANES)])
            npair = SSD_HEADS // 2
            hp = jnp.arange(SSD_INNER) // SSD_HEADDIM
            ex_e = (jnp.arange(LANES)[:, None] * 2 == hp[None, :]) & (jnp.arange(LANES)[:, None] < npair)
            ex_o = (jnp.arange(LANES)[:, None] * 2 + 1 == hp[None, :]) & (jnp.arange(LANES)[:, None] < npair)
            expand = jnp.stack([ex_e, ex_o]).astype(BF16)
            ob = ssd(u, cb, cw[:, :SSD_INNER], cbias[:, :SSD_INNER], cw[:, SSD_INNER:],
                     cbias[:, SSD_INNER:], perm(ssd_dt_bias[i]), perm(ssd_a_log[i]),
                     jnp.repeat(ssd_d[i], SSD_HEADDIM)[None, :], ssd_norm[i][None, :],
                     expand, tri, batch, seq, tm_seq)
            w_out = ev_w_out[i].astype(BF16)
            w1, w2 = w_out[:GLA_HEADS * GLA_DV], w_out[GLA_HEADS * GLA_DV:]
        else:
            w_in, cb = _odd_layout(od_w_in[i])
            u = inproj(h, norm_mix[layer], sh1, sc1, w_in, seq, tm_mm, w_in.shape[1] // 2)
            per = 4
            oa = lru(u, cb, lru_conv_w[i], lru_conv_b[i][None, :],
                     _block_diag(lru_w_a[i], per).astype(BF16), lru_b_a[i][None, :],
                     _block_diag(lru_w_x[i], per).astype(BF16), lru_b_x[i][None, :],
                     lru_lambda[i][None, :], batch, seq, tm_seq)
            wq = mla_w_q_up[i].reshape(MLA_Q_LORA, MLA_HEADS, MLA_QK)
            wq = jnp.concatenate([wq, _swap_halves(wq[..., MLA_NOPE:])], axis=-1)
            wq = wq.reshape(MLA_Q_LORA, MLA_HEADS * MLA_QK_PAD).astype(BF16)
            wkv = mla_w_kv_up[i].reshape(MLA_KV_LORA, MLA_HEADS, MLA_NOPE + MLA_V)
            wkv = jnp.concatenate([wkv[..., :MLA_NOPE].reshape(MLA_KV_LORA, -1),
                                   wkv[..., MLA_NOPE:].reshape(MLA_KV_LORA, -1)], axis=1).astype(BF16)
            ext_gain = lambda gq: jnp.concatenate([gq, _swap_halves(gq[MLA_NOPE:])])[None, :]
            fr = ROPE_THETA ** (-jnp.arange(0, MLA_ROPE, 2, dtype=F32) / MLA_ROPE)
            freq = jnp.tile(fr, 4)[None, :]
            q, k, v = mla_prep(u, cb, positions.reshape(t, 1), freq, mla_q_norm[i][None, :],
                               mla_kv_norm[i][None, :], wq, wkv, ext_gain(mla_q_qknorm[i]),
                               ext_gain(mla_k_qknorm[i]), tm_mm)
            ob = attention(q, k, v, batch, seq, tq)
            w_out = od_w_out[i].astype(BF16)
            w1, w2 = w_out[:d], w_out[d:]
        h, hf, logits_t = outproj(oa, ob, w1, w2, h, g1, norm_ffn[layer], sh2, sc2, rwt, seq, tm_mm)
        xs, lpos, wts, cnt = route_sort(logits_t, router_bias, hf, tl, nr)
        plan = _moe_plan(cnt[:, :, 0].astype(jnp.int32), nr, tmg, nt_max)
        y = moe_group(xs, moe_w_gate, moe_w_up, moe_w_down, layer, plan, tmg, nt_max)
        h = moe_combine(y, lpos, wts, h, g2, seq, tl, nr)
    return h.reshape(batch, seq, d)
```

```python
import functools
import math

import jax
import jax.numpy as jnp
from jax import lax
from jax.experimental import pallas as pl
from jax.experimental.pallas import tpu as pltpu

F32 = jnp.float32
BF16 = jnp.bfloat16

EPS = 1e-6
CHUNK = 64
CONV_K = 4
GLA_HEADS, GLA_DK, GLA_DV = 4, 64, 128
GLA_GATE_RANK, GLA_GATE_NORM = 16, 16.0
SSD_HEADS, SSD_HEADDIM, SSD_GROUPS, SSD_STATE = 16, 64, 2, 128
SSD_INNER = SSD_HEADS * SSD_HEADDIM
SSD_GINNER = SSD_INNER // SSD_GROUPS
LRU_BLOCKS, LRU_C = 16, 8.0
MLA_HEADS, MLA_NOPE, MLA_ROPE, MLA_V = 8, 128, 64, 128
MLA_QK = MLA_NOPE + MLA_ROPE
MLA_QK_PAD = 256
MLA_Q_LORA, MLA_KV_LORA = 384, 256
ROPE_THETA = 10000.0
N_EXPERTS, N_GROUPS, D_EXPERT = 16, 4, 512
EXPERTS_PER_GROUP = N_EXPERTS // N_GROUPS

TM_SEQ = 256
TM_MM = 512
TM_MOE = 512
TL_MOE = 512
ROW_CHUNK = 16
TQ_ATTN = 512
ATTN_HEADS_PER_STEP = 4

LANES = 128
SUBLANES = 8
VMEM_LIMIT = 48 * 1024 * 1024

NT_DIMS = (((1,), (1,)), ((), ()))
TN_DIMS = (((0,), (0,)), ((), ()))


def _params(*sem):
    return pltpu.CompilerParams(dimension_semantics=sem, vmem_limit_bytes=VMEM_LIMIT)


def _dot(a, b):
    return jnp.dot(a, b, preferred_element_type=F32)


def _dot_nt(a, b):
    return lax.dot_general(a, b, NT_DIMS, preferred_element_type=F32)


def _dot_tn(a, b):
    return lax.dot_general(a, b, TN_DIMS, preferred_element_type=F32)


def _split3(a):
    hi = a.astype(BF16)
    r1 = a - hi.astype(F32)
    mid = r1.astype(BF16)
    lo = (r1 - mid.astype(F32)).astype(BF16)
    return hi, mid, lo


def _dot_exact_rhs(a, b_bf16):
    hi, mid, lo = _split3(a)
    return _dot(hi, b_bf16) + _dot(mid, b_bf16) + _dot(lo, b_bf16)


def _dot_exact_lhs(a_bf16, b):
    hi, mid, lo = _split3(b)
    return _dot(a_bf16, hi) + _dot(a_bf16, mid) + _dot(a_bf16, lo)


def _softplus(x):
    return jnp.maximum(x, 0.0) + jnp.log1p(jnp.exp(-jnp.abs(x)))


def _silu(x):
    return x * jax.nn.sigmoid(x)


def _lane(shape):
    return lax.broadcasted_iota(jnp.int32, shape, len(shape) - 1)


def _row(shape):
    return lax.broadcasted_iota(jnp.int32, shape, len(shape) - 2)


def _ada_kernel(c_ref, w_ref, b_ref, o_ref):
    c = c_ref[...]
    a_hi, a_mid, a_lo = _split3(_silu(c))
    w_hi, w_mid, w_lo = _split3(w_ref[0])
    acc = (_dot(a_hi, w_hi) + _dot(a_hi, w_mid) + _dot(a_mid, w_hi)
           + _dot(a_hi, w_lo) + _dot(a_lo, w_hi) + _dot(a_mid, w_mid))
    o_ref[0] = acc + b_ref[0]


def ada_mod(c, ada_w, ada_b):
    depth, d, n = ada_w.shape
    b = c.shape[0]
    bp = 8
    cp = jnp.zeros((bp, d), F32).at[:b].set(c)
    tn = 1536
    out = pl.pallas_call(
        _ada_kernel,
        grid=(depth, n // tn),
        in_specs=[pl.BlockSpec((bp, d), lambda l, j: (0, 0)),
                  pl.BlockSpec((1, d, tn), lambda l, j: (l, 0, j)),
                  pl.BlockSpec((1, 1, tn), lambda l, j: (l, 0, j))],
        out_specs=pl.BlockSpec((1, bp, tn), lambda l, j: (l, 0, j)),
        out_shape=jax.ShapeDtypeStruct((depth, bp, n), F32),
        compiler_params=_params("arbitrary", "arbitrary"),
        name="ada_mod",
    )(cp, ada_w, ada_b.reshape(depth, 1, n))
    return out[:, :b]


def _inproj_kernel(x_ref, g_ref, sh_ref, sc_ref, w_ref, o_ref, *, tn):
    x = x_ref[...]
    ms = jnp.mean(x * x, axis=-1, keepdims=True)
    y = x * lax.rsqrt(ms + EPS) * g_ref[...]
    hm = (y * (1.0 + sc_ref[0]) + sh_ref[0]).astype(BF16)
    for j in range(w_ref.shape[1] // tn):
        o_ref[:, j * tn:(j + 1) * tn] = _dot(hm, w_ref[:, j * tn:(j + 1) * tn]).astype(o_ref.dtype)


def inproj(h2d, gain, shift, scale, w, seq, tm, tn):
    t, d = h2d.shape
    n = w.shape[1]
    tpb = seq // tm
    return pl.pallas_call(
        functools.partial(_inproj_kernel, tn=tn),
        grid=(t // tm,),
        in_specs=[pl.BlockSpec((tm, d), lambda i: (i, 0)),
                  pl.BlockSpec((1, d), lambda i: (0, 0)),
                  pl.BlockSpec((1, 1, d), lambda i: (i // tpb, 0, 0)),
                  pl.BlockSpec((1, 1, d), lambda i: (i // tpb, 0, 0)),
                  pl.BlockSpec((d, n), lambda i: (0, 0), pipeline_mode=pl.Buffered(1))],
        out_specs=pl.BlockSpec((tm, n), lambda i: (i, 0)),
        out_shape=jax.ShapeDtypeStruct((t, n), BF16),
        compiler_params=_params("arbitrary"),
        name="inproj",
    )(h2d, gain.reshape(1, d), shift[:, None, :], scale[:, None, :], w)


def _gla_kernel(q_ref, k_ref, v_ref, og_ref, misc_ref, wg_ref, bg_ref, on_ref, tri_ref,
                o_ref, st_ref, cum_ref):
    tm = q_ref.shape[0]

    @pl.when(pl.program_id(1) == 0)
    def _():
        st_ref[...] = jnp.zeros_like(st_ref)

    g = _dot(misc_ref[...], wg_ref[...]) + bg_ref[...]
    la = (jnp.minimum(g, 0.0) - jnp.log1p(jnp.exp(-jnp.abs(g)))) * (1.0 / GLA_GATE_NORM)
    la = jnp.where((_lane(la.shape) & (LANES - 1)) < GLA_DK, la, 0.0)
    cum_ref[...] = _dot_exact_lhs(tri_ref[...], la)

    causal = _row((CHUNK, CHUNK)) >= _lane((CHUNK, CHUNK))
    for c in range(tm // CHUNK):
        rows = pl.ds(c * CHUNK, CHUNK)
        for h in range(GLA_HEADS):
            cols = pl.ds(h * LANES, LANES)
            cu = cum_ref[rows, cols]
            last = cu[CHUNK - 1:CHUNK, :]
            qh = q_ref[rows, cols].astype(F32) * (GLA_DK ** -0.5)
            kh = k_ref[rows, cols].astype(F32)
            q_dec = (qh * jnp.exp(cu)).astype(BF16)
            k_inv = (kh * jnp.exp(-cu)).astype(BF16)
            k_end = (kh * jnp.exp(last - cu)).astype(BF16)
            vh = v_ref[rows, cols]
            att = jnp.where(causal, _dot_nt(q_dec, k_inv), 0.0)
            st = st_ref[h]
            o = _dot(att.astype(BF16), vh) + _dot_nt(q_dec, st.astype(BF16))
            st_ref[h] = st * jnp.exp(last) + _dot_tn(vh, k_end)
            ms = jnp.mean(o * o, axis=-1, keepdims=True)
            on = o * lax.rsqrt(ms + EPS) * on_ref[:, cols]
            o_ref[rows, cols] = (on * _silu(og_ref[rows, cols].astype(F32))).astype(o_ref.dtype)


def gla(u, cb, wg, bg, onorm, tri, batch, seq, tm):
    t = u.shape[0]
    tpb = seq // tm
    w512 = GLA_HEADS * LANES
    row = lambda b, i: b * tpb + i
    return pl.pallas_call(
        _gla_kernel,
        grid=(batch, tpb),
        in_specs=[pl.BlockSpec((tm, w512), lambda b, i: (row(b, i), cb["q"])),
                  pl.BlockSpec((tm, w512), lambda b, i: (row(b, i), cb["k"])),
                  pl.BlockSpec((tm, w512), lambda b, i: (row(b, i), cb["v"])),
                  pl.BlockSpec((tm, w512), lambda b, i: (row(b, i), cb["og"])),
                  pl.BlockSpec((tm, LANES), lambda b, i: (row(b, i), cb["glr"])),
                  pl.BlockSpec((LANES, w512), lambda b, i: (0, 0)),
                  pl.BlockSpec((1, w512), lambda b, i: (0, 0)),
                  pl.BlockSpec((1, w512), lambda b, i: (0, 0)),
                  pl.BlockSpec((tm, tm), lambda b, i: (0, 0))],
        out_specs=pl.BlockSpec((tm, w512), lambda b, i: (row(b, i), 0)),
        out_shape=jax.ShapeDtypeStruct((t, w512), BF16),
        scratch_shapes=[pltpu.VMEM((GLA_HEADS, GLA_DV, LANES), F32),
                        pltpu.VMEM((tm, w512), F32)],
        compiler_params=_params("arbitrary", "arbitrary"),
        name="gla",
    )(u, u, u, u, u, wg, bg, onorm, tri)


def _causal_conv(ext_ref, x, w_ref, b_ref, first):
    tm = x.shape[0]

    @pl.when(first)
    def _():
        ext_ref[0:8, :] = jnp.zeros((8, ext_ref.shape[1]), F32)

    ext_ref[8:8 + tm, :] = x
    y = b_ref[...] + w_ref[CONV_K - 1:CONV_K, :] * x
    for kk in range(CONV_K - 1):
        off = 8 - (CONV_K - 1) + kk
        y = y + w_ref[kk:kk + 1, :] * ext_ref[off:off + tm, :]
    ext_ref[0:8, :] = ext_ref[tm:tm + 8, :]
    return y


def _ssd_kernel(xs_ref, z_ref, bc_ref, dte_ref, dto_ref,
                cwx_ref, cbx_ref, cwb_ref, cbb_ref, dtb_ref, alog_ref, dexp_ref, ng_ref,
                expand_ref, tri_ref, o_ref,
                extx_ref, extb_ref, st_ref, cume_ref, cumo_ref, cumx_ref, xdt_ref, bcc_ref, xsc_ref):
    tm = xs_ref.shape[0]
    first = pl.program_id(1) == 0

    @pl.when(first)
    def _():
        st_ref[...] = jnp.zeros_like(st_ref)

    xs = _silu(_causal_conv(extx_ref, xs_ref[...].astype(F32), cwx_ref, cbx_ref, first))
    bcv = _silu(_causal_conv(extb_ref, bc_ref[...].astype(F32), cwb_ref, cbb_ref, first))
    xsc_ref[...] = xs
    bcc_ref[...] = bcv.astype(BF16)

    npair = SSD_HEADS // 2
    lane = _lane((tm, LANES))
    valid = lane < npair
    a_e = -jnp.exp(alog_ref[0:1, :])
    a_o = -jnp.exp(alog_ref[1:2, :])
    dt_e = jnp.where(valid, _softplus(dte_ref[...].astype(F32) + dtb_ref[0:1, :]), 0.0)
    dt_o = jnp.where(valid, _softplus(dto_ref[...].astype(F32) + dtb_ref[1:2, :]), 0.0)
    tri = tri_ref[...]
    cum_e = _dot_exact_lhs(tri, dt_e * a_e)
    cum_o = _dot_exact_lhs(tri, dt_o * a_o)
    cume_ref[...] = cum_e
    cumo_ref[...] = cum_o
    ex_e = expand_ref[0]
    ex_o = expand_ref[1]
    cumx_ref[...] = _dot_exact_rhs(cum_e, ex_e) + _dot_exact_rhs(cum_o, ex_o)
    xdt_ref[...] = xs * (_dot_exact_rhs(dt_e, ex_e) + _dot_exact_rhs(dt_o, ex_o))

    rr = _row((CHUNK, LANES))
    ll = _lane((CHUNK, LANES))
    causal2 = rr >= (ll & (CHUNK - 1))
    left = ll < CHUNK
    gw = SSD_GINNER
    for c in range(tm // CHUNK):
        rows = pl.ds(c * CHUNK, CHUNK)
        cumx = cumx_ref[rows, :]
        lastx = cumx[CHUNK - 1:CHUNK, :]
        xdt = xdt_ref[rows, :]
        xw = (xdt * jnp.exp(lastx - cumx)).astype(BF16)
        ecum = jnp.exp(cumx)
        pt = jnp.concatenate([cume_ref[rows, :], cumo_ref[rows, :]], axis=0).T
        ys = []
        for g in range(SSD_GROUPS):
            bg = bcc_ref[rows, pl.ds(g * SSD_STATE, SSD_STATE)]
            cg = bcc_ref[rows, pl.ds((SSD_GROUPS + g) * SSD_STATE, SSD_STATE)]
            st = st_ref[g]
            y_off = _dot(cg, st.astype(BF16)) * ecum[:, g * gw:(g + 1) * gw]
            cbcb = _dot_nt(cg, jnp.concatenate([bg, bg], axis=0))
            parts = []
            for j in range(npair // SSD_GROUPS):
                jp = g * (npair // SSD_GROUPS) + j
                colp = cumx[:, jp * LANES:(jp + 1) * LANES]
                seg = colp - pt[jp:jp + 1, :]
                dec = jnp.exp(jnp.where(causal2, seg, -jnp.inf))
                w = (cbcb * dec).astype(BF16)
                xp = xdt[:, jp * LANES:(jp + 1) * LANES]
                x2 = jnp.concatenate([jnp.where(left, xp, 0.0), jnp.where(left, 0.0, xp)],
                                     axis=0).astype(BF16)
                parts.append(_dot(w, x2))
            ys.append(jnp.concatenate(parts, axis=1) + y_off)
            st_ref[g] = (st * jnp.exp(lastx[:, g * gw:(g + 1) * gw])
                         + _dot_tn(bg, xw[:, g * gw:(g + 1) * gw]))
        y = jnp.concatenate(ys, axis=1)
        y = y + dexp_ref[...] * xsc_ref[rows, :]
        y = y * _silu(z_ref[rows, :].astype(F32))
        for g in range(SSD_GROUPS):
            yg = y[:, g * gw:(g + 1) * gw]
            ms = jnp.mean(yg * yg, axis=-1, keepdims=True)
            o_ref[rows, pl.ds(g * gw, gw)] = (
                yg * lax.rsqrt(ms + EPS) * ng_ref[:, g * gw:(g + 1) * gw]).astype(o_ref.dtype)


def ssd(u, cb, cwx, cbx, cwb, cbb, dtb, alog, dexp, ng, expand, tri, batch, seq, tm):
    t = u.shape[0]
    tpb = seq // tm
    row = lambda b, i: b * tpb + i
    bcw = 2 * SSD_GROUPS * SSD_STATE
    const = lambda shape: pl.BlockSpec(shape, lambda b, i: tuple(0 for _ in shape))
    return pl.pallas_call(
        _ssd_kernel,
        grid=(batch, tpb),
        in_specs=[pl.BlockSpec((tm, SSD_INNER), lambda b, i: (row(b, i), cb["xs"])),
                  pl.BlockSpec((tm, SSD_INNER), lambda b, i: (row(b, i), cb["z"])),
                  pl.BlockSpec((tm, bcw), lambda b, i: (row(b, i), cb["bc"])),
                  pl.BlockSpec((tm, LANES), lambda b, i: (row(b, i), cb["dte"])),
                  pl.BlockSpec((tm, LANES), lambda b, i: (row(b, i), cb["dto"])),
                  const((CONV_K, SSD_INNER)), const((1, SSD_INNER)),
                  const((CONV_K, bcw)), const((1, bcw)),
                  const((2, LANES)), const((2, LANES)),
                  const((1, SSD_INNER)), const((1, SSD_INNER)),
                  const((2, LANES, SSD_INNER)), const((tm, tm))],
        out_specs=pl.BlockSpec((tm, SSD_INNER), lambda b, i: (row(b, i), 0)),
        out_shape=jax.ShapeDtypeStruct((t, SSD_INNER), BF16),
        scratch_shapes=[pltpu.VMEM((tm + 8, SSD_INNER), F32),
                        pltpu.VMEM((tm + 8, bcw), F32),
                        pltpu.VMEM((SSD_GROUPS, SSD_STATE, SSD_GINNER), F32),
                        pltpu.VMEM((tm, LANES), F32),
                        pltpu.VMEM((tm, LANES), F32),
                        pltpu.VMEM((tm, SSD_INNER), F32),
                        pltpu.VMEM((tm, SSD_INNER), F32),
                        pltpu.VMEM((tm, bcw), BF16),
                        pltpu.VMEM((tm, SSD_INNER), F32)],
        compiler_params=_params("arbitrary", "arbitrary"),
        name="ssd",
    )(u, u, u, u, u, cwx, cbx, cwb, cbb, dtb, alog, dexp, ng, expand, tri)


def _gelu_tanh(x):
    return 0.5 * x * (1.0 + jnp.tanh(math.sqrt(2.0 / math.pi) * (x + 0.044715 * (x * x * x))))


def _lru_kernel(gate_ref, xr_ref, cw_ref, cb_ref, wa_ref, ba_ref, wx_ref, bx_ref, lam_ref,
                o_ref, ext_ref, carry_ref):
    tm, width = xr_ref.shape
    first = pl.program_id(1) == 0

    @pl.when(first)
    def _():
        carry_ref[...] = jnp.zeros_like(carry_ref)

    x = _causal_conv(ext_ref, xr_ref[...].astype(F32), cw_ref, cb_ref, first)
    xb = x.astype(BF16)
    nblk = wa_ref.shape[0]
    bw = width // nblk
    ra = jnp.concatenate([_dot(xb[:, n * bw:(n + 1) * bw], wa_ref[n]) for n in range(nblk)], axis=1)
    rx = jnp.concatenate([_dot(xb[:, n * bw:(n + 1) * bw], wx_ref[n]) for n in range(nblk)], axis=1)
    r = jax.nn.sigmoid(ra + ba_ref[...])
    ig = jax.nn.sigmoid(rx + bx_ref[...])
    log_a = (-LRU_C) * r * _softplus(-lam_ref[...])
    a = jnp.exp(log_a)
    u = jnp.sqrt(1.0 - jnp.exp(2.0 * log_a)) * (ig * x)

    ngrp = tm // SUBLANES
    a3 = a.reshape(ngrp, SUBLANES, width)
    u3 = u.reshape(ngrp, SUBLANES, width)
    sub = lax.broadcasted_iota(jnp.int32, a3.shape, 1)
    d = 1
    while d < SUBLANES:
        m = sub >= d
        a_s = pltpu.roll(a3, d, 1)
        u_s = pltpu.roll(u3, d, 1)
        u3 = jnp.where(m, a3 * u_s + u3, u3)
        a3 = jnp.where(m, a3 * a_s, a3)
        d *= 2
    carry = carry_ref[0:1, :]
    groups = []
    for j in range(ngrp):
        hj = a3[j] * carry + u3[j]
        groups.append(hj)
        carry = hj[SUBLANES - 1:SUBLANES, :]
    hseq = jnp.concatenate(groups, axis=0)
    carry_ref[...] = jnp.broadcast_to(carry, carry_ref.shape)
    o_ref[...] = (hseq * _gelu_tanh(gate_ref[...].astype(F32))).astype(o_ref.dtype)


def lru(u, cb, cw, cbias, wa, ba, wx, bx, lam, batch, seq, tm):
    t = u.shape[0]
    width = cw.shape[1]
    tpb = seq // tm
    row = lambda b, i: b * tpb + i
    const = lambda shape: pl.BlockSpec(shape, lambda b, i: tuple(0 for _ in shape))
    return pl.pallas_call(
        _lru_kernel,
        grid=(batch, tpb),
        in_specs=[pl.BlockSpec((tm, width), lambda b, i: (row(b, i), cb["gate"])),
                  pl.BlockSpec((tm, width), lambda b, i: (row(b, i), cb["xr"])),
                  const((CONV_K, width)), const((1, width)),
                  const(wa.shape), const((1, width)),
                  const(wx.shape), const((1, width)), const((1, width))],
        out_specs=pl.BlockSpec((tm, width), lambda b, i: (row(b, i), 0)),
        out_shape=jax.ShapeDtypeStruct((t, width), BF16),
        scratch_shapes=[pltpu.VMEM((tm + 8, width), F32),
                        pltpu.VMEM((8, width), F32)],
        compiler_params=_params("arbitrary", "arbitrary"),
        name="lru",
    )(u, u, cw, cbias, wa, ba, wx, bx, lam)


def _mla_prep_kernel(uq_ref, ukv_ref, krr_ref, pos_ref, freq_ref, qn_ref, kvn_ref,
                     wq_ref, wkv_ref, qg_ref, kg_ref, q_ref, k_ref, v_ref):
    tm = uq_ref.shape[0]
    lane = _lane((tm, LANES))
    lo_half = lane < MLA_ROPE

    ang = pos_ref[...].astype(F32) * freq_ref[...]
    cs = jnp.where(lo_half, jnp.cos(ang),
                   jnp.where(lane < MLA_ROPE + MLA_ROPE // 2, -jnp.sin(ang), jnp.sin(ang)))

    def latent_norm(ref, g_ref):
        x = ref[...].astype(F32)
        ms = jnp.mean(x * x, axis=-1, keepdims=True)
        return (x * lax.rsqrt(ms + EPS) * g_ref[...]).astype(BF16)

    qf = _dot(latent_norm(uq_ref, qn_ref), wq_ref[...])
    kvf = _dot(latent_norm(ukv_ref, kvn_ref), wkv_ref[...])
    krr = krr_ref[...].astype(F32)
    kr_sq = jnp.where(lo_half, krr * krr, 0.0)
    scale = MLA_QK ** -0.5 * math.log2(math.e)

    def rope_half(y2):
        t = y2 * cs
        return jnp.where(lo_half, t + pltpu.roll(t, MLA_ROPE, 1), 0.0)

    kr_rope = rope_half(krr * kg_ref[:, LANES:2 * LANES])

    for h in range(MLA_HEADS):
        x1 = qf[:, h * MLA_QK_PAD:h * MLA_QK_PAD + LANES]
        x2 = qf[:, h * MLA_QK_PAD + LANES:(h + 1) * MLA_QK_PAD]
        ss = jnp.sum(x1 * x1 + jnp.where(lo_half, x2 * x2, 0.0), axis=-1, keepdims=True)
        r = lax.rsqrt(ss * (1.0 / MLA_QK) + EPS) * scale
        q_ref[:, pl.ds(h * MLA_QK_PAD, LANES)] = (x1 * r * qg_ref[:, 0:LANES]).astype(q_ref.dtype)
        q_ref[:, pl.ds(h * MLA_QK_PAD + LANES, LANES)] = rope_half(
            x2 * r * qg_ref[:, LANES:2 * LANES]).astype(q_ref.dtype)

        kn = kvf[:, h * LANES:(h + 1) * LANES]
        ssk = jnp.sum(kn * kn + kr_sq, axis=-1, keepdims=True)
        rk = lax.rsqrt(ssk * (1.0 / MLA_QK) + EPS)
        k_ref[:, pl.ds(h * MLA_QK_PAD, LANES)] = (kn * rk * kg_ref[:, 0:LANES]).astype(k_ref.dtype)
        k_ref[:, pl.ds(h * MLA_QK_PAD + LANES, LANES)] = (kr_rope * rk).astype(k_ref.dtype)
        v_ref[:, pl.ds(h * MLA_V, MLA_V)] = kvf[:, (MLA_HEADS + h) * LANES:
                                                (MLA_HEADS + h + 1) * LANES].astype(v_ref.dtype)


def mla_prep(u, cb, pos2d, freq, qn, kvn, wq, wkv, qg, kg, tm):
    t = u.shape[0]
    const = lambda shape: pl.BlockSpec(shape, lambda i: tuple(0 for _ in shape))
    hq = MLA_HEADS * MLA_QK_PAD
    hv = MLA_HEADS * MLA_V
    return pl.pallas_call(
        _mla_prep_kernel,
        grid=(t // tm,),
        in_specs=[pl.BlockSpec((tm, MLA_Q_LORA), lambda i: (i, cb["uq"])),
                  pl.BlockSpec((tm, MLA_KV_LORA), lambda i: (i, cb["ukv"])),
                  pl.BlockSpec((tm, LANES), lambda i: (i, cb["krr"])),
                  pl.BlockSpec((tm, 1), lambda i: (i, 0)),
                  const((1, LANES)), const((1, MLA_Q_LORA)), const((1, MLA_KV_LORA)),
                  const(wq.shape), const(wkv.shape), const((1, 2 * LANES)), const((1, 2 * LANES))],
        out_specs=[pl.BlockSpec((tm, hq), lambda i: (i, 0)),
                   pl.BlockSpec((tm, hq), lambda i: (i, 0)),
                   pl.BlockSpec((tm, hv), lambda i: (i, 0))],
        out_shape=[jax.ShapeDtypeStruct((t, hq), BF16),
                   jax.ShapeDtypeStruct((t, hq), BF16),
                   jax.ShapeDtypeStruct((t, hv), BF16)],
        compiler_params=_params("arbitrary"),
        name="mla_prep",
    )(u, u, u, pos2d, freq, qn, kvn, wq, wkv, qg, kg)


def _attn_kernel(q_ref, k_ref, v_ref, o_ref, m_ref, acc_ref):
    i = pl.program_id(2)
    tq = q_ref.shape[0]
    nh = q_ref.shape[1] // MLA_QK_PAD
    nc = tq // LANES

    m_ref[...] = jnp.full_like(m_ref, -jnp.inf)
    acc_ref[...] = jnp.zeros_like(acc_ref)
    ones = jnp.ones((tq, LANES), BF16)

    def update(g, s, v):
        m_prev = m_ref[g]
        m_cur = s[:, 0:LANES]
        for c in range(1, nc):
            m_cur = jnp.maximum(m_cur, s[:, c * LANES:(c + 1) * LANES])
        m_new = jnp.maximum(m_prev, jnp.max(m_cur, axis=-1, keepdims=True))
        alpha = jnp.exp2(m_prev - m_new)
        p = jnp.concatenate(
            [jnp.exp2((s[:, c * LANES:(c + 1) * LANES] - m_new).astype(BF16)) for c in range(nc)],
            axis=1)
        pv = _dot(p, jnp.concatenate([v, ones], axis=1))
        acc_ref[g] = jnp.concatenate([alpha, alpha], axis=1) * acc_ref[g] + pv
        m_ref[g] = m_new

    def block(j, masked):
        rows = pl.ds(pl.multiple_of(j * tq, tq), tq)
        for g in range(nh):
            s = _dot_nt(q_ref[:, g * MLA_QK_PAD:(g + 1) * MLA_QK_PAD],
                        k_ref[rows, pl.ds(g * MLA_QK_PAD, MLA_QK_PAD)])
            if masked:
                s = jnp.where(_row((tq, tq)) >= _lane((tq, tq)), s, -jnp.inf)
            update(g, s, v_ref[rows, pl.ds(g * MLA_V, MLA_V)])

    def body(jj, carry):
        block(2 * jj, False)
        block(2 * jj + 1, False)
        return carry

    lax.fori_loop(0, i >> 1, body, 0)

    @pl.when((i & 1) == 1)
    def _():
        block(i - 1, False)

    block(i, True)
    for g in range(nh):
        acc = acc_ref[g]
        o_ref[:, pl.ds(g * MLA_V, MLA_V)] = (acc[:, :MLA_V] / acc[:, MLA_V:]).astype(o_ref.dtype)


def attention(q, k, v, batch, seq, tq):
    t = q.shape[0]
    nq = seq // tq
    nh = ATTN_HEADS_PER_STEP
    return pl.pallas_call(
        _attn_kernel,
        grid=(batch, MLA_HEADS // nh, nq),
        in_specs=[pl.BlockSpec((tq, nh * MLA_QK_PAD), lambda b, h, i: (b * nq + i, h)),
                  pl.BlockSpec((seq, nh * MLA_QK_PAD), lambda b, h, i: (b, h),
                               pipeline_mode=pl.Buffered(1)),
                  pl.BlockSpec((seq, nh * MLA_V), lambda b, h, i: (b, h),
                               pipeline_mode=pl.Buffered(1))],
        out_specs=pl.BlockSpec((tq, nh * MLA_V), lambda b, h, i: (b * nq + i, h)),
        out_shape=jax.ShapeDtypeStruct((t, MLA_HEADS * MLA_V), BF16),
        scratch_shapes=[pltpu.VMEM((nh, tq, LANES), F32),
                        pltpu.VMEM((nh, tq, MLA_V + LANES), F32)],
        compiler_params=_params("arbitrary", "arbitrary", "arbitrary"),
        name="attention",
    )(q, k, v)


def _outproj_kernel(oa_ref, ob_ref, w1_ref, w2_ref, h_ref, g1_ref, nf_ref, sh_ref, sc_ref,
                    rwt_ref, hn_ref, hf_ref, lg_ref):
    y = _dot(oa_ref[...], w1_ref[...]) + _dot(ob_ref[...], w2_ref[...])
    hn = h_ref[...] + g1_ref[0] * y
    hn_ref[...] = hn
    ms = jnp.mean(hn * hn, axis=-1, keepdims=True)
    hf = hn * lax.rsqrt(ms + EPS) * nf_ref[...] * (1.0 + sc_ref[0]) + sh_ref[0]
    hf_ref[...] = hf.astype(BF16)
    r_hi, r_mid, r_lo = _split3(rwt_ref[...])
    f_hi, f_mid, f_lo = _split3(hf)
    lg_ref[...] = (_dot_nt(r_hi, f_hi) + _dot_nt(r_hi, f_mid) + _dot_nt(r_mid, f_hi)
                   + _dot_nt(r_hi, f_lo) + _dot_nt(r_lo, f_hi) + _dot_nt(r_mid, f_mid))


def outproj(oa, ob, w1, w2, h2d, g1, nf, sh2, sc2, rwt, seq, tm):
    t, d = h2d.shape
    tpb = seq // tm
    const = lambda shape: pl.BlockSpec(shape, lambda i: tuple(0 for _ in shape))
    bvec = pl.BlockSpec((1, 1, d), lambda i: (i // tpb, 0, 0))
    return pl.pallas_call(
        _outproj_kernel,
        grid=(t // tm,),
        in_specs=[pl.BlockSpec((tm, oa.shape[1]), lambda i: (i, 0)),
                  pl.BlockSpec((tm, ob.shape[1]), lambda i: (i, 0)),
                  const(w1.shape), const(w2.shape),
                  pl.BlockSpec((tm, d), lambda i: (i, 0)),
                  bvec, const((1, d)), bvec, bvec, const(rwt.shape)],
        out_specs=[pl.BlockSpec((tm, d), lambda i: (i, 0)),
                   pl.BlockSpec((tm, d), lambda i: (i, 0)),
                   pl.BlockSpec((N_EXPERTS, tm), lambda i: (0, i))],
        out_shape=[jax.ShapeDtypeStruct((t, d), F32),
                   jax.ShapeDtypeStruct((t, d), BF16),
                   jax.ShapeDtypeStruct((N_EXPERTS, t), F32)],
        compiler_params=_params("arbitrary"),
        name="outproj",
    )(oa, ob, w1, w2, h2d, g1[:, None, :], nf.reshape(1, d), sh2[:, None, :], sc2[:, None, :], rwt)


def _route_sort_kernel(*refs):
    last = pl.program_id(0) == pl.num_programs(0) - 1

    @pl.when(last)
    def _():
        xs_ref = refs[5]
        xs_ref[...] = jnp.zeros_like(xs_ref)

    @pl.when(jnp.logical_not(last))
    def _():
        _route_sort_body(*refs)


def _route_sort_body(lg_ref, bias_ref, hf_ref, utri_ref, ltri_ref,
                     xs_ref, lpos_ref, wts_ref, cnt_ref):
    scores = jax.nn.sigmoid(lg_ref[...])
    sel = scores + bias_ref[...]
    tm = sel.shape[1]
    eidx = _row((N_EXPERTS, tm))
    neg = -jnp.inf

    best_score = None
    best_grp = None
    for g in range(N_GROUPS):
        m = [sel[g * EXPERTS_PER_GROUP + i:g * EXPERTS_PER_GROUP + i + 1, :]
             for i in range(EXPERTS_PER_GROUP)]
        gs = None
        for a in range(EXPERTS_PER_GROUP):
            for b in range(a + 1, EXPERTS_PER_GROUP):
                pair = m[a] + m[b]
                gs = pair if gs is None else jnp.maximum(gs, pair)
        if best_score is None:
            best_score, best_grp = gs, jnp.zeros_like(gs, dtype=jnp.int32)
        else:
            better = gs > best_score
            best_score = jnp.where(better, gs, best_score)
            best_grp = jnp.where(better, g, best_grp)

    masked = jnp.where((eidx >> 2) == best_grp, sel, neg)
    m1 = jnp.max(masked, axis=0, keepdims=True)
    i1 = jnp.min(jnp.where(masked == m1, eidx, N_EXPERTS), axis=0, keepdims=True)
    masked2 = jnp.where(eidx == i1, neg, masked)
    m2 = jnp.max(masked2, axis=0, keepdims=True)
    i2 = jnp.min(jnp.where(masked2 == m2, eidx, N_EXPERTS), axis=0, keepdims=True)
    pick1 = eidx == i1
    pick2 = eidx == i2
    s1 = jnp.sum(jnp.where(pick1, scores, 0.0), axis=0, keepdims=True)
    s2 = jnp.sum(jnp.where(pick2, scores, 0.0), axis=0, keepdims=True)
    tot = s1 + s2
    wts_ref[...] = jnp.concatenate([s1 / tot, s2 / tot], axis=0)

    nr = xs_ref.shape[0]
    picks = jnp.where(pick1, 1.0, jnp.where(pick2, 1.0, 0.0))
    csum = _dot(picks.astype(BF16), utri_ref[...])
    cnt = jnp.sum(picks, axis=1, keepdims=True)
    cnt_ref[0] = jnp.broadcast_to(cnt, (N_EXPERTS, LANES))
    cnt_pad = jnp.floor((cnt + (ROW_CHUNK - 1.0)) * (1.0 / ROW_CHUNK)) * ROW_CHUNK
    seg_off = _dot(ltri_ref[...], jnp.broadcast_to(cnt_pad, (N_EXPERTS, LANES)).astype(BF16))[:, 0:1]
    lposmat = seg_off + csum - 1.0
    lp1 = jnp.sum(jnp.where(pick1, lposmat, 0.0), axis=0, keepdims=True).astype(jnp.int32)
    lp2 = jnp.sum(jnp.where(pick2, lposmat, 0.0), axis=0, keepdims=True).astype(jnp.int32)
    lpos_ref[...] = jnp.concatenate([lp1, lp2], axis=0)
    rowi = _row((nr, tm))
    onehot = jnp.where(rowi == lp1, 1.0, jnp.where(rowi == lp2, 1.0, 0.0)).astype(BF16)
    xs_ref[...] = _dot(onehot, hf_ref[...]).astype(xs_ref.dtype)


def route_sort(logits_t, router_bias, hf, tl, nr):
    e, t = logits_t.shape
    d = hf.shape[1]
    ntile = t // tl
    r = jnp.arange(tl)
    utri = (r[:, None] <= r[None, :]).astype(BF16)
    re = jnp.arange(e)
    ltri = (re[None, :] < re[:, None]).astype(BF16)
    const = lambda shape: pl.BlockSpec(shape, lambda i: tuple(0 for _ in shape))
    tile = lambda i: jnp.minimum(i, ntile - 1)
    return pl.pallas_call(
        _route_sort_kernel,
        grid=(ntile + 1,),
        in_specs=[pl.BlockSpec((e, tl), lambda i: (0, tile(i))),
                  const((e, 1)),
                  pl.BlockSpec((tl, d), lambda i: (tile(i), 0)),
                  const((tl, tl)), const((e, e))],
        out_specs=[pl.BlockSpec((nr, d), lambda i: (i, 0)),
                   pl.BlockSpec((2, tl), lambda i: (0, tile(i))),
                   pl.BlockSpec((2, tl), lambda i: (0, tile(i))),
                   pl.BlockSpec((1, e, LANES), lambda i: (tile(i), 0, 0))],
        out_shape=[jax.ShapeDtypeStruct(((ntile + 1) * nr, d), BF16),
                   jax.ShapeDtypeStruct((2, t), jnp.int32),
                   jax.ShapeDtypeStruct((2, t), F32),
                   jax.ShapeDtypeStruct((t // tl, e, LANES), F32)],
        compiler_params=_params("arbitrary"),
        name="route_sort",
    )(logits_t, router_bias.reshape(e, 1).astype(F32), hf, utri, ltri)


def _moe_plan(cnt, nr, tmg, nt_max):
    ntile, ne = cnt.shape
    cpt = tmg // ROW_CHUNK
    nch = (cnt + ROW_CHUNK - 1) // ROW_CHUNK
    seg_off = jnp.cumsum(nch, axis=1) - nch
    cum = jnp.cumsum(nch, axis=0)
    tot = cum[-1]
    padded = (tot + cpt - 1) // cpt * cpt
    gend = jnp.cumsum(padded)
    gstart = gend - padded
    start = (gstart[None, :] + cum - nch).T.reshape(-1)
    end = start + nch.T.reshape(-1)
    base = (jnp.arange(ntile, dtype=jnp.int32)[:, None] * (nr // ROW_CHUNK) + seg_off).T.reshape(-1)
    c = jnp.arange(nt_max * cpt, dtype=jnp.int32)[:, None]
    hit = (c >= start[None, :]) & (c < end[None, :])
    valid = jnp.any(hit, axis=1)
    src = jnp.sum(jnp.where(hit, base[None, :] - start[None, :] + c, 0), axis=1) * ROW_CHUNK
    src = jnp.where(valid, src, nr - ROW_CHUNK)
    cflat = c[:, 0]
    spare = ntile * nr + ((cflat // cpt) % 2) * tmg + (cflat % cpt) * ROW_CHUNK
    dst = jnp.where(valid, src, spare)
    first = jnp.arange(nt_max, dtype=jnp.int32) * cpt
    e_first = jnp.sum(first[:, None] >= gend[None, :], axis=1).astype(jnp.int32)
    tile_on = (e_first < ne).astype(jnp.int32)
    return src.astype(jnp.int32), dst.astype(jnp.int32), jnp.minimum(e_first, ne - 1), tile_on


def _moe_group_kernel(src_ref, dst_ref, te_ref, on_ref, xs_hbm, wg_ref, wu_ref, wd_ref,
                      y_hbm, xbuf, ybuf, wgb, wub, wdb, in_sem, out_sem):
    del xs_hbm
    g = pl.program_id(0)
    ng = pl.num_programs(0)
    slot = lax.rem(g, 2)
    tmg = xbuf.shape[1]
    cpt = tmg // ROW_CHUNK

    def in_copy(tile, sl, k):
        row = pl.multiple_of(src_ref[tile * cpt + k], ROW_CHUNK)
        return pltpu.make_async_copy(y_hbm.at[pl.ds(row, ROW_CHUNK), :],
                                     xbuf.at[sl, pl.ds(k * ROW_CHUNK, ROW_CHUNK), :], in_sem.at[sl])

    def out_copy(tile, sl, k):
        row = pl.multiple_of(dst_ref[tile * cpt + k], ROW_CHUNK)
        return pltpu.make_async_copy(ybuf.at[sl, pl.ds(k * ROW_CHUNK, ROW_CHUNK), :],
                                     y_hbm.at[pl.ds(row, ROW_CHUNK), :], out_sem.at[sl])

    def start_in(tile, sl):
        for k in range(cpt):
            in_copy(tile, sl, k).start()

    def wait_in(tile, sl):
        for k in range(cpt):
            in_copy(tile, sl, k).wait()

    def start_out(tile, sl):
        for k in range(cpt):
            out_copy(tile, sl, k).start()

    def wait_out(tile, sl):
        for k in range(cpt):
            out_copy(tile, sl, k).wait()

    @pl.when((g >= 2) & (on_ref[jnp.maximum(g - 2, 0)] == 1))
    def _():
        wait_out(g - 2, slot)

    @pl.when((g == 0) & (on_ref[0] == 1))
    def _():
        start_in(0, 0)

    @pl.when((g + 1 < ng) & (on_ref[jnp.minimum(g + 1, ng - 1)] == 1))
    def _():
        start_in(g + 1, 1 - slot)

    @pl.when(on_ref[g] == 1)
    def _():
        @pl.when((g == 0) | (te_ref[g] != te_ref[jnp.maximum(g - 1, 0)]))
        def _():
            wgb[...] = wg_ref[0, 0].astype(BF16)
            wub[...] = wu_ref[0, 0].astype(BF16)
            wdb[...] = wd_ref[0, 0].astype(BF16)

        wait_in(g, slot)
        x = xbuf[slot]
        hid = (_silu(_dot(x, wgb[...])) * _dot(x, wub[...])).astype(BF16)
        ybuf[slot] = _dot(hid, wdb[...]).astype(ybuf.dtype)
        start_out(g, slot)

    @pl.when(g == ng - 1)
    def _():
        @pl.when((ng >= 2) & (on_ref[jnp.maximum(g - 1, 0)] == 1))
        def _():
            wait_out(g - 1, 1 - slot)

        @pl.when(on_ref[g] == 1)
        def _():
            wait_out(g, slot)


def moe_group(xs, wg, wu, wd, layer, plan, tmg, nt_max):
    src, dst, tile_e, tile_on = plan
    rows_out, d = xs.shape
    wspec = lambda w: pl.BlockSpec((1, 1) + w.shape[2:], lambda g, s, v, te, on: (layer, te[g], 0, 0))
    grid_spec = pltpu.PrefetchScalarGridSpec(
        num_scalar_prefetch=4,
        grid=(nt_max,),
        in_specs=[pl.BlockSpec(memory_space=pl.ANY), wspec(wg), wspec(wu), wspec(wd)],
        out_specs=pl.BlockSpec(memory_space=pl.ANY),
        scratch_shapes=[pltpu.VMEM((2, tmg, d), BF16), pltpu.VMEM((2, tmg, d), BF16),
                        pltpu.VMEM(wg.shape[2:], BF16), pltpu.VMEM(wu.shape[2:], BF16),
                        pltpu.VMEM(wd.shape[2:], BF16),
                        pltpu.SemaphoreType.DMA((2,)), pltpu.SemaphoreType.DMA((2,))])
    return pl.pallas_call(
        _moe_group_kernel,
        grid_spec=grid_spec,
        out_shape=jax.ShapeDtypeStruct((rows_out, d), BF16),
        input_output_aliases={4: 0},
        compiler_params=_params("arbitrary"),
        name="moe_group",
    )(src, dst, tile_e, tile_on, xs, wg, wu, wd)


def _moe_combine_kernel(y_ref, lpos_ref, wts_ref, h_ref, g2_ref, o_ref):
    nr = y_ref.shape[0]
    tl = h_ref.shape[0]
    rowi = _row((nr, tl))
    wc = (jnp.where(rowi == lpos_ref[0:1, :], wts_ref[0:1, :], 0.0)
          + jnp.where(rowi == lpos_ref[1:2, :], wts_ref[1:2, :], 0.0)).astype(BF16)
    o_ref[...] = h_ref[...] + g2_ref[0] * _dot_tn(wc, y_ref[...])


def moe_combine(y, lpos, wts, h2d, g2, seq, tl, nr):
    t, d = h2d.shape
    tpb = seq // tl
    return pl.pallas_call(
        _moe_combine_kernel,
        grid=(t // tl,),
        in_specs=[pl.BlockSpec((nr, d), lambda i: (i, 0)),
                  pl.BlockSpec((2, tl), lambda i: (0, i)),
                  pl.BlockSpec((2, tl), lambda i: (0, i)),
                  pl.BlockSpec((tl, d), lambda i: (i, 0)),
                  pl.BlockSpec((1, 1, d), lambda i: (i // tpb, 0, 0))],
        out_specs=pl.BlockSpec((tl, d), lambda i: (i, 0)),
        out_shape=jax.ShapeDtypeStruct((t, d), F32),
        compiler_params=_params("arbitrary"),
        name="moe_combine",
    )(y, lpos, wts, h2d, g2[:, None, :])


def _pad_heads(w, heads, dh):
    lead = w.shape[:-1]
    w = w.reshape(lead + (heads, dh))
    w = jnp.pad(w, [(0, 0)] * len(lead) + [(0, 0), (0, LANES - dh)])
    return w.reshape(lead + (heads * LANES,))


def _pad_cols(w, n):
    return jnp.pad(w, [(0, 0)] * (w.ndim - 1) + [(0, n - w.shape[-1])])


def _tri_blocks(tm):
    r = jnp.arange(tm)
    return ((r[:, None] >= r[None, :]) & (r[:, None] // CHUNK == r[None, :] // CHUNK)).astype(BF16)


def _even_layout(w_in):
    sizes = (GLA_HEADS * GLA_DK, GLA_HEADS * GLA_DK, GLA_HEADS * GLA_DV, GLA_GATE_RANK,
             GLA_HEADS * GLA_DV, SSD_INNER, SSD_INNER + 2 * SSD_GROUPS * SSD_STATE, SSD_HEADS)
    offs = [0]
    for s in sizes:
        offs.append(offs[-1] + s)
    seg = lambda i: w_in[:, offs[i]:offs[i + 1]]
    q, k, v, glr, og, z, xbc, dt = (seg(i) for i in range(8))
    xs, bc = xbc[:, :SSD_INNER], xbc[:, SSD_INNER:]
    cols = [xs, z, v, og, _pad_heads(q, GLA_HEADS, GLA_DK), _pad_heads(k, GLA_HEADS, GLA_DK), bc,
            _pad_cols(glr, LANES), _pad_cols(dt[:, 0::2], LANES), _pad_cols(dt[:, 1::2], LANES)]
    w = jnp.concatenate(cols, axis=1).astype(BF16)
    cb = {"xs": 0, "z": 1, "v": 4, "og": 5, "q": 6, "k": 7, "bc": 8, "glr": 36, "dte": 37, "dto": 38}
    return w, cb


def _odd_layout(w_in):
    d = w_in.shape[0]
    o = [0, d, 2 * d, 2 * d + MLA_Q_LORA, 2 * d + MLA_Q_LORA + MLA_KV_LORA,
         2 * d + MLA_Q_LORA + MLA_KV_LORA + MLA_ROPE]
    gate, xr, uq, ukv, kr = (w_in[:, o[i]:o[i + 1]] for i in range(5))
    half = MLA_ROPE // 2
    kr_sw = jnp.concatenate([kr[:, half:], kr[:, :half]], axis=1)
    w = jnp.concatenate([gate, xr, ukv, uq, kr, kr_sw], axis=1).astype(BF16)
    cb = {"gate": 0, "xr": 1, "ukv": 8, "uq": 6, "krr": 21}
    return w, cb


def _swap_halves(x):
    half = x.shape[-1] // 2
    return jnp.concatenate([x[..., half:], x[..., :half]], axis=-1)


def _block_diag(w, per):
    nb, bw, _ = w.shape
    w = w.reshape(nb // per, per, bw, bw)
    eye = jnp.eye(per, dtype=w.dtype)
    out = jnp.einsum("gpij,pq->gpiqj", w, eye)
    return out.reshape(nb // per, per * bw, per * bw)


def kernel(x, c, positions, router_w, router_bias, ada_w, ada_b, norm_mix, norm_ffn, moe_w_gate, moe_w_up, moe_w_down, ev_w_in, gla_w_g2, gla_b_g2, gla_onorm, ssd_conv_w, ssd_conv_b, ssd_dt_bias, ssd_a_log, ssd_d, ssd_norm, ev_w_out, od_w_in, lru_conv_w, lru_conv_b, lru_w_a, lru_b_a, lru_w_x, lru_b_x, lru_lambda, mla_q_norm, mla_w_q_up, mla_kv_norm, mla_w_kv_up, mla_q_qknorm, mla_k_qknorm, od_w_out):
    batch, seq, d = x.shape
    t = batch * seq
    depth = ada_w.shape[0]
    tm_seq = min(TM_SEQ, seq)
    tm_mm = min(TM_MM, seq)
    tq = min(TQ_ATTN, seq)
    tl = min(TL_MOE, seq)
    nr = 2 * tl + N_EXPERTS * ROW_CHUNK
    tmg = TM_MOE
    chunks_max = 2 * t // ROW_CHUNK + (t // tl) * N_EXPERTS + N_EXPERTS * (tmg // ROW_CHUNK - 1)
    nt_max = -(-chunks_max // (tmg // ROW_CHUNK))

    mod = ada_mod(c, ada_w, ada_b)
    rwt = router_w.T.astype(F32)
    tri = _tri_blocks(tm_seq)
    h = x.reshape(t, d)

    for layer in range(depth):
        sh1, sc1, g1, sh2, sc2, g2 = (mod[layer, :, i * d:(i + 1) * d] for i in range(6))
        i = layer // 2
        if layer % 2 == 0:
            w_in, cb = _even_layout(ev_w_in[i])
            u = inproj(h, norm_mix[layer], sh1, sc1, w_in, seq, tm_mm, w_in.shape[1] // 3)
            wg = jnp.pad(_pad_heads(gla_w_g2[i], GLA_HEADS, GLA_DK),
                         ((0, LANES - GLA_GATE_RANK), (0, 0))).astype(BF16)
            bg = _pad_heads(gla_b_g2[i][None, :], GLA_HEADS, GLA_DK)
            oa = gla(u, cb, wg, bg, gla_onorm[i][None, :], tri, batch, seq, tm_seq)
            cw, cbias = ssd_conv_w[i], ssd_conv_b[i][None, :]
            perm = lambda v: jnp.stack([_pad_cols(v[0::2], LANES), _pad_cols(v[1::2], LANES)])
            npair = SSD_HEADS // 2
            hp = jnp.arange(SSD_INNER) // SSD_HEADDIM
            ex_e = (jnp.arange(LANES)[:, None] * 2 == hp[None, :]) & (jnp.arange(LANES)[:, None] < npair)
            ex_o = (jnp.arange(LANES)[:, None] * 2 + 1 == hp[None, :]) & (jnp.arange(LANES)[:, None] < npair)
            expand = jnp.stack([ex_e, ex_o]).astype(BF16)
            ob = ssd(u, cb, cw[:, :SSD_INNER], cbias[:, :SSD_INNER], cw[:, SSD_INNER:],
                     cbias[:, SSD_INNER:], perm(ssd_dt_bias[i]), perm(ssd_a_log[i]),
                     jnp.repeat(ssd_d[i], SSD_HEADDIM)[None, :], ssd_norm[i][None, :],
                     expand, tri, batch, seq, tm_seq)
            w_out = ev_w_out[i].astype(BF16)
            w1, w2 = w_out[:GLA_HEADS * GLA_DV], w_out[GLA_HEADS * GLA_DV:]
        else:
            w_in, cb = _odd_layout(od_w_in[i])
            u = inproj(h, norm_mix[layer], sh1, sc1, w_in, seq, tm_mm, w_in.shape[1] // 2)
            per = 4
            oa = lru(u, cb, lru_conv_w[i], lru_conv_b[i][None, :],
                     _block_diag(lru_w_a[i], per).astype(BF16), lru_b_a[i][None, :],
                     _block_diag(lru_w_x[i], per).astype(BF16), lru_b_x[i][None, :],
                     lru_lambda[i][None, :], batch, seq, tm_seq)
            wq = mla_w_q_up[i].reshape(MLA_Q_LORA, MLA_HEADS, MLA_QK)
            wq = jnp.concatenate([wq, _swap_halves(wq[..., MLA_NOPE:])], axis=-1)
            wq = wq.reshape(MLA_Q_LORA, MLA_HEADS * MLA_QK_PAD).astype(BF16)
            wkv = mla_w_kv_up[i].reshape(MLA_KV_LORA, MLA_HEADS, MLA_NOPE + MLA_V)
            wkv = jnp.concatenate([wkv[..., :MLA_NOPE].reshape(MLA_KV_LORA, -1),
                                   wkv[..., MLA_NOPE:].reshape(MLA_KV_LORA, -1)], axis=1).astype(BF16)
            ext_gain = lambda gq: jnp.concatenate([gq, _swap_halves(gq[MLA_NOPE:])])[None, :]
            fr = ROPE_THETA ** (-jnp.arange(0, MLA_ROPE, 2, dtype=F32) / MLA_ROPE)
            freq = jnp.tile(fr, 4)[None, :]
            q, k, v = mla_prep(u, cb, positions.reshape(t, 1), freq, mla_q_norm[i][None, :],
                               mla_kv_norm[i][None, :], wq, wkv, ext_gain(mla_q_qknorm[i]),
                               ext_gain(mla_k_qknorm[i]), tm_mm)
            ob = attention(q, k, v, batch, seq, tq)
            w_out = od_w_out[i].astype(BF16)
            w1, w2 = w_out[:d], w_out[d:]
        h, hf, logits_t = outproj(oa, ob, w1, w2, h, g1, norm_ffn[layer], sh2, sc2, rwt, seq, tm_mm)
        assert 2 * tmg <= nr
        xs, lpos, wts, cnt = route_sort(logits_t, router_bias, hf, tl, nr)
        plan = _moe_plan(cnt[:, :, 0].astype(jnp.int32), nr, tmg, nt_max)
        y = moe_group(xs, moe_w_gate, moe_w_up, moe_w_down, layer, plan, tmg, nt_max)
        h = moe_combine(y, lpos, wts, h, g2, seq, tl, nr)
    return h.reshape(batch, seq, d)
```

```python
import functools
import math

import jax
import jax.numpy as jnp
from jax import lax
from jax.experimental import pallas as pl
from jax.experimental.pallas import tpu as pltpu

F32 = jnp.float32
BF16 = jnp.bfloat16

EPS = 1e-6
CHUNK = 64
CONV_K = 4
GLA_HEADS, GLA_DK, GLA_DV = 4, 64, 128
GLA_GATE_RANK, GLA_GATE_NORM = 16, 16.0
SSD_HEADS, SSD_HEADDIM, SSD_GROUPS, SSD_STATE = 16, 64, 2, 128
SSD_INNER = SSD_HEADS * SSD_HEADDIM
SSD_GINNER = SSD_INNER // SSD_GROUPS
LRU_BLOCKS, LRU_C = 16, 8.0
MLA_HEADS, MLA_NOPE, MLA_ROPE, MLA_V = 8, 128, 64, 128
MLA_QK = MLA_NOPE + MLA_ROPE
MLA_QK_PAD = 256
MLA_Q_LORA, MLA_KV_LORA = 384, 256
ROPE_THETA = 10000.0
N_EXPERTS, N_GROUPS, D_EXPERT = 16, 4, 512
EXPERTS_PER_GROUP = N_EXPERTS // N_GROUPS

TM_SEQ = 256
TM_MM = 512
TM_MOE = 512
TL_MOE = 512
ROW_CHUNK = 16
TQ_ATTN = 512
ATTN_HEADS_PER_STEP = 4
ATTN_KV_UNROLL = 4

LANES = 128
SUBLANES = 8
VMEM_LIMIT = 48 * 1024 * 1024

NT_DIMS = (((1,), (1,)), ((), ()))
TN_DIMS = (((0,), (0,)), ((), ()))


def _params(*sem):
    return pltpu.CompilerParams(dimension_semantics=sem, vmem_limit_bytes=VMEM_LIMIT)


def _dot(a, b):
    return jnp.dot(a, b, preferred_element_type=F32)


def _dot_nt(a, b):
    return lax.dot_general(a, b, NT_DIMS, preferred_element_type=F32)


def _dot_tn(a, b):
    return lax.dot_general(a, b, TN_DIMS, preferred_element_type=F32)


def _split3(a):
    hi = a.astype(BF16)
    r1 = a - hi.astype(F32)
    mid = r1.astype(BF16)
    lo = (r1 - mid.astype(F32)).astype(BF16)
    return hi, mid, lo


def _dot_exact_rhs(a, b_bf16):
    hi, mid, lo = _split3(a)
    return _dot(hi, b_bf16) + _dot(mid, b_bf16) + _dot(lo, b_bf16)


def _dot_exact_lhs(a_bf16, b):
    hi, mid, lo = _split3(b)
    return _dot(a_bf16, hi) + _dot(a_bf16, mid) + _dot(a_bf16, lo)


def _softplus(x):
    return jnp.maximum(x, 0.0) + jnp.log1p(jnp.exp(-jnp.abs(x)))


def _silu(x):
    return x * jax.nn.sigmoid(x)


def _lane(shape):
    return lax.broadcasted_iota(jnp.int32, shape, len(shape) - 1)


def _row(shape):
    return lax.broadcasted_iota(jnp.int32, shape, len(shape) - 2)


def _ada_kernel(c_ref, w_ref, b_ref, o_ref):
    c = c_ref[...]
    a_hi, a_mid, a_lo = _split3(_silu(c))
    w_hi, w_mid, w_lo = _split3(w_ref[0])
    acc = (_dot(a_hi, w_hi) + _dot(a_hi, w_mid) + _dot(a_mid, w_hi)
           + _dot(a_hi, w_lo) + _dot(a_lo, w_hi) + _dot(a_mid, w_mid))
    o_ref[0] = acc + b_ref[0]


def ada_mod(c, ada_w, ada_b):
    depth, d, n = ada_w.shape
    b = c.shape[0]
    bp = 8
    cp = jnp.zeros((bp, d), F32).at[:b].set(c)
    tn = 1536
    out = pl.pallas_call(
        _ada_kernel,
        grid=(depth, n // tn),
        in_specs=[pl.BlockSpec((bp, d), lambda l, j: (0, 0)),
                  pl.BlockSpec((1, d, tn), lambda l, j: (l, 0, j)),
                  pl.BlockSpec((1, 1, tn), lambda l, j: (l, 0, j))],
        out_specs=pl.BlockSpec((1, bp, tn), lambda l, j: (l, 0, j)),
        out_shape=jax.ShapeDtypeStruct((depth, bp, n), F32),
        compiler_params=_params("arbitrary", "arbitrary"),
        name="ada_mod",
    )(cp, ada_w, ada_b.reshape(depth, 1, n))
    return out[:, :b]


def _inproj_kernel(x_ref, g_ref, sh_ref, sc_ref, w_ref, o_ref, *, tn):
    x = x_ref[...]
    ms = jnp.mean(x * x, axis=-1, keepdims=True)
    y = x * lax.rsqrt(ms + EPS) * g_ref[...]
    hm = (y * (1.0 + sc_ref[0]) + sh_ref[0]).astype(BF16)
    for j in range(w_ref.shape[1] // tn):
        o_ref[:, j * tn:(j + 1) * tn] = _dot(hm, w_ref[:, j * tn:(j + 1) * tn]).astype(o_ref.dtype)


def inproj(h2d, gain, shift, scale, w, seq, tm, tn):
    t, d = h2d.shape
    n = w.shape[1]
    tpb = seq // tm
    return pl.pallas_call(
        functools.partial(_inproj_kernel, tn=tn),
        grid=(t // tm,),
        in_specs=[pl.BlockSpec((tm, d), lambda i: (i, 0)),
                  pl.BlockSpec((1, d), lambda i: (0, 0)),
                  pl.BlockSpec((1, 1, d), lambda i: (i // tpb, 0, 0)),
                  pl.BlockSpec((1, 1, d), lambda i: (i // tpb, 0, 0)),
                  pl.BlockSpec((d, n), lambda i: (0, 0), pipeline_mode=pl.Buffered(1))],
        out_specs=pl.BlockSpec((tm, n), lambda i: (i, 0)),
        out_shape=jax.ShapeDtypeStruct((t, n), BF16),
        compiler_params=_params("arbitrary"),
        name="inproj",
    )(h2d, gain.reshape(1, d), shift[:, None, :], scale[:, None, :], w)


def _gla_kernel(q_ref, k_ref, v_ref, og_ref, misc_ref, wg_ref, bg_ref, on_ref, tri_ref,
                o_ref, st_ref, cum_ref):
    tm = q_ref.shape[0]

    @pl.when(pl.program_id(1) == 0)
    def _():
        st_ref[...] = jnp.zeros_like(st_ref)

    g = _dot(misc_ref[...], wg_ref[...]) + bg_ref[...]
    la = (jnp.minimum(g, 0.0) - jnp.log1p(jnp.exp(-jnp.abs(g)))) * (1.0 / GLA_GATE_NORM)
    la = jnp.where((_lane(la.shape) & (LANES - 1)) < GLA_DK, la, 0.0)
    cum_ref[...] = _dot_exact_lhs(tri_ref[...], la)

    causal = _row((CHUNK, CHUNK)) >= _lane((CHUNK, CHUNK))
    for c in range(tm // CHUNK):
        rows = pl.ds(c * CHUNK, CHUNK)
        for h in range(GLA_HEADS):
            cols = pl.ds(h * LANES, LANES)
            cu = cum_ref[rows, cols]
            last = cu[CHUNK - 1:CHUNK, :]
            qh = q_ref[rows, cols].astype(F32) * (GLA_DK ** -0.5)
            kh = k_ref[rows, cols].astype(F32)
            q_dec = (qh * jnp.exp(cu)).astype(BF16)
            k_inv = (kh * jnp.exp(-cu)).astype(BF16)
            k_end = (kh * jnp.exp(last - cu)).astype(BF16)
            vh = v_ref[rows, cols]
            att = jnp.where(causal, _dot_nt(q_dec, k_inv), 0.0)
            st = st_ref[h]
            o = _dot(att.astype(BF16), vh) + _dot_nt(q_dec, st.astype(BF16))
            st_ref[h] = st * jnp.exp(last) + _dot_tn(vh, k_end)
            ms = jnp.mean(o * o, axis=-1, keepdims=True)
            on = o * lax.rsqrt(ms + EPS) * on_ref[:, cols]
            o_ref[rows, cols] = (on * _silu(og_ref[rows, cols].astype(F32))).astype(o_ref.dtype)


def gla(u, cb, wg, bg, onorm, tri, batch, seq, tm):
    t = u.shape[0]
    tpb = seq // tm
    w512 = GLA_HEADS * LANES
    row = lambda b, i: b * tpb + i
    return pl.pallas_call(
        _gla_kernel,
        grid=(batch, tpb),
        in_specs=[pl.BlockSpec((tm, w512), lambda b, i: (row(b, i), cb["q"])),
                  pl.BlockSpec((tm, w512), lambda b, i: (row(b, i), cb["k"])),
                  pl.BlockSpec((tm, w512), lambda b, i: (row(b, i), cb["v"])),
                  pl.BlockSpec((tm, w512), lambda b, i: (row(b, i), cb["og"])),
                  pl.BlockSpec((tm, LANES), lambda b, i: (row(b, i), cb["glr"])),
                  pl.BlockSpec((LANES, w512), lambda b, i: (0, 0)),
                  pl.BlockSpec((1, w512), lambda b, i: (0, 0)),
                  pl.BlockSpec((1, w512), lambda b, i: (0, 0)),
                  pl.BlockSpec((tm, tm), lambda b, i: (0, 0))],
        out_specs=pl.BlockSpec((tm, w512), lambda b, i: (row(b, i), 0)),
        out_shape=jax.ShapeDtypeStruct((t, w512), BF16),
        scratch_shapes=[pltpu.VMEM((GLA_HEADS, GLA_DV, LANES), F32),
                        pltpu.VMEM((tm, w512), F32)],
        compiler_params=_params("arbitrary", "arbitrary"),
        name="gla",
    )(u, u, u, u, u, wg, bg, onorm, tri)


def _causal_conv(ext_ref, x, w_ref, b_ref, first):
    tm = x.shape[0]

    @pl.when(first)
    def _():
        ext_ref[0:8, :] = jnp.zeros((8, ext_ref.shape[1]), F32)

    ext_ref[8:8 + tm, :] = x
    y = b_ref[...] + w_ref[CONV_K - 1:CONV_K, :] * x
    for kk in range(CONV_K - 1):
        off = 8 - (CONV_K - 1) + kk
        y = y + w_ref[kk:kk + 1, :] * ext_ref[off:off + tm, :]
    ext_ref[0:8, :] = ext_ref[tm:tm + 8, :]
    return y


def _ssd_kernel(xs_ref, z_ref, bc_ref, dte_ref, dto_ref,
                cwx_ref, cbx_ref, cwb_ref, cbb_ref, dtb_ref, alog_ref, dexp_ref, ng_ref,
                expand_ref, tri_ref, o_ref,
                extx_ref, extb_ref, st_ref, cume_ref, cumo_ref, cumx_ref, xdt_ref, bcc_ref, xsc_ref):
    tm = xs_ref.shape[0]
    first = pl.program_id(1) == 0

    @pl.when(first)
    def _():
        st_ref[...] = jnp.zeros_like(st_ref)

    xs = _silu(_causal_conv(extx_ref, xs_ref[...].astype(F32), cwx_ref, cbx_ref, first))
    bcv = _silu(_causal_conv(extb_ref, bc_ref[...].astype(F32), cwb_ref, cbb_ref, first))
    xsc_ref[...] = xs
    bcc_ref[...] = bcv.astype(BF16)

    npair = SSD_HEADS // 2
    lane = _lane((tm, LANES))
    valid = lane < npair
    a_e = -jnp.exp(alog_ref[0:1, :])
    a_o = -jnp.exp(alog_ref[1:2, :])
    dt_e = jnp.where(valid, _softplus(dte_ref[...].astype(F32) + dtb_ref[0:1, :]), 0.0)
    dt_o = jnp.where(valid, _softplus(dto_ref[...].astype(F32) + dtb_ref[1:2, :]), 0.0)
    tri = tri_ref[...]
    cum_e = _dot_exact_lhs(tri, dt_e * a_e)
    cum_o = _dot_exact_lhs(tri, dt_o * a_o)
    cume_ref[...] = cum_e
    cumo_ref[...] = cum_o
    ex_e = expand_ref[0]
    ex_o = expand_ref[1]
    cumx_ref[...] = _dot_exact_rhs(cum_e, ex_e) + _dot_exact_rhs(cum_o, ex_o)
    xdt_ref[...] = xs * (_dot_exact_rhs(dt_e, ex_e) + _dot_exact_rhs(dt_o, ex_o))

    rr = _row((CHUNK, LANES))
    ll = _lane((CHUNK, LANES))
    causal2 = rr >= (ll & (CHUNK - 1))
    left = ll < CHUNK
    gw = SSD_GINNER
    for c in range(tm // CHUNK):
        rows = pl.ds(c * CHUNK, CHUNK)
        cumx = cumx_ref[rows, :]
        lastx = cumx[CHUNK - 1:CHUNK, :]
        xdt = xdt_ref[rows, :]
        xw = (xdt * jnp.exp(lastx - cumx)).astype(BF16)
        ecum = jnp.exp(cumx)
        pt = jnp.concatenate([cume_ref[rows, :], cumo_ref[rows, :]], axis=0).T
        ys = []
        for g in range(SSD_GROUPS):
            bg = bcc_ref[rows, pl.ds(g * SSD_STATE, SSD_STATE)]
            cg = bcc_ref[rows, pl.ds((SSD_GROUPS + g) * SSD_STATE, SSD_STATE)]
            st = st_ref[g]
            y_off = _dot(cg, st.astype(BF16)) * ecum[:, g * gw:(g + 1) * gw]
            cbcb = _dot_nt(cg, jnp.concatenate([bg, bg], axis=0))
            parts = []
            for j in range(npair // SSD_GROUPS):
                jp = g * (npair // SSD_GROUPS) + j
                colp = cumx[:, jp * LANES:(jp + 1) * LANES]
                seg = colp - pt[jp:jp + 1, :]
                dec = jnp.exp(jnp.where(causal2, seg, -jnp.inf))
                w = (cbcb * dec).astype(BF16)
                xp = xdt[:, jp * LANES:(jp + 1) * LANES]
                x2 = jnp.concatenate([jnp.where(left, xp, 0.0), jnp.where(left, 0.0, xp)],
                                     axis=0).astype(BF16)
                parts.append(_dot(w, x2))
            ys.append(jnp.concatenate(parts, axis=1) + y_off)
            st_ref[g] = (st * jnp.exp(lastx[:, g * gw:(g + 1) * gw])
                         + _dot_tn(bg, xw[:, g * gw:(g + 1) * gw]))
        y = jnp.concatenate(ys, axis=1)
        y = y + dexp_ref[...] * xsc_ref[rows, :]
        y = y * _silu(z_ref[rows, :].astype(F32))
        for g in range(SSD_GROUPS):
            yg = y[:, g * gw:(g + 1) * gw]
            ms = jnp.mean(yg * yg, axis=-1, keepdims=True)
            o_ref[rows, pl.ds(g * gw, gw)] = (
                yg * lax.rsqrt(ms + EPS) * ng_ref[:, g * gw:(g + 1) * gw]).astype(o_ref.dtype)


def ssd(u, cb, cwx, cbx, cwb, cbb, dtb, alog, dexp, ng, expand, tri, batch, seq, tm):
    t = u.shape[0]
    tpb = seq // tm
    row = lambda b, i: b * tpb + i
    bcw = 2 * SSD_GROUPS * SSD_STATE
    const = lambda shape: pl.BlockSpec(shape, lambda b, i: tuple(0 for _ in shape))
    return pl.pallas_call(
        _ssd_kernel,
        grid=(batch, tpb),
        in_specs=[pl.BlockSpec((tm, SSD_INNER), lambda b, i: (row(b, i), cb["xs"])),
                  pl.BlockSpec((tm, SSD_INNER), lambda b, i: (row(b, i), cb["z"])),
                  pl.BlockSpec((tm, bcw), lambda b, i: (row(b, i), cb["bc"])),
                  pl.BlockSpec((tm, LANES), lambda b, i: (row(b, i), cb["dte"])),
                  pl.BlockSpec((tm, LANES), lambda b, i: (row(b, i), cb["dto"])),
                  const((CONV_K, SSD_INNER)), const((1, SSD_INNER)),
                  const((CONV_K, bcw)), const((1, bcw)),
                  const((2, LANES)), const((2, LANES)),
                  const((1, SSD_INNER)), const((1, SSD_INNER)),
                  const((2, LANES, SSD_INNER)), const((tm, tm))],
        out_specs=pl.BlockSpec((tm, SSD_INNER), lambda b, i: (row(b, i), 0)),
        out_shape=jax.ShapeDtypeStruct((t, SSD_INNER), BF16),
        scratch_shapes=[pltpu.VMEM((tm + 8, SSD_INNER), F32),
                        pltpu.VMEM((tm + 8, bcw), F32),
                        pltpu.VMEM((SSD_GROUPS, SSD_STATE, SSD_GINNER), F32),
                        pltpu.VMEM((tm, LANES), F32),
                        pltpu.VMEM((tm, LANES), F32),
                        pltpu.VMEM((tm, SSD_INNER), F32),
                        pltpu.VMEM((tm, SSD_INNER), F32),
                        pltpu.VMEM((tm, bcw), BF16),
                        pltpu.VMEM((tm, SSD_INNER), F32)],
        compiler_params=_params("arbitrary", "arbitrary"),
        name="ssd",
    )(u, u, u, u, u, cwx, cbx, cwb, cbb, dtb, alog, dexp, ng, expand, tri)


def _gelu_tanh(x):
    return 0.5 * x * (1.0 + jnp.tanh(math.sqrt(2.0 / math.pi) * (x + 0.044715 * (x * x * x))))


def _lru_kernel(gate_ref, xr_ref, cw_ref, cb_ref, wa_ref, ba_ref, wx_ref, bx_ref, lam_ref,
                o_ref, ext_ref, carry_ref):
    tm, width = xr_ref.shape
    first = pl.program_id(1) == 0

    @pl.when(first)
    def _():
        carry_ref[...] = jnp.zeros_like(carry_ref)

    x = _causal_conv(ext_ref, xr_ref[...].astype(F32), cw_ref, cb_ref, first)
    xb = x.astype(BF16)
    nblk = wa_ref.shape[0]
    bw = width // nblk
    ra = jnp.concatenate([_dot(xb[:, n * bw:(n + 1) * bw], wa_ref[n]) for n in range(nblk)], axis=1)
    rx = jnp.concatenate([_dot(xb[:, n * bw:(n + 1) * bw], wx_ref[n]) for n in range(nblk)], axis=1)
    r = jax.nn.sigmoid(ra + ba_ref[...])
    ig = jax.nn.sigmoid(rx + bx_ref[...])
    log_a = (-LRU_C) * r * _softplus(-lam_ref[...])
    a = jnp.exp(log_a)
    u = jnp.sqrt(1.0 - jnp.exp(2.0 * log_a)) * (ig * x)

    ngrp = tm // SUBLANES
    a3 = a.reshape(ngrp, SUBLANES, width)
    u3 = u.reshape(ngrp, SUBLANES, width)
    sub = lax.broadcasted_iota(jnp.int32, a3.shape, 1)
    d = 1
    while d < SUBLANES:
        m = sub >= d
        a_s = pltpu.roll(a3, d, 1)
        u_s = pltpu.roll(u3, d, 1)
        u3 = jnp.where(m, a3 * u_s + u3, u3)
        a3 = jnp.where(m, a3 * a_s, a3)
        d *= 2
    carry = carry_ref[0:1, :]
    groups = []
    for j in range(ngrp):
        hj = a3[j] * carry + u3[j]
        groups.append(hj)
        carry = hj[SUBLANES - 1:SUBLANES, :]
    hseq = jnp.concatenate(groups, axis=0)
    carry_ref[...] = jnp.broadcast_to(carry, carry_ref.shape)
    o_ref[...] = (hseq * _gelu_tanh(gate_ref[...].astype(F32))).astype(o_ref.dtype)


def lru(u, cb, cw, cbias, wa, ba, wx, bx, lam, batch, seq, tm):
    t = u.shape[0]
    width = cw.shape[1]
    tpb = seq // tm
    row = lambda b, i: b * tpb + i
    const = lambda shape: pl.BlockSpec(shape, lambda b, i: tuple(0 for _ in shape))
    return pl.pallas_call(
        _lru_kernel,
        grid=(batch, tpb),
        in_specs=[pl.BlockSpec((tm, width), lambda b, i: (row(b, i), cb["gate"])),
                  pl.BlockSpec((tm, width), lambda b, i: (row(b, i), cb["xr"])),
                  const((CONV_K, width)), const((1, width)),
                  const(wa.shape), const((1, width)),
                  const(wx.shape), const((1, width)), const((1, width))],
        out_specs=pl.BlockSpec((tm, width), lambda b, i: (row(b, i), 0)),
        out_shape=jax.ShapeDtypeStruct((t, width), BF16),
        scratch_shapes=[pltpu.VMEM((tm + 8, width), F32),
                        pltpu.VMEM((8, width), F32)],
        compiler_params=_params("arbitrary", "arbitrary"),
        name="lru",
    )(u, u, cw, cbias, wa, ba, wx, bx, lam)


def _mla_prep_kernel(uq_ref, ukv_ref, krr_ref, pos_ref, freq_ref, qn_ref, kvn_ref,
                     wq_ref, wkv_ref, qg_ref, kg_ref, q_ref, k_ref, v_ref):
    tm = uq_ref.shape[0]
    lane = _lane((tm, LANES))
    lo_half = lane < MLA_ROPE

    ang = pos_ref[...].astype(F32) * freq_ref[...]
    cs = jnp.where(lo_half, jnp.cos(ang),
                   jnp.where(lane < MLA_ROPE + MLA_ROPE // 2, -jnp.sin(ang), jnp.sin(ang)))

    def latent_norm(ref, g_ref):
        x = ref[...].astype(F32)
        ms = jnp.mean(x * x, axis=-1, keepdims=True)
        return (x * lax.rsqrt(ms + EPS) * g_ref[...]).astype(BF16)

    qf = _dot(latent_norm(uq_ref, qn_ref), wq_ref[...])
    kvf = _dot(latent_norm(ukv_ref, kvn_ref), wkv_ref[...])
    krr = krr_ref[...].astype(F32)
    kr_sq = jnp.where(lo_half, krr * krr, 0.0)
    scale = MLA_QK ** -0.5 * math.log2(math.e)

    def rope_half(y2):
        t = y2 * cs
        return jnp.where(lo_half, t + pltpu.roll(t, MLA_ROPE, 1), 0.0)

    kr_rope = rope_half(krr * kg_ref[:, LANES:2 * LANES])

    for h in range(MLA_HEADS):
        x1 = qf[:, h * MLA_QK_PAD:h * MLA_QK_PAD + LANES]
        x2 = qf[:, h * MLA_QK_PAD + LANES:(h + 1) * MLA_QK_PAD]
        ss = jnp.sum(x1 * x1 + jnp.where(lo_half, x2 * x2, 0.0), axis=-1, keepdims=True)
        r = lax.rsqrt(ss * (1.0 / MLA_QK) + EPS) * scale
        q_ref[:, pl.ds(h * MLA_QK_PAD, LANES)] = (x1 * r * qg_ref[:, 0:LANES]).astype(q_ref.dtype)
        q_ref[:, pl.ds(h * MLA_QK_PAD + LANES, LANES)] = rope_half(
            x2 * r * qg_ref[:, LANES:2 * LANES]).astype(q_ref.dtype)

        kn = kvf[:, h * LANES:(h + 1) * LANES]
        ssk = jnp.sum(kn * kn + kr_sq, axis=-1, keepdims=True)
        rk = lax.rsqrt(ssk * (1.0 / MLA_QK) + EPS)
        k_ref[:, pl.ds(h * MLA_QK_PAD, LANES)] = (kn * rk * kg_ref[:, 0:LANES]).astype(k_ref.dtype)
        k_ref[:, pl.ds(h * MLA_QK_PAD + LANES, LANES)] = (kr_rope * rk).astype(k_ref.dtype)
        v_ref[:, pl.ds(h * MLA_V, MLA_V)] = kvf[:, (MLA_HEADS + h) * LANES:
                                                (MLA_HEADS + h + 1) * LANES].astype(v_ref.dtype)


def mla_prep(u, cb, pos2d, freq, qn, kvn, wq, wkv, qg, kg, tm):
    t = u.shape[0]
    const = lambda shape: pl.BlockSpec(shape, lambda i: tuple(0 for _ in shape))
    hq = MLA_HEADS * MLA_QK_PAD
    hv = MLA_HEADS * MLA_V
    return pl.pallas_call(
        _mla_prep_kernel,
        grid=(t // tm,),
        in_specs=[pl.BlockSpec((tm, MLA_Q_LORA), lambda i: (i, cb["uq"])),
                  pl.BlockSpec((tm, MLA_KV_LORA), lambda i: (i, cb["ukv"])),
                  pl.BlockSpec((tm, LANES), lambda i: (i, cb["krr"])),
                  pl.BlockSpec((tm, 1), lambda i: (i, 0)),
                  const((1, LANES)), const((1, MLA_Q_LORA)), const((1, MLA_KV_LORA)),
                  const(wq.shape), const(wkv.shape), const((1, 2 * LANES)), const((1, 2 * LANES))],
        out_specs=[pl.BlockSpec((tm, hq), lambda i: (i, 0)),
                   pl.BlockSpec((tm, hq), lambda i: (i, 0)),
                   pl.BlockSpec((tm, hv), lambda i: (i, 0))],
        out_shape=[jax.ShapeDtypeStruct((t, hq), BF16),
                   jax.ShapeDtypeStruct((t, hq), BF16),
                   jax.ShapeDtypeStruct((t, hv), BF16)],
        compiler_params=_params("arbitrary"),
        name="mla_prep",
    )(u, u, u, pos2d, freq, qn, kvn, wq, wkv, qg, kg)


def _attn_kernel(q_ref, k_ref, v_ref, o_ref, m_ref, acc_ref):
    i = pl.program_id(2)
    tq = q_ref.shape[0]
    nh = q_ref.shape[1] // MLA_QK_PAD
    nc = tq // LANES

    m_ref[...] = jnp.full_like(m_ref, -jnp.inf)
    acc_ref[...] = jnp.zeros_like(acc_ref)
    ones = jnp.ones((tq, LANES), BF16)

    def update(g, s, v):
        m_prev = m_ref[g]
        m_cur = s[:, 0:LANES]
        for c in range(1, nc):
            m_cur = jnp.maximum(m_cur, s[:, c * LANES:(c + 1) * LANES])
        m_new = jnp.maximum(m_prev, jnp.max(m_cur, axis=-1, keepdims=True))
        alpha = jnp.exp2(m_prev - m_new)
        p = jnp.concatenate(
            [jnp.exp2((s[:, c * LANES:(c + 1) * LANES] - m_new).astype(BF16)) for c in range(nc)],
            axis=1)
        pv = _dot(p, jnp.concatenate([v, ones], axis=1))
        acc_ref[g] = jnp.concatenate([alpha, alpha], axis=1) * acc_ref[g] + pv
        m_ref[g] = m_new

    def block(j, masked):
        rows = pl.ds(pl.multiple_of(j * tq, tq), tq)
        for g in range(nh):
            s = _dot_nt(q_ref[:, g * MLA_QK_PAD:(g + 1) * MLA_QK_PAD],
                        k_ref[rows, pl.ds(g * MLA_QK_PAD, MLA_QK_PAD)])
            if masked:
                s = jnp.where(_row((tq, tq)) >= _lane((tq, tq)), s, -jnp.inf)
            update(g, s, v_ref[rows, pl.ds(g * MLA_V, MLA_V)])

    def body(jj, carry):
        for u in range(ATTN_KV_UNROLL):
            block(ATTN_KV_UNROLL * jj + u, False)
        return carry

    shift = ATTN_KV_UNROLL.bit_length() - 1
    ntrip = i >> shift
    lax.fori_loop(0, ntrip, body, 0)

    def rest(j, carry):
        block(j, False)
        return carry

    lax.fori_loop(ntrip << shift, i, rest, 0)

    block(i, True)
    for g in range(nh):
        acc = acc_ref[g]
        o_ref[:, pl.ds(g * MLA_V, MLA_V)] = (acc[:, :MLA_V] / acc[:, MLA_V:]).astype(o_ref.dtype)


def attention(q, k, v, batch, seq, tq):
    t = q.shape[0]
    nq = seq // tq
    nh = ATTN_HEADS_PER_STEP
    return pl.pallas_call(
        _attn_kernel,
        grid=(batch, MLA_HEADS // nh, nq),
        in_specs=[pl.BlockSpec((tq, nh * MLA_QK_PAD), lambda b, h, i: (b * nq + i, h)),
                  pl.BlockSpec((seq, nh * MLA_QK_PAD), lambda b, h, i: (b, h),
                               pipeline_mode=pl.Buffered(1)),
                  pl.BlockSpec((seq, nh * MLA_V), lambda b, h, i: (b, h),
                               pipeline_mode=pl.Buffered(1))],
        out_specs=pl.BlockSpec((tq, nh * MLA_V), lambda b, h, i: (b * nq + i, h)),
        out_shape=jax.ShapeDtypeStruct((t, MLA_HEADS * MLA_V), BF16),
        scratch_shapes=[pltpu.VMEM((nh, tq, LANES), F32),
                        pltpu.VMEM((nh, tq, MLA_V + LANES), F32)],
        compiler_params=_params("arbitrary", "arbitrary", "arbitrary"),
        name="attention",
    )(q, k, v)


def _outproj_kernel(oa_ref, ob_ref, w1_ref, w2_ref, h_ref, g1_ref, nf_ref, sh_ref, sc_ref,
                    rwt_ref, hn_ref, hf_ref, lg_ref):
    y = _dot(oa_ref[...], w1_ref[...]) + _dot(ob_ref[...], w2_ref[...])
    hn = h_ref[...] + g1_ref[0] * y
    hn_ref[...] = hn
    ms = jnp.mean(hn * hn, axis=-1, keepdims=True)
    hf = hn * lax.rsqrt(ms + EPS) * nf_ref[...] * (1.0 + sc_ref[0]) + sh_ref[0]
    hf_ref[...] = hf.astype(BF16)
    r_hi, r_mid, r_lo = _split3(rwt_ref[...])
    f_hi, f_mid, _ = _split3(hf)
    ne = r_hi.shape[0]
    a = _dot_nt(jnp.concatenate([r_hi, r_mid, r_lo], axis=0), f_hi)
    b = _dot_nt(jnp.concatenate([r_hi, r_mid], axis=0), f_mid)
    lg_ref[...] = a[0:ne] + b[0:ne] + a[ne:2 * ne] + a[2 * ne:3 * ne] + b[ne:2 * ne]


def outproj(oa, ob, w1, w2, h2d, g1, nf, sh2, sc2, rwt, seq, tm):
    t, d = h2d.shape
    tpb = seq // tm
    const = lambda shape: pl.BlockSpec(shape, lambda i: tuple(0 for _ in shape))
    bvec = pl.BlockSpec((1, 1, d), lambda i: (i // tpb, 0, 0))
    return pl.pallas_call(
        _outproj_kernel,
        grid=(t // tm,),
        in_specs=[pl.BlockSpec((tm, oa.shape[1]), lambda i: (i, 0)),
                  pl.BlockSpec((tm, ob.shape[1]), lambda i: (i, 0)),
                  const(w1.shape), const(w2.shape),
                  pl.BlockSpec((tm, d), lambda i: (i, 0)),
                  bvec, const((1, d)), bvec, bvec, const(rwt.shape)],
        out_specs=[pl.BlockSpec((tm, d), lambda i: (i, 0)),
                   pl.BlockSpec((tm, d), lambda i: (i, 0)),
                   pl.BlockSpec((N_EXPERTS, tm), lambda i: (0, i))],
        out_shape=[jax.ShapeDtypeStruct((t, d), F32),
                   jax.ShapeDtypeStruct((t, d), BF16),
                   jax.ShapeDtypeStruct((N_EXPERTS, t), F32)],
        compiler_params=_params("arbitrary"),
        name="outproj",
    )(oa, ob, w1, w2, h2d, g1[:, None, :], nf.reshape(1, d), sh2[:, None, :], sc2[:, None, :], rwt)


def _route_sort_kernel(*refs):
    last = pl.program_id(0) == pl.num_programs(0) - 1

    @pl.when(last)
    def _():
        xs_ref = refs[5]
        xs_ref[...] = jnp.zeros_like(xs_ref)

    @pl.when(jnp.logical_not(last))
    def _():
        _route_sort_body(*refs)


def _route_sort_body(lg_ref, bias_ref, hf_ref, utri_ref, ltri_ref,
                     xs_ref, lpos_ref, wts_ref, cnt_ref):
    scores = jax.nn.sigmoid(lg_ref[...])
    sel = scores + bias_ref[...]
    tm = sel.shape[1]
    eidx = _row((N_EXPERTS, tm))
    neg = -jnp.inf

    best_score = None
    best_grp = None
    for g in range(N_GROUPS):
        m = [sel[g * EXPERTS_PER_GROUP + i:g * EXPERTS_PER_GROUP + i + 1, :]
             for i in range(EXPERTS_PER_GROUP)]
        gs = None
        for a in range(EXPERTS_PER_GROUP):
            for b in range(a + 1, EXPERTS_PER_GROUP):
                pair = m[a] + m[b]
                gs = pair if gs is None else jnp.maximum(gs, pair)
        if best_score is None:
            best_score, best_grp = gs, jnp.zeros_like(gs, dtype=jnp.int32)
        else:
            better = gs > best_score
            best_score = jnp.where(better, gs, best_score)
            best_grp = jnp.where(better, g, best_grp)

    masked = jnp.where((eidx >> 2) == best_grp, sel, neg)
    m1 = jnp.max(masked, axis=0, keepdims=True)
    i1 = jnp.min(jnp.where(masked == m1, eidx, N_EXPERTS), axis=0, keepdims=True)
    masked2 = jnp.where(eidx == i1, neg, masked)
    m2 = jnp.max(masked2, axis=0, keepdims=True)
    i2 = jnp.min(jnp.where(masked2 == m2, eidx, N_EXPERTS), axis=0, keepdims=True)
    pick1 = eidx == i1
    pick2 = eidx == i2
    s1 = jnp.sum(jnp.where(pick1, scores, 0.0), axis=0, keepdims=True)
    s2 = jnp.sum(jnp.where(pick2, scores, 0.0), axis=0, keepdims=True)
    tot = s1 + s2
    wts_ref[...] = jnp.concatenate([s1 / tot, s2 / tot], axis=0)

    nr = xs_ref.shape[0]
    picks = jnp.where(pick1, 1.0, jnp.where(pick2, 1.0, 0.0))
    csum = _dot(picks.astype(BF16), utri_ref[...])
    cnt = jnp.sum(picks, axis=1, keepdims=True)
    cnt_ref[0] = jnp.broadcast_to(cnt, (N_EXPERTS, LANES))
    cnt_pad = jnp.floor((cnt + (ROW_CHUNK - 1.0)) * (1.0 / ROW_CHUNK)) * ROW_CHUNK
    seg_off = _dot(ltri_ref[...], jnp.broadcast_to(cnt_pad, (N_EXPERTS, LANES)).astype(BF16))[:, 0:1]
    lposmat = seg_off + csum - 1.0
    lp1 = jnp.sum(jnp.where(pick1, lposmat, 0.0), axis=0, keepdims=True).astype(jnp.int32)
    lp2 = jnp.sum(jnp.where(pick2, lposmat, 0.0), axis=0, keepdims=True).astype(jnp.int32)
    lpos_ref[...] = jnp.concatenate([lp1, lp2], axis=0)
    rowi = _row((nr, tm))
    onehot = jnp.where(rowi == lp1, 1.0, jnp.where(rowi == lp2, 1.0, 0.0)).astype(BF16)
    xs_ref[...] = _dot(onehot, hf_ref[...]).astype(xs_ref.dtype)


def route_sort(logits_t, router_bias, hf, tl, nr):
    e, t = logits_t.shape
    d = hf.shape[1]
    ntile = t // tl
    r = jnp.arange(tl)
    utri = (r[:, None] <= r[None, :]).astype(BF16)
    re = jnp.arange(e)
    ltri = (re[None, :] < re[:, None]).astype(BF16)
    const = lambda shape: pl.BlockSpec(shape, lambda i: tuple(0 for _ in shape))
    tile = lambda i: jnp.minimum(i, ntile - 1)
    return pl.pallas_call(
        _route_sort_kernel,
        grid=(ntile + 1,),
        in_specs=[pl.BlockSpec((e, tl), lambda i: (0, tile(i))),
                  const((e, 1)),
                  pl.BlockSpec((tl, d), lambda i: (tile(i), 0)),
                  const((tl, tl)), const((e, e))],
        out_specs=[pl.BlockSpec((nr, d), lambda i: (i, 0)),
                   pl.BlockSpec((2, tl), lambda i: (0, tile(i))),
                   pl.BlockSpec((2, tl), lambda i: (0, tile(i))),
                   pl.BlockSpec((1, e, LANES), lambda i: (tile(i), 0, 0))],
        out_shape=[jax.ShapeDtypeStruct(((ntile + 1) * nr, d), BF16),
                   jax.ShapeDtypeStruct((2, t), jnp.int32),
                   jax.ShapeDtypeStruct((2, t), F32),
                   jax.ShapeDtypeStruct((t // tl, e, LANES), F32)],
        compiler_params=_params("arbitrary"),
        name="route_sort",
    )(logits_t, router_bias.reshape(e, 1).astype(F32), hf, utri, ltri)


def _moe_plan(cnt, nr, tmg, nt_max):
    ntile, ne = cnt.shape
    cpt = tmg // ROW_CHUNK
    nch = (cnt + ROW_CHUNK - 1) // ROW_CHUNK
    seg_off = jnp.cumsum(nch, axis=1) - nch
    cum = jnp.cumsum(nch, axis=0)
    tot = cum[-1]
    padded = (tot + cpt - 1) // cpt * cpt
    gend = jnp.cumsum(padded)
    gstart = gend - padded
    start = (gstart[None, :] + cum - nch).T.reshape(-1)
    end = start + nch.T.reshape(-1)
    base = (jnp.arange(ntile, dtype=jnp.int32)[:, None] * (nr // ROW_CHUNK) + seg_off).T.reshape(-1)
    c = jnp.arange(nt_max * cpt, dtype=jnp.int32)[:, None]
    hit = (c >= start[None, :]) & (c < end[None, :])
    valid = jnp.any(hit, axis=1)
    src = jnp.sum(jnp.where(hit, base[None, :] - start[None, :] + c, 0), axis=1) * ROW_CHUNK
    src = jnp.where(valid, src, nr - ROW_CHUNK)
    cflat = c[:, 0]
    spare = ntile * nr + ((cflat // cpt) % 2) * tmg + (cflat % cpt) * ROW_CHUNK
    dst = jnp.where(valid, src, spare)
    first = jnp.arange(nt_max, dtype=jnp.int32) * cpt
    e_first = jnp.sum(first[:, None] >= gend[None, :], axis=1).astype(jnp.int32)
    tile_on = (e_first < ne).astype(jnp.int32)
    return src.astype(jnp.int32), dst.astype(jnp.int32), jnp.minimum(e_first, ne - 1), tile_on


def _moe_group_kernel(src_ref, dst_ref, te_ref, on_ref, xs_hbm, wg_ref, wu_ref, wd_ref,
                      y_hbm, xbuf, ybuf, wgb, wub, wdb, in_sem, out_sem):
    del xs_hbm
    g = pl.program_id(0)
    ng = pl.num_programs(0)
    slot = lax.rem(g, 2)
    tmg = xbuf.shape[1]
    cpt = tmg // ROW_CHUNK

    def in_copy(tile, sl, k):
        row = pl.multiple_of(src_ref[tile * cpt + k], ROW_CHUNK)
        return pltpu.make_async_copy(y_hbm.at[pl.ds(row, ROW_CHUNK), :],
                                     xbuf.at[sl, pl.ds(k * ROW_CHUNK, ROW_CHUNK), :], in_sem.at[sl])

    def out_copy(tile, sl, k):
        row = pl.multiple_of(dst_ref[tile * cpt + k], ROW_CHUNK)
        return pltpu.make_async_copy(ybuf.at[sl, pl.ds(k * ROW_CHUNK, ROW_CHUNK), :],
                                     y_hbm.at[pl.ds(row, ROW_CHUNK), :], out_sem.at[sl])

    def start_in(tile, sl):
        for k in range(cpt):
            in_copy(tile, sl, k).start()

    def wait_in(tile, sl):
        for k in range(cpt):
            in_copy(tile, sl, k).wait()

    def start_out(tile, sl):
        for k in range(cpt):
            out_copy(tile, sl, k).start()

    def wait_out(tile, sl):
        for k in range(cpt):
            out_copy(tile, sl, k).wait()

    @pl.when((g >= 2) & (on_ref[jnp.maximum(g - 2, 0)] == 1))
    def _():
        wait_out(g - 2, slot)

    @pl.when((g == 0) & (on_ref[0] == 1))
    def _():
        start_in(0, 0)

    @pl.when((g + 1 < ng) & (on_ref[jnp.minimum(g + 1, ng - 1)] == 1))
    def _():
        start_in(g + 1, 1 - slot)

    @pl.when(on_ref[g] == 1)
    def _():
        @pl.when((g == 0) | (te_ref[g] != te_ref[jnp.maximum(g - 1, 0)]))
        def _():
            wgb[...] = wg_ref[0, 0].astype(BF16)
            wub[...] = wu_ref[0, 0].astype(BF16)
            wdb[...] = wd_ref[0, 0].astype(BF16)

        wait_in(g, slot)
        x = xbuf[slot]
        hid = (_silu(_dot(x, wgb[...])) * _dot(x, wub[...])).astype(BF16)
        ybuf[slot] = _dot(hid, wdb[...]).astype(ybuf.dtype)
        start_out(g, slot)

    @pl.when(g == ng - 1)
    def _():
        @pl.when((ng >= 2) & (on_ref[jnp.maximum(g - 1, 0)] == 1))
        def _():
            wait_out(g - 1, 1 - slot)

        @pl.when(on_ref[g] == 1)
        def _():
            wait_out(g, slot)


def moe_group(xs, wg, wu, wd, layer, plan, tmg, nt_max):
    src, dst, tile_e, tile_on = plan
    rows_out, d = xs.shape
    wspec = lambda w: pl.BlockSpec((1, 1) + w.shape[2:], lambda g, s, v, te, on: (layer, te[g], 0, 0))
    grid_spec = pltpu.PrefetchScalarGridSpec(
        num_scalar_prefetch=4,
        grid=(nt_max,),
        in_specs=[pl.BlockSpec(memory_space=pl.ANY), wspec(wg), wspec(wu), wspec(wd)],
        out_specs=pl.BlockSpec(memory_space=pl.ANY),
        scratch_shapes=[pltpu.VMEM((2, tmg, d), BF16), pltpu.VMEM((2, tmg, d), BF16),
                        pltpu.VMEM(wg.shape[2:], BF16), pltpu.VMEM(wu.shape[2:], BF16),
                        pltpu.VMEM(wd.shape[2:], BF16),
                        pltpu.SemaphoreType.DMA((2,)), pltpu.SemaphoreType.DMA((2,))])
    return pl.pallas_call(
        _moe_group_kernel,
        grid_spec=grid_spec,
        out_shape=jax.ShapeDtypeStruct((rows_out, d), BF16),
        input_output_aliases={4: 0},
        compiler_params=_params("arbitrary"),
        name="moe_group",
    )(src, dst, tile_e, tile_on, xs, wg, wu, wd)


def _moe_combine_kernel(y_ref, lpos_ref, wts_ref, h_ref, g2_ref, o_ref):
    nr = y_ref.shape[0]
    tl = h_ref.shape[0]
    rowi = _row((nr, tl))
    wc = (jnp.where(rowi == lpos_ref[0:1, :], wts_ref[0:1, :], 0.0)
          + jnp.where(rowi == lpos_ref[1:2, :], wts_ref[1:2, :], 0.0)).astype(BF16)
    o_ref[...] = h_ref[...] + g2_ref[0] * _dot_tn(wc, y_ref[...])


def moe_combine(y, lpos, wts, h2d, g2, seq, tl, nr):
    t, d = h2d.shape
    tpb = seq // tl
    return pl.pallas_call(
        _moe_combine_kernel,
        grid=(t // tl,),
        in_specs=[pl.BlockSpec((nr, d), lambda i: (i, 0)),
                  pl.BlockSpec((2, tl), lambda i: (0, i)),
                  pl.BlockSpec((2, tl), lambda i: (0, i)),
                  pl.BlockSpec((tl, d), lambda i: (i, 0)),
                  pl.BlockSpec((1, 1, d), lambda i: (i // tpb, 0, 0))],
        out_specs=pl.BlockSpec((tl, d), lambda i: (i, 0)),
        out_shape=jax.ShapeDtypeStruct((t, d), F32),
        compiler_params=_params("arbitrary"),
        name="moe_combine",
    )(y, lpos, wts, h2d, g2[:, None, :])


def _pad_heads(w, heads, dh):
    lead = w.shape[:-1]
    w = w.reshape(lead + (heads, dh))
    w = jnp.pad(w, [(0, 0)] * len(lead) + [(0, 0), (0, LANES - dh)])
    return w.reshape(lead + (heads * LANES,))


def _pad_cols(w, n):
    return jnp.pad(w, [(0, 0)] * (w.ndim - 1) + [(0, n - w.shape[-1])])


def _tri_blocks(tm):
    r = jnp.arange(tm)
    return ((r[:, None] >= r[None, :]) & (r[:, None] // CHUNK == r[None, :] // CHUNK)).astype(BF16)


def _even_layout(w_in):
    sizes = (GLA_HEADS * GLA_DK, GLA_HEADS * GLA_DK, GLA_HEADS * GLA_DV, GLA_GATE_RANK,
             GLA_HEADS * GLA_DV, SSD_INNER, SSD_INNER + 2 * SSD_GROUPS * SSD_STATE, SSD_HEADS)
    offs = [0]
    for s in sizes:
        offs.append(offs[-1] + s)
    seg = lambda i: w_in[:, offs[i]:offs[i + 1]]
    q, k, v, glr, og, z, xbc, dt = (seg(i) for i in range(8))
    xs, bc = xbc[:, :SSD_INNER], xbc[:, SSD_INNER:]
    cols = [xs, z, v, og, _pad_heads(q, GLA_HEADS, GLA_DK), _pad_heads(k, GLA_HEADS, GLA_DK), bc,
            _pad_cols(glr, LANES), _pad_cols(dt[:, 0::2], LANES), _pad_cols(dt[:, 1::2], LANES)]
    w = jnp.concatenate(cols, axis=1).astype(BF16)
    cb = {"xs": 0, "z": 1, "v": 4, "og": 5, "q": 6, "k": 7, "bc": 8, "glr": 36, "dte": 37, "dto": 38}
    return w, cb


def _odd_layout(w_in):
    d = w_in.shape[0]
    o = [0, d, 2 * d, 2 * d + MLA_Q_LORA, 2 * d + MLA_Q_LORA + MLA_KV_LORA,
         2 * d + MLA_Q_LORA + MLA_KV_LORA + MLA_ROPE]
    gate, xr, uq, ukv, kr = (w_in[:, o[i]:o[i + 1]] for i in range(5))
    half = MLA_ROPE // 2
    kr_sw = jnp.concatenate([kr[:, half:], kr[:, :half]], axis=1)
    w = jnp.concatenate([gate, xr, ukv, uq, kr, kr_sw], axis=1).astype(BF16)
    cb = {"gate": 0, "xr": 1, "ukv": 8, "uq": 6, "krr": 21}
    return w, cb


def _swap_halves(x):
    half = x.shape[-1] // 2
    return jnp.concatenate([x[..., half:], x[..., :half]], axis=-1)


def _block_diag(w, per):
    nb, bw, _ = w.shape
    w = w.reshape(nb // per, per, bw, bw)
    eye = jnp.eye(per, dtype=w.dtype)
    out = jnp.einsum("gpij,pq->gpiqj", w, eye)
    return out.reshape(nb // per, per * bw, per * bw)


def kernel(x, c, positions, router_w, router_bias, ada_w, ada_b, norm_mix, norm_ffn, moe_w_gate, moe_w_up, moe_w_down, ev_w_in, gla_w_g2, gla_b_g2, gla_onorm, ssd_conv_w, ssd_conv_b, ssd_dt_bias, ssd_a_log, ssd_d, ssd_norm, ev_w_out, od_w_in, lru_conv_w, lru_conv_b, lru_w_a, lru_b_a, lru_w_x, lru_b_x, lru_lambda, mla_q_norm, mla_w_q_up, mla_kv_norm, mla_w_kv_up, mla_q_qknorm, mla_k_qknorm, od_w_out):
    batch, seq, d = x.shape
    t = batch * seq
    depth = ada_w.shape[0]
    tm_seq = min(TM_SEQ, seq)
    tm_mm = min(TM_MM, seq)
    tq = min(TQ_ATTN, seq)
    tl = min(TL_MOE, seq)
    nr = 2 * tl + N_EXPERTS * ROW_CHUNK
    tmg = TM_MOE
    chunks_max = 2 * t // ROW_CHUNK + (t // tl) * N_EXPERTS + N_EXPERTS * (tmg // ROW_CHUNK - 1)
    nt_max = -(-chunks_max // (tmg // ROW_CHUNK))

    mod = ada_mod(c, ada_w, ada_b)
    rwt = router_w.T.astype(F32)
    tri = _tri_blocks(tm_seq)
    h = x.reshape(t, d)

    for layer in range(depth):
        sh1, sc1, g1, sh2, sc2, g2 = (mod[layer, :, i * d:(i + 1) * d] for i in range(6))
        i = layer // 2
        if layer % 2 == 0:
            w_in, cb = _even_layout(ev_w_in[i])
            u = inproj(h, norm_mix[layer], sh1, sc1, w_in, seq, tm_mm, w_in.shape[1] // 3)
            wg = jnp.pad(_pad_heads(gla_w_g2[i], GLA_HEADS, GLA_DK),
                         ((0, LANES - GLA_GATE_RANK), (0, 0))).astype(BF16)
            bg = _pad_heads(gla_b_g2[i][None, :], GLA_HEADS, GLA_DK)
            oa = gla(u, cb, wg, bg, gla_onorm[i][None, :], tri, batch, seq, tm_seq)
            cw, cbias = ssd_conv_w[i], ssd_conv_b[i][None, :]
            perm = lambda v: jnp.stack([_pad_cols(v[0::2], LANES), _pad_cols(v[1::2], LANES)])
            npair = SSD_HEADS // 2
            hp = jnp.arange(SSD_INNER) // SSD_HEADDIM
            ex_e = (jnp.arange(LANES)[:, None] * 2 == hp[None, :]) & (jnp.arange(LANES)[:, None] < npair)
            ex_o = (jnp.arange(LANES)[:, None] * 2 + 1 == hp[None, :]) & (jnp.arange(LANES)[:, None] < npair)
            expand = jnp.stack([ex_e, ex_o]).astype(BF16)
            ob = ssd(u, cb, cw[:, :SSD_INNER], cbias[:, :SSD_INNER], cw[:, SSD_INNER:],
                     cbias[:, SSD_INNER:], perm(ssd_dt_bias[i]), perm(ssd_a_log[i]),
                     jnp.repeat(ssd_d[i], SSD_HEADDIM)[None, :], ssd_norm[i][None, :],
                     expand, tri, batch, seq, tm_seq)
            w_out = ev_w_out[i].astype(BF16)
            w1, w2 = w_out[:GLA_HEADS * GLA_DV], w_out[GLA_HEADS * GLA_DV:]
        else:
            w_in, cb = _odd_layout(od_w_in[i])
            u = inproj(h, norm_mix[layer], sh1, sc1, w_in, seq, tm_mm, w_in.shape[1] // 2)
            per = 4
            oa = lru(u, cb, lru_conv_w[i], lru_conv_b[i][None, :],
                     _block_diag(lru_w_a[i], per).astype(BF16), lru_b_a[i][None, :],
                     _block_diag(lru_w_x[i], per).astype(BF16), lru_b_x[i][None, :],
                     lru_lambda[i][None, :], batch, seq, tm_seq)
            wq = mla_w_q_up[i].reshape(MLA_Q_LORA, MLA_HEADS, MLA_QK)
            wq = jnp.concatenate([wq, _swap_halves(wq[..., MLA_NOPE:])], axis=-1)
            wq = wq.reshape(MLA_Q_LORA, MLA_HEADS * MLA_QK_PAD).astype(BF16)
            wkv = mla_w_kv_up[i].reshape(MLA_KV_LORA, MLA_HEADS, MLA_NOPE + MLA_V)
            wkv = jnp.concatenate([wkv[..., :MLA_NOPE].reshape(MLA_KV_LORA, -1),
                                   wkv[..., MLA_NOPE:].reshape(MLA_KV_LORA, -1)], axis=1).astype(BF16)
            ext_gain = lambda gq: jnp.concatenate([gq, _swap_halves(gq[MLA_NOPE:])])[None, :]
            fr = ROPE_THETA ** (-jnp.arange(0, MLA_ROPE, 2, dtype=F32) / MLA_ROPE)
            freq = jnp.tile(fr, 4)[None, :]
            q, k, v = mla_prep(u, cb, positions.reshape(t, 1), freq, mla_q_norm[i][None, :],
                               mla_kv_norm[i][None, :], wq, wkv, ext_gain(mla_q_qknorm[i]),
                               ext_gain(mla_k_qknorm[i]), tm_mm)
            ob = attention(q, k, v, batch, seq, tq)
            w_out = od_w_out[i].astype(BF16)
            w1, w2 = w_out[:d], w_out[d:]
        h, hf, logits_t = outproj(oa, ob, w1, w2, h, g1, norm_ffn[layer], sh2, sc2, rwt, seq, tm_mm)
        assert 2 * tmg <= nr
        xs, lpos, wts, cnt = route_sort(logits_t, router_bias, hf, tl, nr)
        plan = _moe_plan(cnt[:, :, 0].astype(jnp.int32), nr, tmg, nt_max)
        y = moe_group(xs, moe_w_gate, moe_w_up, moe_w_down, layer, plan, tmg, nt_max)
        h = moe_combine(y, lpos, wts, h, g2, seq, tl, nr)
    return h.reshape(batch, seq, d)
```

```python
import functools
import math

import jax
import jax.numpy as jnp
from jax import lax
from jax.experimental import pallas as pl
from jax.experimental.pallas import tpu as pltpu

F32 = jnp.float32
BF16 = jnp.bfloat16

EPS = 1e-6
CHUNK = 64
CONV_K = 4
GLA_HEADS, GLA_DK, GLA_DV = 4, 64, 128
GLA_GATE_RANK, GLA_GATE_NORM = 16, 16.0
SSD_HEADS, SSD_HEADDIM, SSD_GROUPS, SSD_STATE = 16, 64, 2, 128
SSD_INNER = SSD_HEADS * SSD_HEADDIM
SSD_GINNER = SSD_INNER // SSD_GROUPS
LRU_BLOCKS, LRU_C = 16, 8.0
MLA_HEADS, MLA_NOPE, MLA_ROPE, MLA_V = 8, 128, 64, 128
MLA_QK = MLA_NOPE + MLA_ROPE
MLA_QK_PAD = 256
MLA_Q_LORA, MLA_KV_LORA = 384, 256
ROPE_THETA = 10000.0
N_EXPERTS, N_GROUPS, D_EXPERT = 16, 4, 512
EXPERTS_PER_GROUP = N_EXPERTS // N_GROUPS

TM_SEQ = 256
TM_MM = 512
TM_MOE = 512
TL_MOE = 512
ROW_CHUNK = 16
TQ_ATTN = 512
ATTN_HEADS_PER_STEP = 4
ATTN_KV_UNROLL = 4

LANES = 128
SUBLANES = 8
VMEM_LIMIT = 48 * 1024 * 1024

NT_DIMS = (((1,), (1,)), ((), ()))
TN_DIMS = (((0,), (0,)), ((), ()))


def _params(*sem):
    return pltpu.CompilerParams(dimension_semantics=sem, vmem_limit_bytes=VMEM_LIMIT)


def _dot(a, b):
    return jnp.dot(a, b, preferred_element_type=F32)


def _dot_nt(a, b):
    return lax.dot_general(a, b, NT_DIMS, preferred_element_type=F32)


def _dot_tn(a, b):
    return lax.dot_general(a, b, TN_DIMS, preferred_element_type=F32)


def _split3(a):
    hi = a.astype(BF16)
    r1 = a - hi.astype(F32)
    mid = r1.astype(BF16)
    lo = (r1 - mid.astype(F32)).astype(BF16)
    return hi, mid, lo


def _dot_exact_rhs(a, b_bf16):
    hi, mid, lo = _split3(a)
    return _dot(hi, b_bf16) + _dot(mid, b_bf16) + _dot(lo, b_bf16)


def _dot_exact_lhs(a_bf16, b):
    hi, mid, lo = _split3(b)
    return _dot(a_bf16, hi) + _dot(a_bf16, mid) + _dot(a_bf16, lo)


def _softplus(x):
    return jnp.maximum(x, 0.0) + jnp.log1p(jnp.exp(-jnp.abs(x)))


def _silu(x):
    return x * jax.nn.sigmoid(x)


def _lane(shape):
    return lax.broadcasted_iota(jnp.int32, shape, len(shape) - 1)


def _row(shape):
    return lax.broadcasted_iota(jnp.int32, shape, len(shape) - 2)


def _ada_kernel(c_ref, w_ref, b_ref, o_ref):
    c = c_ref[...]
    a_hi, a_mid, a_lo = _split3(_silu(c))
    w_hi, w_mid, w_lo = _split3(w_ref[0])
    acc = (_dot(a_hi, w_hi) + _dot(a_hi, w_mid) + _dot(a_mid, w_hi)
           + _dot(a_hi, w_lo) + _dot(a_lo, w_hi) + _dot(a_mid, w_mid))
    o_ref[0] = acc + b_ref[0]


def ada_mod(c, ada_w, ada_b):
    depth, d, n = ada_w.shape
    b = c.shape[0]
    bp = 8
    cp = jnp.zeros((bp, d), F32).at[:b].set(c)
    tn = 1536
    out = pl.pallas_call(
        _ada_kernel,
        grid=(depth, n // tn),
        in_specs=[pl.BlockSpec((bp, d), lambda l, j: (0, 0)),
                  pl.BlockSpec((1, d, tn), lambda l, j: (l, 0, j)),
                  pl.BlockSpec((1, 1, tn), lambda l, j: (l, 0, j))],
        out_specs=pl.BlockSpec((1, bp, tn), lambda l, j: (l, 0, j)),
        out_shape=jax.ShapeDtypeStruct((depth, bp, n), F32),
        compiler_params=_params("arbitrary", "arbitrary"),
        name="ada_mod",
    )(cp, ada_w, ada_b.reshape(depth, 1, n))
    return out[:, :b]


def _inproj_kernel(x_ref, g_ref, sh_ref, sc_ref, w_ref, o_ref, *, tn):
    x = x_ref[...]
    ms = jnp.mean(x * x, axis=-1, keepdims=True)
    y = x * lax.rsqrt(ms + EPS) * g_ref[...]
    hm = (y * (1.0 + sc_ref[0]) + sh_ref[0]).astype(BF16)
    for j in range(w_ref.shape[1] // tn):
        o_ref[:, j * tn:(j + 1) * tn] = _dot(hm, w_ref[:, j * tn:(j + 1) * tn]).astype(o_ref.dtype)


def inproj(h2d, gain, shift, scale, w, seq, tm, tn):
    t, d = h2d.shape
    n = w.shape[1]
    tpb = seq // tm
    return pl.pallas_call(
        functools.partial(_inproj_kernel, tn=tn),
        grid=(t // tm,),
        in_specs=[pl.BlockSpec((tm, d), lambda i: (i, 0)),
                  pl.BlockSpec((1, d), lambda i: (0, 0)),
                  pl.BlockSpec((1, 1, d), lambda i: (i // tpb, 0, 0)),
                  pl.BlockSpec((1, 1, d), lambda i: (i // tpb, 0, 0)),
                  pl.BlockSpec((d, n), lambda i: (0, 0), pipeline_mode=pl.Buffered(1))],
        out_specs=pl.BlockSpec((tm, n), lambda i: (i, 0)),
        out_shape=jax.ShapeDtypeStruct((t, n), BF16),
        compiler_params=_params("arbitrary"),
        name="inproj",
    )(h2d, gain.reshape(1, d), shift[:, None, :], scale[:, None, :], w)


def _gla_kernel(q_ref, k_ref, v_ref, og_ref, misc_ref, wg_ref, bg_ref, on_ref, tri_ref,
                o_ref, st_ref, cum_ref):
    tm = q_ref.shape[0]

    @pl.when(pl.program_id(1) == 0)
    def _():
        st_ref[...] = jnp.zeros_like(st_ref)

    g = _dot(misc_ref[...], wg_ref[...]) + bg_ref[...]
    la = (jnp.minimum(g, 0.0) - jnp.log1p(jnp.exp(-jnp.abs(g)))) * (1.0 / GLA_GATE_NORM)
    la = jnp.where((_lane(la.shape) & (LANES - 1)) < GLA_DK, la, 0.0)
    cum_ref[...] = _dot_exact_lhs(tri_ref[...], la)

    causal = _row((CHUNK, CHUNK)) >= _lane((CHUNK, CHUNK))
    for c in range(tm // CHUNK):
        rows = pl.ds(c * CHUNK, CHUNK)
        for h in range(GLA_HEADS):
            cols = pl.ds(h * LANES, LANES)
            cu = cum_ref[rows, cols]
            last = cu[CHUNK - 1:CHUNK, :]
            qh = q_ref[rows, cols].astype(F32) * (GLA_DK ** -0.5)
            kh = k_ref[rows, cols].astype(F32)
            q_dec = (qh * jnp.exp(cu)).astype(BF16)
            k_inv = (kh * jnp.exp(-cu)).astype(BF16)
            k_end = (kh * jnp.exp(last - cu)).astype(BF16)
            vh = v_ref[rows, cols]
            att = jnp.where(causal, _dot_nt(q_dec, k_inv), 0.0)
            st = st_ref[h]
            o = _dot(att.astype(BF16), vh) + _dot_nt(q_dec, st.astype(BF16))
            st_ref[h] = st * jnp.exp(last) + _dot_tn(vh, k_end)
            ms = jnp.mean(o * o, axis=-1, keepdims=True)
            on = o * lax.rsqrt(ms + EPS) * on_ref[:, cols]
            o_ref[rows, cols] = (on * _silu(og_ref[rows, cols].astype(F32))).astype(o_ref.dtype)


def gla(u, cb, wg, bg, onorm, tri, batch, seq, tm):
    t = u.shape[0]
    tpb = seq // tm
    w512 = GLA_HEADS * LANES
    row = lambda b, i: b * tpb + i
    return pl.pallas_call(
        _gla_kernel,
        grid=(batch, tpb),
        in_specs=[pl.BlockSpec((tm, w512), lambda b, i: (row(b, i), cb["q"])),
                  pl.BlockSpec((tm, w512), lambda b, i: (row(b, i), cb["k"])),
                  pl.BlockSpec((tm, w512), lambda b, i: (row(b, i), cb["v"])),
                  pl.BlockSpec((tm, w512), lambda b, i: (row(b, i), cb["og"])),
                  pl.BlockSpec((tm, LANES), lambda b, i: (row(b, i), cb["glr"])),
                  pl.BlockSpec((LANES, w512), lambda b, i: (0, 0)),
                  pl.BlockSpec((1, w512), lambda b, i: (0, 0)),
                  pl.BlockSpec((1, w512), lambda b, i: (0, 0)),
                  pl.BlockSpec((tm, tm), lambda b, i: (0, 0))],
        out_specs=pl.BlockSpec((tm, w512), lambda b, i: (row(b, i), 0)),
        out_shape=jax.ShapeDtypeStruct((t, w512), BF16),
        scratch_shapes=[pltpu.VMEM((GLA_HEADS, GLA_DV, LANES), F32),
                        pltpu.VMEM((tm, w512), F32)],
        compiler_params=_params("arbitrary", "arbitrary"),
        name="gla",
    )(u, u, u, u, u, wg, bg, onorm, tri)


def _causal_conv(ext_ref, x, w_ref, b_ref, first):
    tm = x.shape[0]

    @pl.when(first)
    def _():
        ext_ref[0:8, :] = jnp.zeros((8, ext_ref.shape[1]), F32)

    ext_ref[8:8 + tm, :] = x
    y = b_ref[...] + w_ref[CONV_K - 1:CONV_K, :] * x
    for kk in range(CONV_K - 1):
        off = 8 - (CONV_K - 1) + kk
        y = y + w_ref[kk:kk + 1, :] * ext_ref[off:off + tm, :]
    ext_ref[0:8, :] = ext_ref[tm:tm + 8, :]
    return y


def _ssd_kernel(xs_ref, z_ref, bc_ref, dte_ref, dto_ref,
                cwx_ref, cbx_ref, cwb_ref, cbb_ref, dtb_ref, alog_ref, dexp_ref, ng_ref,
                expand_ref, tri_ref, o_ref,
                extx_ref, extb_ref, st_ref, cume_ref, cumo_ref, cumx_ref, xdt_ref, bcc_ref, xsc_ref):
    tm = xs_ref.shape[0]
    first = pl.program_id(1) == 0

    @pl.when(first)
    def _():
        st_ref[...] = jnp.zeros_like(st_ref)

    xs = _silu(_causal_conv(extx_ref, xs_ref[...].astype(F32), cwx_ref, cbx_ref, first))
    bcv = _silu(_causal_conv(extb_ref, bc_ref[...].astype(F32), cwb_ref, cbb_ref, first))
    xsc_ref[...] = xs
    bcc_ref[...] = bcv.astype(BF16)

    npair = SSD_HEADS // 2
    lane = _lane((tm, LANES))
    valid = lane < npair
    a_e = -jnp.exp(alog_ref[0:1, :])
    a_o = -jnp.exp(alog_ref[1:2, :])
    dt_e = jnp.where(valid, _softplus(dte_ref[...].astype(F32) + dtb_ref[0:1, :]), 0.0)
    dt_o = jnp.where(valid, _softplus(dto_ref[...].astype(F32) + dtb_ref[1:2, :]), 0.0)
    tri = tri_ref[...]
    cum_e = _dot_exact_lhs(tri, dt_e * a_e)
    cum_o = _dot_exact_lhs(tri, dt_o * a_o)
    cume_ref[...] = cum_e
    cumo_ref[...] = cum_o
    ex_e = expand_ref[0]
    ex_o = expand_ref[1]
    cumx_ref[...] = _dot_exact_rhs(cum_e, ex_e) + _dot_exact_rhs(cum_o, ex_o)
    xdt_ref[...] = xs * (_dot_exact_rhs(dt_e, ex_e) + _dot_exact_rhs(dt_o, ex_o))

    rr = _row((CHUNK, LANES))
    ll = _lane((CHUNK, LANES))
    causal2 = rr >= (ll & (CHUNK - 1))
    left = ll < CHUNK
    gw = SSD_GINNER
    for c in range(tm // CHUNK):
        rows = pl.ds(c * CHUNK, CHUNK)
        cumx = cumx_ref[rows, :]
        lastx = cumx[CHUNK - 1:CHUNK, :]
        xdt = xdt_ref[rows, :]
        xw = (xdt * jnp.exp(lastx - cumx)).astype(BF16)
        ecum = jnp.exp(cumx)
        pt = jnp.concatenate([cume_ref[rows, :], cumo_ref[rows, :]], axis=0).T
        ys = []
        for g in range(SSD_GROUPS):
            bg = bcc_ref[rows, pl.ds(g * SSD_STATE, SSD_STATE)]
            cg = bcc_ref[rows, pl.ds((SSD_GROUPS + g) * SSD_STATE, SSD_STATE)]
            st = st_ref[g]
            y_off = _dot(cg, st.astype(BF16)) * ecum[:, g * gw:(g + 1) * gw]
            cbcb = _dot_nt(cg, jnp.concatenate([bg, bg], axis=0))
            parts = []
            for j in range(npair // SSD_GROUPS):
                jp = g * (npair // SSD_GROUPS) + j
                colp = cumx[:, jp * LANES:(jp + 1) * LANES]
                seg = colp - pt[jp:jp + 1, :]
                dec = jnp.exp(jnp.where(causal2, seg, -jnp.inf))
                w = (cbcb * dec).astype(BF16)
                xp = xdt[:, jp * LANES:(jp + 1) * LANES]
                x2 = jnp.concatenate([jnp.where(left, xp, 0.0), jnp.where(left, 0.0, xp)],
                                     axis=0).astype(BF16)
                parts.append(_dot(w, x2))
            ys.append(jnp.concatenate(parts, axis=1) + y_off)
            st_ref[g] = (st * jnp.exp(lastx[:, g * gw:(g + 1) * gw])
                         + _dot_tn(bg, xw[:, g * gw:(g + 1) * gw]))
        y = jnp.concatenate(ys, axis=1)
        y = y + dexp_ref[...] * xsc_ref[rows, :]
        y = y * _silu(z_ref[rows, :].astype(F32))
        for g in range(SSD_GROUPS):
            yg = y[:, g * gw:(g + 1) * gw]
            ms = jnp.mean(yg * yg, axis=-1, keepdims=True)
            o_ref[rows, pl.ds(g * gw, gw)] = (
                yg * lax.rsqrt(ms + EPS) * ng_ref[:, g * gw:(g + 1) * gw]).astype(o_ref.dtype)


def ssd(u, cb, cwx, cbx, cwb, cbb, dtb, alog, dexp, ng, expand, tri, batch, seq, tm):
    t = u.shape[0]
    tpb = seq // tm
    row = lambda b, i: b * tpb + i
    bcw = 2 * SSD_GROUPS * SSD_STATE
    const = lambda shape: pl.BlockSpec(shape, lambda b, i: tuple(0 for _ in shape))
    return pl.pallas_call(
        _ssd_kernel,
        grid=(batch, tpb),
        in_specs=[pl.BlockSpec((tm, SSD_INNER), lambda b, i: (row(b, i), cb["xs"])),
                  pl.BlockSpec((tm, SSD_INNER), lambda b, i: (row(b, i), cb["z"])),
                  pl.BlockSpec((tm, bcw), lambda b, i: (row(b, i), cb["bc"])),
                  pl.BlockSpec((tm, LANES), lambda b, i: (row(b, i), cb["dte"])),
                  pl.BlockSpec((tm, LANES), lambda b, i: (row(b, i), cb["dto"])),
                  const((CONV_K, SSD_INNER)), const((1, SSD_INNER)),
                  const((CONV_K, bcw)), const((1, bcw)),
                  const((2, LANES)), const((2, LANES)),
                  const((1, SSD_INNER)), const((1, SSD_INNER)),
                  const((2, LANES, SSD_INNER)), const((tm, tm))],
        out_specs=pl.BlockSpec((tm, SSD_INNER), lambda b, i: (row(b, i), 0)),
        out_shape=jax.ShapeDtypeStruct((t, SSD_INNER), BF16),
        scratch_shapes=[pltpu.VMEM((tm + 8, SSD_INNER), F32),
                        pltpu.VMEM((tm + 8, bcw), F32),
                        pltpu.VMEM((SSD_GROUPS, SSD_STATE, SSD_GINNER), F32),
                        pltpu.VMEM((tm, LANES), F32),
                        pltpu.VMEM((tm, LANES), F32),
                        pltpu.VMEM((tm, SSD_INNER), F32),
                        pltpu.VMEM((tm, SSD_INNER), F32),
                        pltpu.VMEM((tm, bcw), BF16),
                        pltpu.VMEM((tm, SSD_INNER), F32)],
        compiler_params=_params("arbitrary", "arbitrary"),
        name="ssd",
    )(u, u, u, u, u, cwx, cbx, cwb, cbb, dtb, alog, dexp, ng, expand, tri)


def _gelu_tanh(x):
    return 0.5 * x * (1.0 + jnp.tanh(math.sqrt(2.0 / math.pi) * (x + 0.044715 * (x * x * x))))


def _lru_kernel(gate_ref, xr_ref, cw_ref, cb_ref, wa_ref, ba_ref, wx_ref, bx_ref, lam_ref,
                o_ref, ext_ref, carry_ref):
    tm, width = xr_ref.shape
    first = pl.program_id(1) == 0

    @pl.when(first)
    def _():
        carry_ref[...] = jnp.zeros_like(carry_ref)

    x = _causal_conv(ext_ref, xr_ref[...].astype(F32), cw_ref, cb_ref, first)
    xb = x.astype(BF16)
    nblk = wa_ref.shape[0]
    bw = width // nblk
    ra = jnp.concatenate([_dot(xb[:, n * bw:(n + 1) * bw], wa_ref[n]) for n in range(nblk)], axis=1)
    rx = jnp.concatenate([_dot(xb[:, n * bw:(n + 1) * bw], wx_ref[n]) for n in range(nblk)], axis=1)
    r = jax.nn.sigmoid(ra + ba_ref[...])
    ig = jax.nn.sigmoid(rx + bx_ref[...])
    log_a = (-LRU_C) * r * _softplus(-lam_ref[...])
    a = jnp.exp(log_a)
    u = jnp.sqrt(1.0 - jnp.exp(2.0 * log_a)) * (ig * x)

    ngrp = tm // SUBLANES
    a3 = a.reshape(ngrp, SUBLANES, width)
    u3 = u.reshape(ngrp, SUBLANES, width)
    sub = lax.broadcasted_iota(jnp.int32, a3.shape, 1)
    d = 1
    while d < SUBLANES:
        m = sub >= d
        a_s = pltpu.roll(a3, d, 1)
        u_s = pltpu.roll(u3, d, 1)
        u3 = jnp.where(m, a3 * u_s + u3, u3)
        a3 = jnp.where(m, a3 * a_s, a3)
        d *= 2
    carry = carry_ref[0:1, :]
    groups = []
    for j in range(ngrp):
        hj = a3[j] * carry + u3[j]
        groups.append(hj)
        carry = hj[SUBLANES - 1:SUBLANES, :]
    hseq = jnp.concatenate(groups, axis=0)
    carry_ref[...] = jnp.broadcast_to(carry, carry_ref.shape)
    o_ref[...] = (hseq * _gelu_tanh(gate_ref[...].astype(F32))).astype(o_ref.dtype)


def lru(u, cb, cw, cbias, wa, ba, wx, bx, lam, batch, seq, tm):
    t = u.shape[0]
    width = cw.shape[1]
    tpb = seq // tm
    row = lambda b, i: b * tpb + i
    const = lambda shape: pl.BlockSpec(shape, lambda b, i: tuple(0 for _ in shape))
    return pl.pallas_call(
        _lru_kernel,
        grid=(batch, tpb),
        in_specs=[pl.BlockSpec((tm, width), lambda b, i: (row(b, i), cb["gate"])),
                  pl.BlockSpec((tm, width), lambda b, i: (row(b, i), cb["xr"])),
                  const((CONV_K, width)), const((1, width)),
                  const(wa.shape), const((1, width)),
                  const(wx.shape), const((1, width)), const((1, width))],
        out_specs=pl.BlockSpec((tm, width), lambda b, i: (row(b, i), 0)),
        out_shape=jax.ShapeDtypeStruct((t, width), BF16),
        scratch_shapes=[pltpu.VMEM((tm + 8, width), F32),
                        pltpu.VMEM((8, width), F32)],
        compiler_params=_params("arbitrary", "arbitrary"),
        name="lru",
    )(u, u, cw, cbias, wa, ba, wx, bx, lam)


def _mla_prep_kernel(uq_ref, ukv_ref, krr_ref, pos_ref, freq_ref, qn_ref, kvn_ref,
                     wq_ref, wkt_ref, wv_ref, qg_ref, kgt_ref, q_ref, kt_ref, v_ref):
    tm = uq_ref.shape[0]
    nrep = tm // LANES
    rep = lambda a: jnp.concatenate([a] * nrep, axis=1)
    lane = _lane((tm, LANES))
    lo_half = lane < MLA_ROPE

    ang_t = rep(freq_ref[...]) * pos_ref[...].astype(F32)
    cos_t = jnp.cos(ang_t)
    sin_t = jnp.sin(ang_t)
    cs_t = jnp.concatenate([cos_t, cos_t, -sin_t, sin_t], axis=0)
    cs = cs_t.T

    def latent_norm(ref, g_ref):
        x = ref[...].astype(F32)
        ms = jnp.mean(x * x, axis=-1, keepdims=True)
        return (x * lax.rsqrt(ms + EPS) * g_ref[...]).astype(BF16)

    qf = _dot(latent_norm(uq_ref, qn_ref), wq_ref[...])
    ukv_n = latent_norm(ukv_ref, kvn_ref)
    vf = _dot(ukv_n, wv_ref[...])
    kn_t = _dot_nt(wkt_ref[...], ukv_n)
    kr_t = krr_ref[...].astype(F32).T
    kr_lo = kr_t[0:MLA_ROPE]
    kr_ss = jnp.sum(kr_lo * kr_lo, axis=0, keepdims=True)
    scale = MLA_QK ** -0.5 * math.log2(math.e)

    def rope_half(y2):
        t = y2 * cs
        return jnp.where(lo_half, t + pltpu.roll(t, MLA_ROPE, 1), 0.0)

    g_nope_t = rep(kgt_ref[0:MLA_NOPE, :])
    t = kr_t * rep(kgt_ref[MLA_NOPE:MLA_NOPE + 2 * MLA_ROPE, :]) * cs_t
    kr_rope_t = t[0:MLA_ROPE] + t[MLA_ROPE:2 * MLA_ROPE]
    pad_rows = MLA_QK_PAD - MLA_QK
    zeros_t = jnp.zeros((pad_rows, tm), kt_ref.dtype)

    for h in range(MLA_HEADS):
        x1 = qf[:, h * MLA_QK_PAD:h * MLA_QK_PAD + LANES]
        x2 = qf[:, h * MLA_QK_PAD + LANES:(h + 1) * MLA_QK_PAD]
        ss = jnp.sum(x1 * x1 + jnp.where(lo_half, x2 * x2, 0.0), axis=-1, keepdims=True)
        r = lax.rsqrt(ss * (1.0 / MLA_QK) + EPS) * scale
        q_ref[:, pl.ds(h * MLA_QK_PAD, LANES)] = (x1 * r * qg_ref[:, 0:LANES]).astype(q_ref.dtype)
        q_ref[:, pl.ds(h * MLA_QK_PAD + LANES, LANES)] = rope_half(
            x2 * r * qg_ref[:, LANES:2 * LANES]).astype(q_ref.dtype)

        kn = kn_t[h * MLA_NOPE:(h + 1) * MLA_NOPE]
        ssk = jnp.sum(kn * kn, axis=0, keepdims=True) + kr_ss
        rk = lax.rsqrt(ssk * (1.0 / MLA_QK) + EPS)
        base = h * MLA_QK_PAD
        kt_ref[0, 0, pl.ds(base, MLA_NOPE), :] = (kn * rk * g_nope_t).astype(kt_ref.dtype)
        kt_ref[0, 0, pl.ds(base + MLA_NOPE, MLA_ROPE), :] = (kr_rope_t * rk).astype(kt_ref.dtype)
        kt_ref[0, 0, pl.ds(base + MLA_QK, pad_rows), :] = zeros_t
        v_ref[:, pl.ds(h * MLA_V, MLA_V)] = vf[:, h * MLA_V:(h + 1) * MLA_V].astype(v_ref.dtype)


def mla_prep(u, cb, pos_row, freq_col, qn, kvn, wq, wkt, wv, qg, kgt, seq, tm):
    t = u.shape[0]
    tpb = seq // tm
    const = lambda shape: pl.BlockSpec(shape, lambda i: tuple(0 for _ in shape))
    hq = MLA_HEADS * MLA_QK_PAD
    hv = MLA_HEADS * MLA_V
    return pl.pallas_call(
        _mla_prep_kernel,
        grid=(t // tm,),
        in_specs=[pl.BlockSpec((tm, MLA_Q_LORA), lambda i: (i, cb["uq"])),
                  pl.BlockSpec((tm, MLA_KV_LORA), lambda i: (i, cb["ukv"])),
                  pl.BlockSpec((tm, LANES), lambda i: (i, cb["krr"])),
                  pl.BlockSpec((1, tm), lambda i: (0, i)),
                  const(freq_col.shape), const((1, MLA_Q_LORA)), const((1, MLA_KV_LORA)),
                  const(wq.shape), const(wkt.shape), const(wv.shape),
                  const((1, 2 * LANES)), const(kgt.shape)],
        out_specs=[pl.BlockSpec((tm, hq), lambda i: (i, 0)),
                   pl.BlockSpec((1, 1, hq, tm), lambda i: (i // tpb, i % tpb, 0, 0)),
                   pl.BlockSpec((tm, hv), lambda i: (i, 0))],
        out_shape=[jax.ShapeDtypeStruct((t, hq), BF16),
                   jax.ShapeDtypeStruct((t // seq, tpb, hq, tm), BF16),
                   jax.ShapeDtypeStruct((t, hv), BF16)],
        compiler_params=_params("arbitrary"),
        name="mla_prep",
    )(u, u, u, pos_row, freq_col, qn, kvn, wq, wkt, wv, qg, kgt)


def _attn_kernel(q_ref, k_ref, v_ref, o_ref, m_ref, acc_ref):
    i = pl.program_id(2)
    tq = q_ref.shape[0]
    nh = q_ref.shape[1] // MLA_QK_PAD
    nc = tq // LANES

    m_ref[...] = jnp.full_like(m_ref, -jnp.inf)
    acc_ref[...] = jnp.zeros_like(acc_ref)
    ones = jnp.ones((tq, LANES), BF16)

    def update(g, s, v):
        m_prev = m_ref[g]
        m_cur = s[:, 0:LANES]
        for c in range(1, nc):
            m_cur = jnp.maximum(m_cur, s[:, c * LANES:(c + 1) * LANES])
        m_new = jnp.maximum(m_prev, jnp.max(m_cur, axis=-1, keepdims=True))
        alpha = jnp.exp2(m_prev - m_new)
        p = jnp.concatenate(
            [jnp.exp2((s[:, c * LANES:(c + 1) * LANES] - m_new).astype(BF16)) for c in range(nc)],
            axis=1)
        pv = _dot(p, jnp.concatenate([v, ones], axis=1))
        acc_ref[g] = jnp.concatenate([alpha, alpha], axis=1) * acc_ref[g] + pv
        m_ref[g] = m_new

    def block(j, masked):
        rows = pl.ds(pl.multiple_of(j * tq, tq), tq)
        for g in range(nh):
            s = _dot(q_ref[:, g * MLA_QK_PAD:(g + 1) * MLA_QK_PAD],
                     k_ref[0, j, pl.ds(g * MLA_QK_PAD, MLA_QK_PAD), :])
            if masked:
                s = jnp.where(_row((tq, tq)) >= _lane((tq, tq)), s, -jnp.inf)
            update(g, s, v_ref[rows, pl.ds(g * MLA_V, MLA_V)])

    def body(jj, carry):
        for u in range(ATTN_KV_UNROLL):
            block(ATTN_KV_UNROLL * jj + u, False)
        return carry

    shift = ATTN_KV_UNROLL.bit_length() - 1
    ntrip = i >> shift
    lax.fori_loop(0, ntrip, body, 0)

    def rest(j, carry):
        block(j, False)
        return carry

    lax.fori_loop(ntrip << shift, i, rest, 0)

    block(i, True)
    for g in range(nh):
        acc = acc_ref[g]
        o_ref[:, pl.ds(g * MLA_V, MLA_V)] = (acc[:, :MLA_V] / acc[:, MLA_V:]).astype(o_ref.dtype)


def attention(q, k, v, batch, seq, tq):
    t = q.shape[0]
    nq = seq // tq
    nh = ATTN_HEADS_PER_STEP
    return pl.pallas_call(
        _attn_kernel,
        grid=(batch, MLA_HEADS // nh, nq),
        in_specs=[pl.BlockSpec((tq, nh * MLA_QK_PAD), lambda b, h, i: (b * nq + i, h)),
                  pl.BlockSpec((1, nq, nh * MLA_QK_PAD, tq), lambda b, h, i: (b, 0, h, 0),
                               pipeline_mode=pl.Buffered(1)),
                  pl.BlockSpec((seq, nh * MLA_V), lambda b, h, i: (b, h),
                               pipeline_mode=pl.Buffered(1))],
        out_specs=pl.BlockSpec((tq, nh * MLA_V), lambda b, h, i: (b * nq + i, h)),
        out_shape=jax.ShapeDtypeStruct((t, MLA_HEADS * MLA_V), BF16),
        scratch_shapes=[pltpu.VMEM((nh, tq, LANES), F32),
                        pltpu.VMEM((nh, tq, MLA_V + LANES), F32)],
        compiler_params=_params("arbitrary", "arbitrary", "arbitrary"),
        name="attention",
    )(q, k, v)


def _outproj_kernel(oa_ref, ob_ref, w1_ref, w2_ref, h_ref, g1_ref, nf_ref, sh_ref, sc_ref,
                    rwt_ref, hn_ref, hf_ref, lg_ref):
    y = _dot(oa_ref[...], w1_ref[...]) + _dot(ob_ref[...], w2_ref[...])
    hn = h_ref[...] + g1_ref[0] * y
    hn_ref[...] = hn
    ms = jnp.mean(hn * hn, axis=-1, keepdims=True)
    hf = hn * lax.rsqrt(ms + EPS) * nf_ref[...] * (1.0 + sc_ref[0]) + sh_ref[0]
    hf_ref[...] = hf.astype(BF16)
    r_hi, r_mid, r_lo = _split3(rwt_ref[...])
    f_hi, f_mid, _ = _split3(hf)
    ne = r_hi.shape[0]
    a = _dot_nt(jnp.concatenate([r_hi, r_mid, r_lo], axis=0), f_hi)
    b = _dot_nt(jnp.concatenate([r_hi, r_mid], axis=0), f_mid)
    lg_ref[...] = a[0:ne] + b[0:ne] + a[ne:2 * ne] + a[2 * ne:3 * ne] + b[ne:2 * ne]


def outproj(oa, ob, w1, w2, h2d, g1, nf, sh2, sc2, rwt, seq, tm):
    t, d = h2d.shape
    tpb = seq // tm
    const = lambda shape: pl.BlockSpec(shape, lambda i: tuple(0 for _ in shape))
    bvec = pl.BlockSpec((1, 1, d), lambda i: (i // tpb, 0, 0))
    return pl.pallas_call(
        _outproj_kernel,
        grid=(t // tm,),
        in_specs=[pl.BlockSpec((tm, oa.shape[1]), lambda i: (i, 0)),
                  pl.BlockSpec((tm, ob.shape[1]), lambda i: (i, 0)),
                  const(w1.shape), const(w2.shape),
                  pl.BlockSpec((tm, d), lambda i: (i, 0)),
                  bvec, const((1, d)), bvec, bvec, const(rwt.shape)],
        out_specs=[pl.BlockSpec((tm, d), lambda i: (i, 0)),
                   pl.BlockSpec((tm, d), lambda i: (i, 0)),
                   pl.BlockSpec((N_EXPERTS, tm), lambda i: (0, i))],
        out_shape=[jax.ShapeDtypeStruct((t, d), F32),
                   jax.ShapeDtypeStruct((t, d), BF16),
                   jax.ShapeDtypeStruct((N_EXPERTS, t), F32)],
        compiler_params=_params("arbitrary"),
        name="outproj",
    )(oa, ob, w1, w2, h2d, g1[:, None, :], nf.reshape(1, d), sh2[:, None, :], sc2[:, None, :], rwt)


def _route_sort_kernel(*refs):
    last = pl.program_id(0) == pl.num_programs(0) - 1

    @pl.when(last)
    def _():
        xs_ref = refs[5]
        xs_ref[...] = jnp.zeros_like(xs_ref)

    @pl.when(jnp.logical_not(last))
    def _():
        _route_sort_body(*refs)


def _route_sort_body(lg_ref, bias_ref, hf_ref, utri_ref, ltri_ref,
                     xs_ref, lpos_ref, wts_ref, cnt_ref):
    scores = jax.nn.sigmoid(lg_ref[...])
    sel = scores + bias_ref[...]
    tm = sel.shape[1]
    eidx = _row((N_EXPERTS, tm))
    neg = -jnp.inf

    best_score = None
    best_grp = None
    for g in range(N_GROUPS):
        m = [sel[g * EXPERTS_PER_GROUP + i:g * EXPERTS_PER_GROUP + i + 1, :]
             for i in range(EXPERTS_PER_GROUP)]
        gs = None
        for a in range(EXPERTS_PER_GROUP):
            for b in range(a + 1, EXPERTS_PER_GROUP):
                pair = m[a] + m[b]
                gs = pair if gs is None else jnp.maximum(gs, pair)
        if best_score is None:
            best_score, best_grp = gs, jnp.zeros_like(gs, dtype=jnp.int32)
        else:
            better = gs > best_score
            best_score = jnp.where(better, gs, best_score)
            best_grp = jnp.where(better, g, best_grp)

    masked = jnp.where((eidx >> 2) == best_grp, sel, neg)
    m1 = jnp.max(masked, axis=0, keepdims=True)
    i1 = jnp.min(jnp.where(masked == m1, eidx, N_EXPERTS), axis=0, keepdims=True)
    masked2 = jnp.where(eidx == i1, neg, masked)
    m2 = jnp.max(masked2, axis=0, keepdims=True)
    i2 = jnp.min(jnp.where(masked2 == m2, eidx, N_EXPERTS), axis=0, keepdims=True)
    pick1 = eidx == i1
    pick2 = eidx == i2
    s1 = jnp.sum(jnp.where(pick1, scores, 0.0), axis=0, keepdims=True)
    s2 = jnp.sum(jnp.where(pick2, scores, 0.0), axis=0, keepdims=True)
    tot = s1 + s2
    wts_ref[...] = jnp.concatenate([s1 / tot, s2 / tot], axis=0)

    nr = xs_ref.shape[0]
    picks = jnp.where(pick1, 1.0, jnp.where(pick2, 1.0, 0.0))
    csum = _dot(picks.astype(BF16), utri_ref[...])
    cnt = jnp.sum(picks, axis=1, keepdims=True)
    cnt_ref[0] = jnp.broadcast_to(cnt, (N_EXPERTS, LANES))
    cnt_pad = jnp.floor((cnt + (ROW_CHUNK - 1.0)) * (1.0 / ROW_CHUNK)) * ROW_CHUNK
    seg_off = _dot(ltri_ref[...], jnp.broadcast_to(cnt_pad, (N_EXPERTS, LANES)).astype(BF16))[:, 0:1]
    lposmat = seg_off + csum - 1.0
    lp1 = jnp.sum(jnp.where(pick1, lposmat, 0.0), axis=0, keepdims=True).astype(jnp.int32)
    lp2 = jnp.sum(jnp.where(pick2, lposmat, 0.0), axis=0, keepdims=True).astype(jnp.int32)
    lpos_ref[...] = jnp.concatenate([lp1, lp2], axis=0)
    rowi = _row((nr, tm))
    onehot = jnp.where(rowi == lp1, 1.0, jnp.where(rowi == lp2, 1.0, 0.0)).astype(BF16)
    xs_ref[...] = _dot(onehot, hf_ref[...]).astype(xs_ref.dtype)


def route_sort(logits_t, router_bias, hf, tl, nr):
    e, t = logits_t.shape
    d = hf.shape[1]
    ntile = t // tl
    r = jnp.arange(tl)
    utri = (r[:, None] <= r[None, :]).astype(BF16)
    re = jnp.arange(e)
    ltri = (re[None, :] < re[:, None]).astype(BF16)
    const = lambda shape: pl.BlockSpec(shape, lambda i: tuple(0 for _ in shape))
    tile = lambda i: jnp.minimum(i, ntile - 1)
    return pl.pallas_call(
        _route_sort_kernel,
        grid=(ntile + 1,),
        in_specs=[pl.BlockSpec((e, tl), lambda i: (0, tile(i))),
                  const((e, 1)),
                  pl.BlockSpec((tl, d), lambda i: (tile(i), 0)),
                  const((tl, tl)), const((e, e))],
        out_specs=[pl.BlockSpec((nr, d), lambda i: (i, 0)),
                   pl.BlockSpec((2, tl), lambda i: (0, tile(i))),
                   pl.BlockSpec((2, tl), lambda i: (0, tile(i))),
                   pl.BlockSpec((1, e, LANES), lambda i: (tile(i), 0, 0))],
        out_shape=[jax.ShapeDtypeStruct(((ntile + 1) * nr, d), BF16),
                   jax.ShapeDtypeStruct((2, t), jnp.int32),
                   jax.ShapeDtypeStruct((2, t), F32),
                   jax.ShapeDtypeStruct((t // tl, e, LANES), F32)],
        compiler_params=_params("arbitrary"),
        name="route_sort",
    )(logits_t, router_bias.reshape(e, 1).astype(F32), hf, utri, ltri)


def _moe_plan(cnt, nr, tmg, nt_max):
    ntile, ne = cnt.shape
    cpt = tmg // ROW_CHUNK
    nch = (cnt + ROW_CHUNK - 1) // ROW_CHUNK
    seg_off = jnp.cumsum(nch, axis=1) - nch
    cum = jnp.cumsum(nch, axis=0)
    tot = cum[-1]
    padded = (tot + cpt - 1) // cpt * cpt
    gend = jnp.cumsum(padded)
    gstart = gend - padded
    start = (gstart[None, :] + cum - nch).T.reshape(-1)
    end = start + nch.T.reshape(-1)
    base = (jnp.arange(ntile, dtype=jnp.int32)[:, None] * (nr // ROW_CHUNK) + seg_off).T.reshape(-1)
    c = jnp.arange(nt_max * cpt, dtype=jnp.int32)[:, None]
    hit = (c >= start[None, :]) & (c < end[None, :])
    valid = jnp.any(hit, axis=1)
    src = jnp.sum(jnp.where(hit, base[None, :] - start[None, :] + c, 0), axis=1) * ROW_CHUNK
    src = jnp.where(valid, src, nr - ROW_CHUNK)
    cflat = c[:, 0]
    spare = ntile * nr + ((cflat // cpt) % 2) * tmg + (cflat % cpt) * ROW_CHUNK
    dst = jnp.where(valid, src, spare)
    first = jnp.arange(nt_max, dtype=jnp.int32) * cpt
    e_first = jnp.sum(first[:, None] >= gend[None, :], axis=1).astype(jnp.int32)
    tile_on = (e_first < ne).astype(jnp.int32)
    return src.astype(jnp.int32), dst.astype(jnp.int32), jnp.minimum(e_first, ne - 1), tile_on


def _moe_group_kernel(src_ref, dst_ref, te_ref, on_ref, xs_hbm, wg_ref, wu_ref, wd_ref,
                      y_hbm, xbuf, ybuf, wgb, wub, wdb, in_sem, out_sem):
    del xs_hbm
    g = pl.program_id(0)
    ng = pl.num_programs(0)
    slot = lax.rem(g, 2)
    tmg = xbuf.shape[1]
    cpt = tmg // ROW_CHUNK

    def in_copy(tile, sl, k):
        row = pl.multiple_of(src_ref[tile * cpt + k], ROW_CHUNK)
        return pltpu.make_async_copy(y_hbm.at[pl.ds(row, ROW_CHUNK), :],
                                     xbuf.at[sl, pl.ds(k * ROW_CHUNK, ROW_CHUNK), :], in_sem.at[sl])

    def out_copy(tile, sl, k):
        row = pl.multiple_of(dst_ref[tile * cpt + k], ROW_CHUNK)
        return pltpu.make_async_copy(ybuf.at[sl, pl.ds(k * ROW_CHUNK, ROW_CHUNK), :],
                                     y_hbm.at[pl.ds(row, ROW_CHUNK), :], out_sem.at[sl])

    def start_in(tile, sl):
        for k in range(cpt):
            in_copy(tile, sl, k).start()

    def wait_in(tile, sl):
        for k in range(cpt):
            in_copy(tile, sl, k).wait()

    def start_out(tile, sl):
        for k in range(cpt):
            out_copy(tile, sl, k).start()

    def wait_out(tile, sl):
        for k in range(cpt):
            out_copy(tile, sl, k).wait()

    @pl.when((g >= 2) & (on_ref[jnp.maximum(g - 2, 0)] == 1))
    def _():
        wait_out(g - 2, slot)

    @pl.when((g == 0) & (on_ref[0] == 1))
    def _():
        start_in(0, 0)

    @pl.when((g + 1 < ng) & (on_ref[jnp.minimum(g + 1, ng - 1)] == 1))
    def _():
        start_in(g + 1, 1 - slot)

    @pl.when(on_ref[g] == 1)
    def _():
        @pl.when((g == 0) | (te_ref[g] != te_ref[jnp.maximum(g - 1, 0)]))
        def _():
            wgb[...] = wg_ref[0, 0].astype(BF16)
            wub[...] = wu_ref[0, 0].astype(BF16)
            wdb[...] = wd_ref[0, 0].astype(BF16)

        wait_in(g, slot)
        x = xbuf[slot]
        hid = (_silu(_dot(x, wgb[...])) * _dot(x, wub[...])).astype(BF16)
        ybuf[slot] = _dot(hid, wdb[...]).astype(ybuf.dtype)
        start_out(g, slot)

    @pl.when(g == ng - 1)
    def _():
        @pl.when((ng >= 2) & (on_ref[jnp.maximum(g - 1, 0)] == 1))
        def _():
            wait_out(g - 1, 1 - slot)

        @pl.when(on_ref[g] == 1)
        def _():
            wait_out(g, slot)


def moe_group(xs, wg, wu, wd, layer, plan, tmg, nt_max):
    src, dst, tile_e, tile_on = plan
    rows_out, d = xs.shape
    wspec = lambda w: pl.BlockSpec((1, 1) + w.shape[2:], lambda g, s, v, te, on: (layer, te[g], 0, 0))
    grid_spec = pltpu.PrefetchScalarGridSpec(
        num_scalar_prefetch=4,
        grid=(nt_max,),
        in_specs=[pl.BlockSpec(memory_space=pl.ANY), wspec(wg), wspec(wu), wspec(wd)],
        out_specs=pl.BlockSpec(memory_space=pl.ANY),
        scratch_shapes=[pltpu.VMEM((2, tmg, d), BF16), pltpu.VMEM((2, tmg, d), BF16),
                        pltpu.VMEM(wg.shape[2:], BF16), pltpu.VMEM(wu.shape[2:], BF16),
                        pltpu.VMEM(wd.shape[2:], BF16),
                        pltpu.SemaphoreType.DMA((2,)), pltpu.SemaphoreType.DMA((2,))])
    return pl.pallas_call(
        _moe_group_kernel,
        grid_spec=grid_spec,
        out_shape=jax.ShapeDtypeStruct((rows_out, d), BF16),
        input_output_aliases={4: 0},
        compiler_params=_params("arbitrary"),
        name="moe_group",
    )(src, dst, tile_e, tile_on, xs, wg, wu, wd)


def _moe_combine_kernel(y_ref, lpos_ref, wts_ref, h_ref, g2_ref, o_ref):
    nr = y_ref.shape[0]
    tl = h_ref.shape[0]
    rowi = _row((nr, tl))
    wc = (jnp.where(rowi == lpos_ref[0:1, :], wts_ref[0:1, :], 0.0)
          + jnp.where(rowi == lpos_ref[1:2, :], wts_ref[1:2, :], 0.0)).astype(BF16)
    o_ref[...] = h_ref[...] + g2_ref[0] * _dot_tn(wc, y_ref[...])


def moe_combine(y, lpos, wts, h2d, g2, seq, tl, nr):
    t, d = h2d.shape
    tpb = seq // tl
    return pl.pallas_call(
        _moe_combine_kernel,
        grid=(t // tl,),
        in_specs=[pl.BlockSpec((nr, d), lambda i: (i, 0)),
                  pl.BlockSpec((2, tl), lambda i: (0, i)),
                  pl.BlockSpec((2, tl), lambda i: (0, i)),
                  pl.BlockSpec((tl, d), lambda i: (i, 0)),
                  pl.BlockSpec((1, 1, d), lambda i: (i // tpb, 0, 0))],
        out_specs=pl.BlockSpec((tl, d), lambda i: (i, 0)),
        out_shape=jax.ShapeDtypeStruct((t, d), F32),
        compiler_params=_params("arbitrary"),
        name="moe_combine",
    )(y, lpos, wts, h2d, g2[:, None, :])


def _pad_heads(w, heads, dh):
    lead = w.shape[:-1]
    w = w.reshape(lead + (heads, dh))
    w = jnp.pad(w, [(0, 0)] * len(lead) + [(0, 0), (0, LANES - dh)])
    return w.reshape(lead + (heads * LANES,))


def _pad_cols(w, n):
    return jnp.pad(w, [(0, 0)] * (w.ndim - 1) + [(0, n - w.shape[-1])])


def _tri_blocks(tm):
    r = jnp.arange(tm)
    return ((r[:, None] >= r[None, :]) & (r[:, None] // CHUNK == r[None, :] // CHUNK)).astype(BF16)


def _even_layout(w_in):
    sizes = (GLA_HEADS * GLA_DK, GLA_HEADS * GLA_DK, GLA_HEADS * GLA_DV, GLA_GATE_RANK,
             GLA_HEADS * GLA_DV, SSD_INNER, SSD_INNER + 2 * SSD_GROUPS * SSD_STATE, SSD_HEADS)
    offs = [0]
    for s in sizes:
        offs.append(offs[-1] + s)
    seg = lambda i: w_in[:, offs[i]:offs[i + 1]]
    q, k, v, glr, og, z, xbc, dt = (seg(i) for i in range(8))
    xs, bc = xbc[:, :SSD_INNER], xbc[:, SSD_INNER:]
    cols = [xs, z, v, og, _pad_heads(q, GLA_HEADS, GLA_DK), _pad_heads(k, GLA_HEADS, GLA_DK), bc,
            _pad_cols(glr, LANES), _pad_cols(dt[:, 0::2], LANES), _pad_cols(dt[:, 1::2], LANES)]
    w = jnp.concatenate(cols, axis=1).astype(BF16)
    cb = {"xs": 0, "z": 1, "v": 4, "og": 5, "q": 6, "k": 7, "bc": 8, "glr": 36, "dte": 37, "dto": 38}
    return w, cb


def _odd_layout(w_in):
    d = w_in.shape[0]
    o = [0, d, 2 * d, 2 * d + MLA_Q_LORA, 2 * d + MLA_Q_LORA + MLA_KV_LORA,
         2 * d + MLA_Q_LORA + MLA_KV_LORA + MLA_ROPE]
    gate, xr, uq, ukv, kr = (w_in[:, o[i]:o[i + 1]] for i in range(5))
    half = MLA_ROPE // 2
    kr_sw = jnp.concatenate([kr[:, half:], kr[:, :half]], axis=1)
    w = jnp.concatenate([gate, xr, ukv, uq, kr, kr_sw], axis=1).astype(BF16)
    cb = {"gate": 0, "xr": 1, "ukv": 8, "uq": 6, "krr": 21}
    return w, cb


def _swap_halves(x):
    half = x.shape[-1] // 2
    return jnp.concatenate([x[..., half:], x[..., :half]], axis=-1)


def _block_diag(w, per):
    nb, bw, _ = w.shape
    w = w.reshape(nb // per, per, bw, bw)
    eye = jnp.eye(per, dtype=w.dtype)
    out = jnp.einsum("gpij,pq->gpiqj", w, eye)
    return out.reshape(nb // per, per * bw, per * bw)


def kernel(x, c, positions, router_w, router_bias, ada_w, ada_b, norm_mix, norm_ffn, moe_w_gate, moe_w_up, moe_w_down, ev_w_in, gla_w_g2, gla_b_g2, gla_onorm, ssd_conv_w, ssd_conv_b, ssd_dt_bias, ssd_a_log, ssd_d, ssd_norm, ev_w_out, od_w_in, lru_conv_w, lru_conv_b, lru_w_a, lru_b_a, lru_w_x, lru_b_x, lru_lambda, mla_q_norm, mla_w_q_up, mla_kv_norm, mla_w_kv_up, mla_q_qknorm, mla_k_qknorm, od_w_out):
    batch, seq, d = x.shape
    t = batch * seq
    depth = ada_w.shape[0]
    tm_seq = min(TM_SEQ, seq)
    tm_mm = min(TM_MM, seq)
    tq = min(TQ_ATTN, seq)
    tl = min(TL_MOE, seq)
    nr = 2 * tl + N_EXPERTS * ROW_CHUNK
    tmg = TM_MOE
    chunks_max = 2 * t // ROW_CHUNK + (t // tl) * N_EXPERTS + N_EXPERTS * (tmg // ROW_CHUNK - 1)
    nt_max = -(-chunks_max // (tmg // ROW_CHUNK))

    mod = ada_mod(c, ada_w, ada_b)
    rwt = router_w.T.astype(F32)
    tri = _tri_blocks(tm_seq)
    h = x.reshape(t, d)

    for layer in range(depth):
        sh1, sc1, g1, sh2, sc2, g2 = (mod[layer, :, i * d:(i + 1) * d] for i in range(6))
        i = layer // 2
        if layer % 2 == 0:
            w_in, cb = _even_layout(ev_w_in[i])
            u = inproj(h, norm_mix[layer], sh1, sc1, w_in, seq, tm_mm, w_in.shape[1] // 3)
            wg = jnp.pad(_pad_heads(gla_w_g2[i], GLA_HEADS, GLA_DK),
                         ((0, LANES - GLA_GATE_RANK), (0, 0))).astype(BF16)
            bg = _pad_heads(gla_b_g2[i][None, :], GLA_HEADS, GLA_DK)
            oa = gla(u, cb, wg, bg, gla_onorm[i][None, :], tri, batch, seq, tm_seq)
            cw, cbias = ssd_conv_w[i], ssd_conv_b[i][None, :]
            perm = lambda v: jnp.stack([_pad_cols(v[0::2], LANES), _pad_cols(v[1::2], LANES)])
            npair = SSD_HEADS // 2
            hp = jnp.arange(SSD_INNER) // SSD_HEADDIM
            ex_e = (jnp.arange(LANES)[:, None] * 2 == hp[None, :]) & (jnp.arange(LANES)[:, None] < npair)
            ex_o = (jnp.arange(LANES)[:, None] * 2 + 1 == hp[None, :]) & (jnp.arange(LANES)[:, None] < npair)
            expand = jnp.stack([ex_e, ex_o]).astype(BF16)
            ob = ssd(u, cb, cw[:, :SSD_INNER], cbias[:, :SSD_INNER], cw[:, SSD_INNER:],
                     cbias[:, SSD_INNER:], perm(ssd_dt_bias[i]), perm(ssd_a_log[i]),
                     jnp.repeat(ssd_d[i], SSD_HEADDIM)[None, :], ssd_norm[i][None, :],
                     expand, tri, batch, seq, tm_seq)
            w_out = ev_w_out[i].astype(BF16)
            w1, w2 = w_out[:GLA_HEADS * GLA_DV], w_out[GLA_HEADS * GLA_DV:]
        else:
            w_in, cb = _odd_layout(od_w_in[i])
            u = inproj(h, norm_mix[layer], sh1, sc1, w_in, seq, tm_mm, w_in.shape[1] // 2)
            per = 4
            oa = lru(u, cb, lru_conv_w[i], lru_conv_b[i][None, :],
                     _block_diag(lru_w_a[i], per).astype(BF16), lru_b_a[i][None, :],
                     _block_diag(lru_w_x[i], per).astype(BF16), lru_b_x[i][None, :],
                     lru_lambda[i][None, :], batch, seq, tm_seq)
            wq = mla_w_q_up[i].reshape(MLA_Q_LORA, MLA_HEADS, MLA_QK)
            wq = jnp.concatenate([wq, _swap_halves(wq[..., MLA_NOPE:])], axis=-1)
            wq = wq.reshape(MLA_Q_LORA, MLA_HEADS * MLA_QK_PAD).astype(BF16)
            wkv = mla_w_kv_up[i].reshape(MLA_KV_LORA, MLA_HEADS, MLA_NOPE + MLA_V)
            wkt = wkv[..., :MLA_NOPE].reshape(MLA_KV_LORA, -1).T.astype(BF16)
            wv = wkv[..., MLA_NOPE:].reshape(MLA_KV_LORA, -1).astype(BF16)
            ext_gain = lambda gq: jnp.concatenate([gq, _swap_halves(gq[MLA_NOPE:])])
            fr = ROPE_THETA ** (-jnp.arange(0, MLA_ROPE, 2, dtype=F32) / MLA_ROPE)
            lanes_rep = lambda col: jnp.broadcast_to(col[:, None], (col.shape[0], LANES))
            assert tm_mm == tq
            q, kt, v = mla_prep(u, cb, positions.reshape(1, t), lanes_rep(fr), mla_q_norm[i][None, :],
                                mla_kv_norm[i][None, :], wq, wkt, wv,
                                ext_gain(mla_q_qknorm[i])[None, :],
                                lanes_rep(ext_gain(mla_k_qknorm[i])), seq, tm_mm)
            ob = attention(q, kt, v, batch, seq, tq)
            w_out = od_w_out[i].astype(BF16)
            w1, w2 = w_out[:d], w_out[d:]
        h, hf, logits_t = outproj(oa, ob, w1, w2, h, g1, norm_ffn[layer], sh2, sc2, rwt, seq, tm_mm)
        assert 2 * tmg <= nr
        xs, lpos, wts, cnt = route_sort(logits_t, router_bias, hf, tl, nr)
        plan = _moe_plan(cnt[:, :, 0].astype(jnp.int32), nr, tmg, nt_max)
        y = moe_group(xs, moe_w_gate, moe_w_up, moe_w_down, layer, plan, tmg, nt_max)
        h = moe_combine(y, lpos, wts, h, g2, seq, tl, nr)
    return h.reshape(batch, seq, d)
```

```python
import functools
import math

import jax
import jax.numpy as jnp
from jax import lax
from jax.experimental import pallas as pl
from jax.experimental.pallas import tpu as pltpu

F32 = jnp.float32
BF16 = jnp.bfloat16

EPS = 1e-6
CHUNK = 64
CONV_K = 4
GLA_HEADS, GLA_DK, GLA_DV = 4, 64, 128
GLA_GATE_RANK, GLA_GATE_NORM = 16, 16.0
SSD_HEADS, SSD_HEADDIM, SSD_GROUPS, SSD_STATE = 16, 64, 2, 128
SSD_INNER = SSD_HEADS * SSD_HEADDIM
SSD_GINNER = SSD_INNER // SSD_GROUPS
LRU_BLOCKS, LRU_C = 16, 8.0
MLA_HEADS, MLA_NOPE, MLA_ROPE, MLA_V = 8, 128, 64, 128
MLA_QK = MLA_NOPE + MLA_ROPE
MLA_QK_PAD = 256
MLA_Q_LORA, MLA_KV_LORA = 384, 256
ROPE_THETA = 10000.0
N_EXPERTS, N_GROUPS, D_EXPERT = 16, 4, 512
EXPERTS_PER_GROUP = N_EXPERTS // N_GROUPS

TM_SEQ = 256
TM_MM = 512
TM_MOE = 512
TL_MOE = 512
ROW_CHUNK = 16
TQ_ATTN = 512
ATTN_HEADS_PER_STEP = 4
ATTN_KV_UNROLL = 4

LANES = 128
SUBLANES = 8
VMEM_LIMIT = 48 * 1024 * 1024

NT_DIMS = (((1,), (1,)), ((), ()))
TN_DIMS = (((0,), (0,)), ((), ()))


def _params(*sem):
    return pltpu.CompilerParams(dimension_semantics=sem, vmem_limit_bytes=VMEM_LIMIT)


def _dot(a, b):
    return jnp.dot(a, b, preferred_element_type=F32)


def _dot_nt(a, b):
    return lax.dot_general(a, b, NT_DIMS, preferred_element_type=F32)


def _dot_tn(a, b):
    return lax.dot_general(a, b, TN_DIMS, preferred_element_type=F32)


def _split3(a):
    hi = a.astype(BF16)
    r1 = a - hi.astype(F32)
    mid = r1.astype(BF16)
    lo = (r1 - mid.astype(F32)).astype(BF16)
    return hi, mid, lo


def _dot_exact_rhs(a, b_bf16):
    hi, mid, lo = _split3(a)
    return _dot(hi, b_bf16) + _dot(mid, b_bf16) + _dot(lo, b_bf16)


def _dot_exact_lhs(a_bf16, b):
    hi, mid, lo = _split3(b)
    return _dot(a_bf16, hi) + _dot(a_bf16, mid) + _dot(a_bf16, lo)


def _softplus(x):
    return jnp.maximum(x, 0.0) + jnp.log1p(jnp.exp(-jnp.abs(x)))


def _silu(x):
    return x * jax.nn.sigmoid(x)


def _lane(shape):
    return lax.broadcasted_iota(jnp.int32, shape, len(shape) - 1)


def _row(shape):
    return lax.broadcasted_iota(jnp.int32, shape, len(shape) - 2)


def _ada_kernel(c_ref, w_ref, b_ref, o_ref):
    c = c_ref[...]
    a_hi, a_mid, a_lo = _split3(_silu(c))
    w_hi, w_mid, w_lo = _split3(w_ref[0])
    acc = (_dot(a_hi, w_hi) + _dot(a_hi, w_mid) + _dot(a_mid, w_hi)
           + _dot(a_hi, w_lo) + _dot(a_lo, w_hi) + _dot(a_mid, w_mid))
    o_ref[0] = acc + b_ref[0]


def ada_mod(c, ada_w, ada_b):
    depth, d, n = ada_w.shape
    b = c.shape[0]
    bp = 8
    cp = jnp.zeros((bp, d), F32).at[:b].set(c)
    tn = 1536
    out = pl.pallas_call(
        _ada_kernel,
        grid=(depth, n // tn),
        in_specs=[pl.BlockSpec((bp, d), lambda l, j: (0, 0)),
                  pl.BlockSpec((1, d, tn), lambda l, j: (l, 0, j)),
                  pl.BlockSpec((1, 1, tn), lambda l, j: (l, 0, j))],
        out_specs=pl.BlockSpec((1, bp, tn), lambda l, j: (l, 0, j)),
        out_shape=jax.ShapeDtypeStruct((depth, bp, n), F32),
        compiler_params=_params("arbitrary", "arbitrary"),
        name="ada_mod",
    )(cp, ada_w, ada_b.reshape(depth, 1, n))
    return out[:, :b]


def _inproj_kernel(x_ref, g_ref, sh_ref, sc_ref, w_ref, o_ref, *, tn):
    x = x_ref[...]
    ms = jnp.mean(x * x, axis=-1, keepdims=True)
    y = x * lax.rsqrt(ms + EPS) * g_ref[...]
    hm = (y * (1.0 + sc_ref[0]) + sh_ref[0]).astype(BF16)
    for j in range(w_ref.shape[1] // tn):
        o_ref[:, j * tn:(j + 1) * tn] = _dot(hm, w_ref[:, j * tn:(j + 1) * tn]).astype(o_ref.dtype)


def inproj(h2d, gain, shift, scale, w, seq, tm, tn):
    t, d = h2d.shape
    n = w.shape[1]
    tpb = seq // tm
    return pl.pallas_call(
        functools.partial(_inproj_kernel, tn=tn),
        grid=(t // tm,),
        in_specs=[pl.BlockSpec((tm, d), lambda i: (i, 0)),
                  pl.BlockSpec((1, d), lambda i: (0, 0)),
                  pl.BlockSpec((1, 1, d), lambda i: (i // tpb, 0, 0)),
                  pl.BlockSpec((1, 1, d), lambda i: (i // tpb, 0, 0)),
                  pl.BlockSpec((d, n), lambda i: (0, 0), pipeline_mode=pl.Buffered(1))],
        out_specs=pl.BlockSpec((tm, n), lambda i: (i, 0)),
        out_shape=jax.ShapeDtypeStruct((t, n), BF16),
        compiler_params=_params("arbitrary"),
        name="inproj",
    )(h2d, gain.reshape(1, d), shift[:, None, :], scale[:, None, :], w)


def _gla_body(q_ref, k_ref, v_ref, og_ref, misc_ref, wg_ref, bg_ref, on_ref, tri_ref,
              o_ref, st_ref, cum_ref):
    tm = q_ref.shape[0]
    g = _dot(misc_ref[...], wg_ref[...]) + bg_ref[...]
    la = (jnp.minimum(g, 0.0) - jnp.log1p(jnp.exp(-jnp.abs(g)))) * (1.0 / GLA_GATE_NORM)
    la = jnp.where((_lane(la.shape) & (LANES - 1)) < GLA_DK, la, 0.0)
    cum_ref[...] = _dot_exact_lhs(tri_ref[...], la)

    causal = _row((CHUNK, CHUNK)) >= _lane((CHUNK, CHUNK))
    for c in range(tm // CHUNK):
        rows = pl.ds(c * CHUNK, CHUNK)
        for h in range(GLA_HEADS):
            cols = pl.ds(h * LANES, LANES)
            cu = cum_ref[rows, cols]
            last = cu[CHUNK - 1:CHUNK, :]
            qh = q_ref[rows, cols].astype(F32) * (GLA_DK ** -0.5)
            kh = k_ref[rows, cols].astype(F32)
            q_dec = (qh * jnp.exp(cu)).astype(BF16)
            k_inv = (kh * jnp.exp(-cu)).astype(BF16)
            k_end = (kh * jnp.exp(last - cu)).astype(BF16)
            vh = v_ref[rows, cols]
            att = jnp.where(causal, _dot_nt(q_dec, k_inv), 0.0)
            st = st_ref[h]
            o = _dot(att.astype(BF16), vh) + _dot_nt(q_dec, st.astype(BF16))
            st_ref[h] = st * jnp.exp(last) + _dot_tn(vh, k_end)
            ms = jnp.mean(o * o, axis=-1, keepdims=True)
            on = o * lax.rsqrt(ms + EPS) * on_ref[:, cols]
            o_ref[rows, cols] = (on * _silu(og_ref[rows, cols].astype(F32))).astype(o_ref.dtype)


N_GLA_IN, N_SSD_IN = 9, 15
N_GLA_SCRATCH = 2


def _gla_ssd_kernel(*refs):
    gla_in = refs[:N_GLA_IN]
    ssd_in = refs[N_GLA_IN:N_GLA_IN + N_SSD_IN]
    o_gla, o_ssd = refs[N_GLA_IN + N_SSD_IN:N_GLA_IN + N_SSD_IN + 2]
    scratch = refs[N_GLA_IN + N_SSD_IN + 2:]
    gla_scratch, ssd_scratch = scratch[:N_GLA_SCRATCH], scratch[N_GLA_SCRATCH:]

    @pl.when(pl.program_id(1) == 0)
    def _():
        gla_scratch[0][...] = jnp.zeros_like(gla_scratch[0])
        ssd_scratch[2][...] = jnp.zeros_like(ssd_scratch[2])
        _conv_reset(ssd_scratch[0])
        _conv_reset(ssd_scratch[1])

    _gla_body(*gla_in, o_gla, *gla_scratch)
    _ssd_body(*ssd_in, o_ssd, *ssd_scratch)


def gla_ssd(u, cb, wg, bg, onorm, cwx, cbx, cwb, cbb, dtb, alog, dexp, ng, expand, tri,
            batch, seq, tm):
    t = u.shape[0]
    tpb = seq // tm
    w512 = GLA_HEADS * LANES
    bcw = 2 * SSD_GROUPS * SSD_STATE
    row = lambda b, i: b * tpb + i
    tok = lambda width, name: pl.BlockSpec((tm, width), lambda b, i: (row(b, i), cb[name]))
    const = lambda shape: pl.BlockSpec(shape, lambda b, i: tuple(0 for _ in shape))
    out = lambda width: pl.BlockSpec((tm, width), lambda b, i: (row(b, i), 0))
    return pl.pallas_call(
        _gla_ssd_kernel,
        grid=(batch, tpb),
        in_specs=[tok(w512, "q"), tok(w512, "k"), tok(w512, "v"), tok(w512, "og"), tok(LANES, "glr"),
                  const((LANES, w512)), const((1, w512)), const((1, w512)), const((tm, tm)),
                  tok(SSD_INNER, "xs"), tok(SSD_INNER, "z"), tok(bcw, "bc"),
                  tok(LANES, "dte"), tok(LANES, "dto"),
                  const((CONV_K, SSD_INNER)), const((1, SSD_INNER)),
                  const((CONV_K, bcw)), const((1, bcw)),
                  const((2, LANES)), const((2, LANES)),
                  const((1, SSD_INNER)), const((1, SSD_INNER)),
                  const((2, LANES, SSD_INNER)), const((tm, tm))],
        out_specs=[out(w512), out(SSD_INNER)],
        out_shape=[jax.ShapeDtypeStruct((t, w512), BF16),
                   jax.ShapeDtypeStruct((t, SSD_INNER), BF16)],
        scratch_shapes=[pltpu.VMEM((GLA_HEADS, GLA_DV, LANES), F32),
                        pltpu.VMEM((tm, w512), F32),
                        pltpu.VMEM((tm + 8, SSD_INNER), F32),
                        pltpu.VMEM((tm + 8, bcw), F32),
                        pltpu.VMEM((SSD_GROUPS, SSD_STATE, SSD_GINNER), F32),
                        pltpu.VMEM((tm, LANES), F32),
                        pltpu.VMEM((tm, LANES), F32),
                        pltpu.VMEM((tm, SSD_INNER), F32),
                        pltpu.VMEM((tm, SSD_INNER), F32),
                        pltpu.VMEM((tm, bcw), BF16),
                        pltpu.VMEM((tm, SSD_INNER), F32)],
        compiler_params=_params("arbitrary", "arbitrary"),
        name="gla_ssd",
    )(u, u, u, u, u, wg, bg, onorm, tri,
      u, u, u, u, u, cwx, cbx, cwb, cbb, dtb, alog, dexp, ng, expand, tri)


def _conv_reset(ext_ref):
    ext_ref[0:8, :] = jnp.zeros((8, ext_ref.shape[1]), F32)


def _causal_conv(ext_ref, x, w_ref, b_ref):
    tm = x.shape[0]
    ext_ref[8:8 + tm, :] = x
    y = b_ref[...] + w_ref[CONV_K - 1:CONV_K, :] * x
    for kk in range(CONV_K - 1):
        off = 8 - (CONV_K - 1) + kk
        y = y + w_ref[kk:kk + 1, :] * ext_ref[off:off + tm, :]
    ext_ref[0:8, :] = ext_ref[tm:tm + 8, :]
    return y


def _ssd_body(xs_ref, z_ref, bc_ref, dte_ref, dto_ref,
              cwx_ref, cbx_ref, cwb_ref, cbb_ref, dtb_ref, alog_ref, dexp_ref, ng_ref,
              expand_ref, tri_ref, o_ref,
              extx_ref, extb_ref, st_ref, cume_ref, cumo_ref, cumx_ref, xdt_ref, bcc_ref, xsc_ref):
    tm = xs_ref.shape[0]
    xs = _silu(_causal_conv(extx_ref, xs_ref[...].astype(F32), cwx_ref, cbx_ref))
    bcv = _silu(_causal_conv(extb_ref, bc_ref[...].astype(F32), cwb_ref, cbb_ref))
    xsc_ref[...] = xs
    bcc_ref[...] = bcv.astype(BF16)

    npair = SSD_HEADS // 2
    lane = _lane((tm, LANES))
    valid = lane < npair
    a_e = -jnp.exp(alog_ref[0:1, :])
    a_o = -jnp.exp(alog_ref[1:2, :])
    dt_e = jnp.where(valid, _softplus(dte_ref[...].astype(F32) + dtb_ref[0:1, :]), 0.0)
    dt_o = jnp.where(valid, _softplus(dto_ref[...].astype(F32) + dtb_ref[1:2, :]), 0.0)
    tri = tri_ref[...]
    cum_e = _dot_exact_lhs(tri, dt_e * a_e)
    cum_o = _dot_exact_lhs(tri, dt_o * a_o)
    cume_ref[...] = cum_e
    cumo_ref[...] = cum_o
    ex_e = expand_ref[0]
    ex_o = expand_ref[1]
    cumx_ref[...] = _dot_exact_rhs(cum_e, ex_e) + _dot_exact_rhs(cum_o, ex_o)
    xdt_ref[...] = xs * (_dot_exact_rhs(dt_e, ex_e) + _dot_exact_rhs(dt_o, ex_o))

    rr = _row((CHUNK, LANES))
    ll = _lane((CHUNK, LANES))
    causal2 = rr >= (ll & (CHUNK - 1))
    left = ll < CHUNK
    gw = SSD_GINNER
    for c in range(tm // CHUNK):
        rows = pl.ds(c * CHUNK, CHUNK)
        cumx = cumx_ref[rows, :]
        lastx = cumx[CHUNK - 1:CHUNK, :]
        xdt = xdt_ref[rows, :]
        xw = (xdt * jnp.exp(lastx - cumx)).astype(BF16)
        ecum = jnp.exp(cumx)
        pt = jnp.concatenate([cume_ref[rows, :], cumo_ref[rows, :]], axis=0).T
        ys = []
        for g in range(SSD_GROUPS):
            bg = bcc_ref[rows, pl.ds(g * SSD_STATE, SSD_STATE)]
            cg = bcc_ref[rows, pl.ds((SSD_GROUPS + g) * SSD_STATE, SSD_STATE)]
            st = st_ref[g]
            y_off = _dot(cg, st.astype(BF16)) * ecum[:, g * gw:(g + 1) * gw]
            cbcb = _dot_nt(cg, jnp.concatenate([bg, bg], axis=0))
            parts = []
            for j in range(npair // SSD_GROUPS):
                jp = g * (npair // SSD_GROUPS) + j
                colp = cumx[:, jp * LANES:(jp + 1) * LANES]
                seg = colp - pt[jp:jp + 1, :]
                dec = jnp.exp(jnp.where(causal2, seg, -jnp.inf))
                w = (cbcb * dec).astype(BF16)
                xp = xdt[:, jp * LANES:(jp + 1) * LANES]
                x2 = jnp.concatenate([jnp.where(left, xp, 0.0), jnp.where(left, 0.0, xp)],
                                     axis=0).astype(BF16)
                parts.append(_dot(w, x2))
            ys.append(jnp.concatenate(parts, axis=1) + y_off)
            st_ref[g] = (st * jnp.exp(lastx[:, g * gw:(g + 1) * gw])
                         + _dot_tn(bg, xw[:, g * gw:(g + 1) * gw]))
        y = jnp.concatenate(ys, axis=1)
        y = y + dexp_ref[...] * xsc_ref[rows, :]
        y = y * _silu(z_ref[rows, :].astype(F32))
        for g in range(SSD_GROUPS):
            yg = y[:, g * gw:(g + 1) * gw]
            ms = jnp.mean(yg * yg, axis=-1, keepdims=True)
            o_ref[rows, pl.ds(g * gw, gw)] = (
                yg * lax.rsqrt(ms + EPS) * ng_ref[:, g * gw:(g + 1) * gw]).astype(o_ref.dtype)


def _gelu_tanh(x):
    return 0.5 * x * (1.0 + jnp.tanh(math.sqrt(2.0 / math.pi) * (x + 0.044715 * (x * x * x))))


def _lru_body(gate_ref, xr_ref, cw_ref, cb_ref, wa_ref, ba_ref, wx_ref, bx_ref, lam_ref,
              o_ref, ext_ref, carry_ref):
    tm, width = xr_ref.shape
    x = _causal_conv(ext_ref, xr_ref[...].astype(F32), cw_ref, cb_ref)
    xb = x.astype(BF16)
    nblk = wa_ref.shape[0]
    bw = width // nblk
    ra = jnp.concatenate([_dot(xb[:, n * bw:(n + 1) * bw], wa_ref[n]) for n in range(nblk)], axis=1)
    rx = jnp.concatenate([_dot(xb[:, n * bw:(n + 1) * bw], wx_ref[n]) for n in range(nblk)], axis=1)
    r = jax.nn.sigmoid(ra + ba_ref[...])
    ig = jax.nn.sigmoid(rx + bx_ref[...])
    log_a = (-LRU_C) * r * _softplus(-lam_ref[...])
    a = jnp.exp(log_a)
    u = jnp.sqrt(1.0 - a * a) * (ig * x)

    ngrp = tm // SUBLANES
    a3 = a.reshape(ngrp, SUBLANES, width)
    u3 = u.reshape(ngrp, SUBLANES, width)
    sub = lax.broadcasted_iota(jnp.int32, a3.shape, 1)
    d = 1
    while d < SUBLANES:
        m = sub >= d
        a_s = pltpu.roll(a3, d, 1)
        u_s = pltpu.roll(u3, d, 1)
        u3 = jnp.where(m, a3 * u_s + u3, u3)
        a3 = jnp.where(m, a3 * a_s, a3)
        d *= 2
    carry = carry_ref[0:1, :]
    groups = []
    for j in range(ngrp):
        hj = a3[j] * carry + u3[j]
        groups.append(hj)
        carry = hj[SUBLANES - 1:SUBLANES, :]
    hseq = jnp.concatenate(groups, axis=0)
    carry_ref[...] = jnp.broadcast_to(carry, carry_ref.shape)
    o_ref[...] = (hseq * _gelu_tanh(gate_ref[...].astype(F32))).astype(o_ref.dtype)


def _mla_prep_kernel(uq_ref, ukv_ref, krr_ref, pos_ref, freq_ref, qn_ref, kvn_ref,
                     wq_ref, wkt_ref, wv_ref, qg_ref, kgt_ref, q_ref, kt_ref, v_ref):
    tm = uq_ref.shape[0]
    nrep = tm // LANES
    rep = lambda a: jnp.concatenate([a] * nrep, axis=1)
    lane = _lane((tm, LANES))
    lo_half = lane < MLA_ROPE

    ang_t = rep(freq_ref[...]) * pos_ref[...].astype(F32)
    cos_t = jnp.cos(ang_t)
    sin_t = jnp.sin(ang_t)
    cs_t = jnp.concatenate([cos_t, cos_t, -sin_t, sin_t], axis=0)
    cs = cs_t.T

    def latent_norm(ref, g_ref):
        x = ref[...].astype(F32)
        ms = jnp.mean(x * x, axis=-1, keepdims=True)
        return (x * lax.rsqrt(ms + EPS) * g_ref[...]).astype(BF16)

    qf = _dot(latent_norm(uq_ref, qn_ref), wq_ref[...])
    ukv_n = latent_norm(ukv_ref, kvn_ref)
    vf = _dot(ukv_n, wv_ref[...])
    kn_t = _dot_nt(wkt_ref[...], ukv_n)
    kr_t = krr_ref[...].astype(F32).T
    kr_lo = kr_t[0:MLA_ROPE]
    kr_ss = jnp.sum(kr_lo * kr_lo, axis=0, keepdims=True)
    scale = MLA_QK ** -0.5 * math.log2(math.e)

    def rope_half(y2):
        t = y2 * cs
        return jnp.where(lo_half, t + pltpu.roll(t, MLA_ROPE, 1), 0.0)

    g_nope_t = rep(kgt_ref[0:MLA_NOPE, :])
    t = kr_t * rep(kgt_ref[MLA_NOPE:MLA_NOPE + 2 * MLA_ROPE, :]) * cs_t
    kr_rope_t = t[0:MLA_ROPE] + t[MLA_ROPE:2 * MLA_ROPE]
    pad_rows = MLA_QK_PAD - MLA_QK
    zeros_t = jnp.zeros((pad_rows, tm), kt_ref.dtype)

    for h in range(MLA_HEADS):
        x1 = qf[:, h * MLA_QK_PAD:h * MLA_QK_PAD + LANES]
        x2 = qf[:, h * MLA_QK_PAD + LANES:(h + 1) * MLA_QK_PAD]
        ss = jnp.sum(x1 * x1 + jnp.where(lo_half, x2 * x2, 0.0), axis=-1, keepdims=True)
        r = lax.rsqrt(ss * (1.0 / MLA_QK) + EPS) * scale
        q_ref[:, pl.ds(h * MLA_QK_PAD, LANES)] = (x1 * r * qg_ref[:, 0:LANES]).astype(q_ref.dtype)
        q_ref[:, pl.ds(h * MLA_QK_PAD + LANES, LANES)] = rope_half(
            x2 * r * qg_ref[:, LANES:2 * LANES]).astype(q_ref.dtype)

        kn = kn_t[h * MLA_NOPE:(h + 1) * MLA_NOPE]
        ssk = jnp.sum(kn * kn, axis=0, keepdims=True) + kr_ss
        rk = lax.rsqrt(ssk * (1.0 / MLA_QK) + EPS)
        base = h * MLA_QK_PAD
        kt_ref[0, 0, pl.ds(base, MLA_NOPE), :] = (kn * rk * g_nope_t).astype(kt_ref.dtype)
        kt_ref[0, 0, pl.ds(base + MLA_NOPE, MLA_ROPE), :] = (kr_rope_t * rk).astype(kt_ref.dtype)
        kt_ref[0, 0, pl.ds(base + MLA_QK, pad_rows), :] = zeros_t
        v_ref[:, pl.ds(h * MLA_V, MLA_V)] = vf[:, h * MLA_V:(h + 1) * MLA_V].astype(v_ref.dtype)


N_LRU_IN, N_MLA_IN = 9, 12


def _lru_mla_kernel(*refs):
    lru_in = refs[:N_LRU_IN]
    mla_in = refs[N_LRU_IN:N_LRU_IN + N_MLA_IN]
    o_lru, q_ref, kt_ref, v_ref, ext_ref, carry_ref = refs[N_LRU_IN + N_MLA_IN:]

    @pl.when(pl.program_id(1) == 0)
    def _():
        carry_ref[...] = jnp.zeros_like(carry_ref)
        _conv_reset(ext_ref)

    _lru_body(*lru_in, o_lru, ext_ref, carry_ref)
    _mla_prep_kernel(*mla_in, q_ref, kt_ref, v_ref)


def lru_mla(u, cb, cw, cbias, wa, ba, wx, bx, lam,
            pos_row, freq_col, qn, kvn, wq, wkt, wv, qg, kgt, batch, seq, tm):
    t = u.shape[0]
    width = cw.shape[1]
    tpb = seq // tm
    row = lambda b, i: b * tpb + i
    tok = lambda w, name: pl.BlockSpec((tm, w), lambda b, i: (row(b, i), cb[name]))
    const = lambda shape: pl.BlockSpec(shape, lambda b, i: tuple(0 for _ in shape))
    out = lambda w: pl.BlockSpec((tm, w), lambda b, i: (row(b, i), 0))
    hq = MLA_HEADS * MLA_QK_PAD
    hv = MLA_HEADS * MLA_V
    return pl.pallas_call(
        _lru_mla_kernel,
        grid=(batch, tpb),
        in_specs=[tok(width, "gate"), tok(width, "xr"),
                  const((CONV_K, width)), const((1, width)),
                  const(wa.shape), const((1, width)),
                  const(wx.shape), const((1, width)), const((1, width)),
                  tok(MLA_Q_LORA, "uq"), tok(MLA_KV_LORA, "ukv"), tok(LANES, "krr"),
                  pl.BlockSpec((1, tm), lambda b, i: (0, row(b, i))),
                  const(freq_col.shape), const((1, MLA_Q_LORA)), const((1, MLA_KV_LORA)),
                  const(wq.shape), const(wkt.shape), const(wv.shape),
                  const((1, 2 * LANES)), const(kgt.shape)],
        out_specs=[out(width), out(hq),
                   pl.BlockSpec((1, 1, hq, tm), lambda b, i: (b, i, 0, 0)),
                   out(hv)],
        out_shape=[jax.ShapeDtypeStruct((t, width), BF16),
                   jax.ShapeDtypeStruct((t, hq), BF16),
                   jax.ShapeDtypeStruct((batch, tpb, hq, tm), BF16),
                   jax.ShapeDtypeStruct((t, hv), BF16)],
        scratch_shapes=[pltpu.VMEM((tm + 8, width), F32),
                        pltpu.VMEM((8, width), F32)],
        compiler_params=_params("arbitrary", "arbitrary"),
        name="lru_mla",
    )(u, u, cw, cbias, wa, ba, wx, bx, lam,
      u, u, u, pos_row, freq_col, qn, kvn, wq, wkt, wv, qg, kgt)


def _attn_kernel(q_ref, k_ref, v_ref, o_ref, m_ref, acc_ref):
    i = pl.program_id(2)
    tq = q_ref.shape[0]
    nh = q_ref.shape[1] // MLA_QK_PAD
    nc = tq // LANES

    m_ref[...] = jnp.full_like(m_ref, -jnp.inf)
    acc_ref[...] = jnp.zeros_like(acc_ref)
    ones = jnp.ones((tq, LANES), BF16)

    def update(g, s, v):
        m_prev = m_ref[g]
        m_cur = s[:, 0:LANES]
        for c in range(1, nc):
            m_cur = jnp.maximum(m_cur, s[:, c * LANES:(c + 1) * LANES])
        m_new = jnp.maximum(m_prev, jnp.max(m_cur, axis=-1, keepdims=True))
        alpha = jnp.exp2(m_prev - m_new)
        p = jnp.concatenate(
            [jnp.exp2((s[:, c * LANES:(c + 1) * LANES] - m_new).astype(BF16)) for c in range(nc)],
            axis=1)
        pv = _dot(p, jnp.concatenate([v, ones], axis=1))
        acc_ref[g] = jnp.concatenate([alpha, alpha], axis=1) * acc_ref[g] + pv
        m_ref[g] = m_new

    def block(j, masked):
        rows = pl.ds(pl.multiple_of(j * tq, tq), tq)
        for g in range(nh):
            s = _dot(q_ref[:, g * MLA_QK_PAD:(g + 1) * MLA_QK_PAD],
                     k_ref[0, j, pl.ds(g * MLA_QK_PAD, MLA_QK_PAD), :])
            if masked:
                s = jnp.where(_row((tq, tq)) >= _lane((tq, tq)), s, -jnp.inf)
            update(g, s, v_ref[rows, pl.ds(g * MLA_V, MLA_V)])

    def body(jj, carry):
        for u in range(ATTN_KV_UNROLL):
            block(ATTN_KV_UNROLL * jj + u, False)
        return carry

    shift = ATTN_KV_UNROLL.bit_length() - 1
    ntrip = i >> shift
    lax.fori_loop(0, ntrip, body, 0)

    def rest(j, carry):
        block(j, False)
        return carry

    lax.fori_loop(ntrip << shift, i, rest, 0)

    block(i, True)
    for g in range(nh):
        acc = acc_ref[g]
        o_ref[:, pl.ds(g * MLA_V, MLA_V)] = (acc[:, :MLA_V] / acc[:, MLA_V:]).astype(o_ref.dtype)


def attention(q, k, v, batch, seq, tq):
    t = q.shape[0]
    nq = seq // tq
    nh = ATTN_HEADS_PER_STEP
    return pl.pallas_call(
        _attn_kernel,
        grid=(batch, MLA_HEADS // nh, nq),
        in_specs=[pl.BlockSpec((tq, nh * MLA_QK_PAD), lambda b, h, i: (b * nq + i, h)),
                  pl.BlockSpec((1, nq, nh * MLA_QK_PAD, tq), lambda b, h, i: (b, 0, h, 0),
                               pipeline_mode=pl.Buffered(1)),
                  pl.BlockSpec((seq, nh * MLA_V), lambda b, h, i: (b, h),
                               pipeline_mode=pl.Buffered(1))],
        out_specs=pl.BlockSpec((tq, nh * MLA_V), lambda b, h, i: (b * nq + i, h)),
        out_shape=jax.ShapeDtypeStruct((t, MLA_HEADS * MLA_V), BF16),
        scratch_shapes=[pltpu.VMEM((nh, tq, LANES), F32),
                        pltpu.VMEM((nh, tq, MLA_V + LANES), F32)],
        compiler_params=_params("arbitrary", "arbitrary", "arbitrary"),
        name="attention",
    )(q, k, v)


def _outproj_kernel(oa_ref, ob_ref, w1_ref, w2_ref, h_ref, g1_ref, nf_ref, sh_ref, sc_ref,
                    rwt_ref, hn_ref, hf_ref, lg_ref):
    y = _dot(oa_ref[...], w1_ref[...]) + _dot(ob_ref[...], w2_ref[...])
    hn = h_ref[...] + g1_ref[0] * y
    hn_ref[...] = hn
    ms = jnp.mean(hn * hn, axis=-1, keepdims=True)
    hf = hn * lax.rsqrt(ms + EPS) * nf_ref[...] * (1.0 + sc_ref[0]) + sh_ref[0]
    hf_ref[...] = hf.astype(BF16)
    r_hi, r_mid, r_lo = _split3(rwt_ref[...])
    f_hi, f_mid, _ = _split3(hf)
    ne = r_hi.shape[0]
    a = _dot_nt(jnp.concatenate([r_hi, r_mid, r_lo], axis=0), f_hi)
    b = _dot_nt(jnp.concatenate([r_hi, r_mid], axis=0), f_mid)
    lg_ref[...] = a[0:ne] + b[0:ne] + a[ne:2 * ne] + a[2 * ne:3 * ne] + b[ne:2 * ne]


def outproj(oa, ob, w1, w2, h2d, g1, nf, sh2, sc2, rwt, seq, tm):
    t, d = h2d.shape
    tpb = seq // tm
    const = lambda shape: pl.BlockSpec(shape, lambda i: tuple(0 for _ in shape))
    bvec = pl.BlockSpec((1, 1, d), lambda i: (i // tpb, 0, 0))
    return pl.pallas_call(
        _outproj_kernel,
        grid=(t // tm,),
        in_specs=[pl.BlockSpec((tm, oa.shape[1]), lambda i: (i, 0)),
                  pl.BlockSpec((tm, ob.shape[1]), lambda i: (i, 0)),
                  const(w1.shape), const(w2.shape),
                  pl.BlockSpec((tm, d), lambda i: (i, 0)),
                  bvec, const((1, d)), bvec, bvec, const(rwt.shape)],
        out_specs=[pl.BlockSpec((tm, d), lambda i: (i, 0)),
                   pl.BlockSpec((tm, d), lambda i: (i, 0)),
                   pl.BlockSpec((N_EXPERTS, tm), lambda i: (0, i))],
        out_shape=[jax.ShapeDtypeStruct((t, d), F32),
                   jax.ShapeDtypeStruct((t, d), BF16),
                   jax.ShapeDtypeStruct((N_EXPERTS, t), F32)],
        compiler_params=_params("arbitrary"),
        name="outproj",
    )(oa, ob, w1, w2, h2d, g1[:, None, :], nf.reshape(1, d), sh2[:, None, :], sc2[:, None, :], rwt)


def _route_sort_kernel(*refs):
    last = pl.program_id(0) == pl.num_programs(0) - 1

    @pl.when(last)
    def _():
        xs_ref = refs[5]
        xs_ref[...] = jnp.zeros_like(xs_ref)

    @pl.when(jnp.logical_not(last))
    def _():
        _route_sort_body(*refs)


def _route_sort_body(lg_ref, bias_ref, hf_ref, utri_ref, ltri_ref,
                     xs_ref, lpos_ref, wts_ref, cnt_ref):
    scores = jax.nn.sigmoid(lg_ref[...])
    sel = scores + bias_ref[...]
    tm = sel.shape[1]
    eidx = _row((N_EXPERTS, tm))
    neg = -jnp.inf

    best_score = None
    best_grp = None
    for g in range(N_GROUPS):
        m = [sel[g * EXPERTS_PER_GROUP + i:g * EXPERTS_PER_GROUP + i + 1, :]
             for i in range(EXPERTS_PER_GROUP)]
        gs = None
        for a in range(EXPERTS_PER_GROUP):
            for b in range(a + 1, EXPERTS_PER_GROUP):
                pair = m[a] + m[b]
                gs = pair if gs is None else jnp.maximum(gs, pair)
        if best_score is None:
            best_score, best_grp = gs, jnp.zeros_like(gs, dtype=jnp.int32)
        else:
            better = gs > best_score
            best_score = jnp.where(better, gs, best_score)
            best_grp = jnp.where(better, g, best_grp)

    masked = jnp.where((eidx >> 2) == best_grp, sel, neg)
    m1 = jnp.max(masked, axis=0, keepdims=True)
    i1 = jnp.min(jnp.where(masked == m1, eidx, N_EXPERTS), axis=0, keepdims=True)
    masked2 = jnp.where(eidx == i1, neg, masked)
    m2 = jnp.max(masked2, axis=0, keepdims=True)
    i2 = jnp.min(jnp.where(masked2 == m2, eidx, N_EXPERTS), axis=0, keepdims=True)
    pick1 = eidx == i1
    pick2 = eidx == i2
    s1 = jnp.sum(jnp.where(pick1, scores, 0.0), axis=0, keepdims=True)
    s2 = jnp.sum(jnp.where(pick2, scores, 0.0), axis=0, keepdims=True)
    tot = s1 + s2
    wts_ref[...] = jnp.concatenate([s1 / tot, s2 / tot], axis=0)

    nr = xs_ref.shape[0]
    picks = jnp.where(pick1, 1.0, jnp.where(pick2, 1.0, 0.0))
    csum = _dot(picks.astype(BF16), utri_ref[...])
    cnt = jnp.sum(picks, axis=1, keepdims=True)
    cnt_ref[0] = jnp.broadcast_to(cnt, (N_EXPERTS, LANES))
    cnt_pad = jnp.floor((cnt + (ROW_CHUNK - 1.0)) * (1.0 / ROW_CHUNK)) * ROW_CHUNK
    seg_off = _dot(ltri_ref[...], jnp.broadcast_to(cnt_pad, (N_EXPERTS, LANES)).astype(BF16))[:, 0:1]
    lposmat = seg_off + csum - 1.0
    lp1 = jnp.sum(jnp.where(pick1, lposmat, 0.0), axis=0, keepdims=True).astype(jnp.int32)
    lp2 = jnp.sum(jnp.where(pick2, lposmat, 0.0), axis=0, keepdims=True).astype(jnp.int32)
    lpos_ref[...] = jnp.concatenate([lp1, lp2], axis=0)
    rowi = _row((nr, tm))
    onehot = jnp.where(rowi == lp1, 1.0, jnp.where(rowi == lp2, 1.0, 0.0)).astype(BF16)
    xs_ref[...] = _dot(onehot, hf_ref[...]).astype(xs_ref.dtype)


def route_sort(logits_t, router_bias, hf, tl, nr):
    e, t = logits_t.shape
    d = hf.shape[1]
    ntile = t // tl
    r = jnp.arange(tl)
    utri = (r[:, None] <= r[None, :]).astype(BF16)
    re = jnp.arange(e)
    ltri = (re[None, :] < re[:, None]).astype(BF16)
    const = lambda shape: pl.BlockSpec(shape, lambda i: tuple(0 for _ in shape))
    tile = lambda i: jnp.minimum(i, ntile - 1)
    return pl.pallas_call(
        _route_sort_kernel,
        grid=(ntile + 1,),
        in_specs=[pl.BlockSpec((e, tl), lambda i: (0, tile(i))),
                  const((e, 1)),
                  pl.BlockSpec((tl, d), lambda i: (tile(i), 0)),
                  const((tl, tl)), const((e, e))],
        out_specs=[pl.BlockSpec((nr, d), lambda i: (i, 0)),
                   pl.BlockSpec((2, tl), lambda i: (0, tile(i))),
                   pl.BlockSpec((2, tl), lambda i: (0, tile(i))),
                   pl.BlockSpec((1, e, LANES), lambda i: (tile(i), 0, 0))],
        out_shape=[jax.ShapeDtypeStruct(((ntile + 1) * nr, d), BF16),
                   jax.ShapeDtypeStruct((2, t), jnp.int32),
                   jax.ShapeDtypeStruct((2, t), F32),
                   jax.ShapeDtypeStruct((t // tl, e, LANES), F32)],
        compiler_params=_params("arbitrary"),
        name="route_sort",
    )(logits_t, router_bias.reshape(e, 1).astype(F32), hf, utri, ltri)


def _moe_plan(cnt, nr, tmg, nt_max):
    ntile, ne = cnt.shape
    cpt = tmg // ROW_CHUNK
    nch = (cnt + ROW_CHUNK - 1) // ROW_CHUNK
    seg_off = jnp.cumsum(nch, axis=1) - nch
    cum = jnp.cumsum(nch, axis=0)
    tot = cum[-1]
    padded = (tot + cpt - 1) // cpt * cpt
    gend = jnp.cumsum(padded)
    gstart = gend - padded
    start = (gstart[None, :] + cum - nch).T.reshape(-1)
    end = start + nch.T.reshape(-1)
    base = (jnp.arange(ntile, dtype=jnp.int32)[:, None] * (nr // ROW_CHUNK) + seg_off).T.reshape(-1)
    c = jnp.arange(nt_max * cpt, dtype=jnp.int32)[:, None]
    hit = (c >= start[None, :]) & (c < end[None, :])
    valid = jnp.any(hit, axis=1)
    src = jnp.sum(jnp.where(hit, base[None, :] - start[None, :] + c, 0), axis=1) * ROW_CHUNK
    src = jnp.where(valid, src, nr - ROW_CHUNK)
    cflat = c[:, 0]
    spare = ntile * nr + ((cflat // cpt) % 2) * tmg + (cflat % cpt) * ROW_CHUNK
    dst = jnp.where(valid, src, spare)
    first = jnp.arange(nt_max, dtype=jnp.int32) * cpt
    e_first = jnp.sum(first[:, None] >= gend[None, :], axis=1).astype(jnp.int32)
    tile_on = (e_first < ne).astype(jnp.int32)
    return src.astype(jnp.int32), dst.astype(jnp.int32), jnp.minimum(e_first, ne - 1), tile_on


def _moe_group_kernel(src_ref, dst_ref, te_ref, on_ref, xs_hbm, wg_ref, wu_ref, wd_ref,
                      y_hbm, xbuf, ybuf, wgb, wub, wdb, in_sem, out_sem):
    del xs_hbm
    g = pl.program_id(0)
    ng = pl.num_programs(0)
    slot = lax.rem(g, 2)
    tmg = xbuf.shape[1]
    cpt = tmg // ROW_CHUNK

    def in_copy(tile, sl, k):
        row = pl.multiple_of(src_ref[tile * cpt + k], ROW_CHUNK)
        return pltpu.make_async_copy(y_hbm.at[pl.ds(row, ROW_CHUNK), :],
                                     xbuf.at[sl, pl.ds(k * ROW_CHUNK, ROW_CHUNK), :], in_sem.at[sl])

    def out_copy(tile, sl, k):
        row = pl.multiple_of(dst_ref[tile * cpt + k], ROW_CHUNK)
        return pltpu.make_async_copy(ybuf.at[sl, pl.ds(k * ROW_CHUNK, ROW_CHUNK), :],
                                     y_hbm.at[pl.ds(row, ROW_CHUNK), :], out_sem.at[sl])

    def start_in(tile, sl):
        for k in range(cpt):
            in_copy(tile, sl, k).start()

    def wait_in(tile, sl):
        for k in range(cpt):
            in_copy(tile, sl, k).wait()

    def start_out(tile, sl):
        for k in range(cpt):
            out_copy(tile, sl, k).start()

    def wait_out(tile, sl):
        for k in range(cpt):
            out_copy(tile, sl, k).wait()

    @pl.when((g >= 2) & (on_ref[jnp.maximum(g - 2, 0)] == 1))
    def _():
        wait_out(g - 2, slot)

    @pl.when((g == 0) & (on_ref[0] == 1))
    def _():
        start_in(0, 0)

    @pl.when((g + 1 < ng) & (on_ref[jnp.minimum(g + 1, ng - 1)] == 1))
    def _():
        start_in(g + 1, 1 - slot)

    @pl.when(on_ref[g] == 1)
    def _():
        @pl.when((g == 0) | (te_ref[g] != te_ref[jnp.maximum(g - 1, 0)]))
        def _():
            wgb[...] = wg_ref[0, 0].astype(BF16)
            wub[...] = wu_ref[0, 0].astype(BF16)
            wdb[...] = wd_ref[0, 0].astype(BF16)

        wait_in(g, slot)
        x = xbuf[slot]
        hid = (_silu(_dot(x, wgb[...])) * _dot(x, wub[...])).astype(BF16)
        ybuf[slot] = _dot(hid, wdb[...]).astype(ybuf.dtype)
        start_out(g, slot)

    @pl.when(g == ng - 1)
    def _():
        @pl.when((ng >= 2) & (on_ref[jnp.maximum(g - 1, 0)] == 1))
        def _():
            wait_out(g - 1, 1 - slot)

        @pl.when(on_ref[g] == 1)
        def _():
            wait_out(g, slot)


def moe_group(xs, wg, wu, wd, layer, plan, tmg, nt_max):
    src, dst, tile_e, tile_on = plan
    rows_out, d = xs.shape
    wspec = lambda w: pl.BlockSpec((1, 1) + w.shape[2:], lambda g, s, v, te, on: (layer, te[g], 0, 0))
    grid_spec = pltpu.PrefetchScalarGridSpec(
        num_scalar_prefetch=4,
        grid=(nt_max,),
        in_specs=[pl.BlockSpec(memory_space=pl.ANY), wspec(wg), wspec(wu), wspec(wd)],
        out_specs=pl.BlockSpec(memory_space=pl.ANY),
        scratch_shapes=[pltpu.VMEM((2, tmg, d), BF16), pltpu.VMEM((2, tmg, d), BF16),
                        pltpu.VMEM(wg.shape[2:], BF16), pltpu.VMEM(wu.shape[2:], BF16),
                        pltpu.VMEM(wd.shape[2:], BF16),
                        pltpu.SemaphoreType.DMA((2,)), pltpu.SemaphoreType.DMA((2,))])
    return pl.pallas_call(
        _moe_group_kernel,
        grid_spec=grid_spec,
        out_shape=jax.ShapeDtypeStruct((rows_out, d), BF16),
        input_output_aliases={4: 0},
        compiler_params=_params("arbitrary"),
        name="moe_group",
    )(src, dst, tile_e, tile_on, xs, wg, wu, wd)


def _moe_combine_kernel(y_ref, lpos_ref, wts_ref, h_ref, g2_ref, o_ref):
    nr = y_ref.shape[0]
    tl = h_ref.shape[0]
    rowi = _row((nr, tl))
    wc = (jnp.where(rowi == lpos_ref[0:1, :], wts_ref[0:1, :], 0.0)
          + jnp.where(rowi == lpos_ref[1:2, :], wts_ref[1:2, :], 0.0)).astype(BF16)
    o_ref[...] = h_ref[...] + g2_ref[0] * _dot_tn(wc, y_ref[...])


def moe_combine(y, lpos, wts, h2d, g2, seq, tl, nr):
    t, d = h2d.shape
    tpb = seq // tl
    return pl.pallas_call(
        _moe_combine_kernel,
        grid=(t // tl,),
        in_specs=[pl.BlockSpec((nr, d), lambda i: (i, 0)),
                  pl.BlockSpec((2, tl), lambda i: (0, i)),
                  pl.BlockSpec((2, tl), lambda i: (0, i)),
                  pl.BlockSpec((tl, d), lambda i: (i, 0)),
                  pl.BlockSpec((1, 1, d), lambda i: (i // tpb, 0, 0))],
        out_specs=pl.BlockSpec((tl, d), lambda i: (i, 0)),
        out_shape=jax.ShapeDtypeStruct((t, d), F32),
        compiler_params=_params("arbitrary"),
        name="moe_combine",
    )(y, lpos, wts, h2d, g2[:, None, :])


def _pad_heads(w, heads, dh):
    lead = w.shape[:-1]
    w = w.reshape(lead + (heads, dh))
    w = jnp.pad(w, [(0, 0)] * len(lead) + [(0, 0), (0, LANES - dh)])
    return w.reshape(lead + (heads * LANES,))


def _pad_cols(w, n):
    return jnp.pad(w, [(0, 0)] * (w.ndim - 1) + [(0, n - w.shape[-1])])


def _tri_blocks(tm):
    r = jnp.arange(tm)
    return ((r[:, None] >= r[None, :]) & (r[:, None] // CHUNK == r[None, :] // CHUNK)).astype(BF16)


def _even_layout(w_in):
    sizes = (GLA_HEADS * GLA_DK, GLA_HEADS * GLA_DK, GLA_HEADS * GLA_DV, GLA_GATE_RANK,
             GLA_HEADS * GLA_DV, SSD_INNER, SSD_INNER + 2 * SSD_GROUPS * SSD_STATE, SSD_HEADS)
    offs = [0]
    for s in sizes:
        offs.append(offs[-1] + s)
    seg = lambda i: w_in[:, offs[i]:offs[i + 1]]
    q, k, v, glr, og, z, xbc, dt = (seg(i) for i in range(8))
    xs, bc = xbc[:, :SSD_INNER], xbc[:, SSD_INNER:]
    cols = [xs, z, v, og, _pad_heads(q, GLA_HEADS, GLA_DK), _pad_heads(k, GLA_HEADS, GLA_DK), bc,
            _pad_cols(glr, LANES), _pad_cols(dt[:, 0::2], LANES), _pad_cols(dt[:, 1::2], LANES)]
    w = jnp.concatenate(cols, axis=1).astype(BF16)
    cb = {"xs": 0, "z": 1, "v": 4, "og": 5, "q": 6, "k": 7, "bc": 8, "glr": 36, "dte": 37, "dto": 38}
    return w, cb


def _odd_layout(w_in):
    d = w_in.shape[0]
    o = [0, d, 2 * d, 2 * d + MLA_Q_LORA, 2 * d + MLA_Q_LORA + MLA_KV_LORA,
         2 * d + MLA_Q_LORA + MLA_KV_LORA + MLA_ROPE]
    gate, xr, uq, ukv, kr = (w_in[:, o[i]:o[i + 1]] for i in range(5))
    half = MLA_ROPE // 2
    kr_sw = jnp.concatenate([kr[:, half:], kr[:, :half]], axis=1)
    w = jnp.concatenate([gate, xr, ukv, uq, kr, kr_sw], axis=1).astype(BF16)
    cb = {"gate": 0, "xr": 1, "ukv": 8, "uq": 6, "krr": 21}
    return w, cb


def _swap_halves(x):
    half = x.shape[-1] // 2
    return jnp.concatenate([x[..., half:], x[..., :half]], axis=-1)


def _block_diag(w, per):
    nb, bw, _ = w.shape
    w = w.reshape(nb // per, per, bw, bw)
    eye = jnp.eye(per, dtype=w.dtype)
    out = jnp.einsum("gpij,pq->gpiqj", w, eye)
    return out.reshape(nb // per, per * bw, per * bw)


def kernel(x, c, positions, router_w, router_bias, ada_w, ada_b, norm_mix, norm_ffn, moe_w_gate, moe_w_up, moe_w_down, ev_w_in, gla_w_g2, gla_b_g2, gla_onorm, ssd_conv_w, ssd_conv_b, ssd_dt_bias, ssd_a_log, ssd_d, ssd_norm, ev_w_out, od_w_in, lru_conv_w, lru_conv_b, lru_w_a, lru_b_a, lru_w_x, lru_b_x, lru_lambda, mla_q_norm, mla_w_q_up, mla_kv_norm, mla_w_kv_up, mla_q_qknorm, mla_k_qknorm, od_w_out):
    batch, seq, d = x.shape
    t = batch * seq
    depth = ada_w.shape[0]
    tm_seq = min(TM_SEQ, seq)
    tm_mm = min(TM_MM, seq)
    tq = min(TQ_ATTN, seq)
    tl = min(TL_MOE, seq)
    nr = 2 * tl + N_EXPERTS * ROW_CHUNK
    tmg = TM_MOE
    chunks_max = 2 * t // ROW_CHUNK + (t // tl) * N_EXPERTS + N_EXPERTS * (tmg // ROW_CHUNK - 1)
    nt_max = -(-chunks_max // (tmg // ROW_CHUNK))

    mod = ada_mod(c, ada_w, ada_b)
    rwt = router_w.T.astype(F32)
    tri = _tri_blocks(tm_seq)
    h = x.reshape(t, d)

    for layer in range(depth):
        sh1, sc1, g1, sh2, sc2, g2 = (mod[layer, :, i * d:(i + 1) * d] for i in range(6))
        i = layer // 2
        if layer % 2 == 0:
            w_in, cb = _even_layout(ev_w_in[i])
            u = inproj(h, norm_mix[layer], sh1, sc1, w_in, seq, tm_mm, w_in.shape[1] // 3)
            wg = jnp.pad(_pad_heads(gla_w_g2[i], GLA_HEADS, GLA_DK),
                         ((0, LANES - GLA_GATE_RANK), (0, 0))).astype(BF16)
            bg = _pad_heads(gla_b_g2[i][None, :], GLA_HEADS, GLA_DK)
            cw, cbias = ssd_conv_w[i], ssd_conv_b[i][None, :]
            perm = lambda v: jnp.stack([_pad_cols(v[0::2], LANES), _pad_cols(v[1::2], LANES)])
            npair = SSD_HEADS // 2
            hp = jnp.arange(SSD_INNER) // SSD_HEADDIM
            ex_e = (jnp.arange(LANES)[:, None] * 2 == hp[None, :]) & (jnp.arange(LANES)[:, None] < npair)
            ex_o = (jnp.arange(LANES)[:, None] * 2 + 1 == hp[None, :]) & (jnp.arange(LANES)[:, None] < npair)
            expand = jnp.stack([ex_e, ex_o]).astype(BF16)
            oa, ob = gla_ssd(u, cb, wg, bg, gla_onorm[i][None, :],
                             cw[:, :SSD_INNER], cbias[:, :SSD_INNER], cw[:, SSD_INNER:],
                             cbias[:, SSD_INNER:], perm(ssd_dt_bias[i]), perm(ssd_a_log[i]),
                             jnp.repeat(ssd_d[i], SSD_HEADDIM)[None, :], ssd_norm[i][None, :],
                             expand, tri, batch, seq, tm_seq)
            w_out = ev_w_out[i].astype(BF16)
            w1, w2 = w_out[:GLA_HEADS * GLA_DV], w_out[GLA_HEADS * GLA_DV:]
        else:
            w_in, cb = _odd_layout(od_w_in[i])
            u = inproj(h, norm_mix[layer], sh1, sc1, w_in, seq, tm_mm, w_in.shape[1] // 2)
            per = 4
            wq =mla_w_q_up[i].reshape(MLA_Q_LORA, MLA_HEADS, MLA_QK)
            wq = jnp.concatenate([wq, _swap_halves(wq[..., MLA_NOPE:])], axis=-1)
            wq = wq.reshape(MLA_Q_LORA, MLA_HEADS * MLA_QK_PAD).astype(BF16)
            wkv = mla_w_kv_up[i].reshape(MLA_KV_LORA, MLA_HEADS, MLA_NOPE + MLA_V)
            wkt = wkv[..., :MLA_NOPE].reshape(MLA_KV_LORA, -1).T.astype(BF16)
            wv = wkv[..., MLA_NOPE:].reshape(MLA_KV_LORA, -1).astype(BF16)
            ext_gain = lambda gq: jnp.concatenate([gq, _swap_halves(gq[MLA_NOPE:])])
            fr = ROPE_THETA ** (-jnp.arange(0, MLA_ROPE, 2, dtype=F32) / MLA_ROPE)
            lanes_rep = lambda col: jnp.broadcast_to(col[:, None], (col.shape[0], LANES))
            assert tm_mm == tq
            oa, q, kt, v = lru_mla(
                u, cb, lru_conv_w[i], lru_conv_b[i][None, :],
                _block_diag(lru_w_a[i], per).astype(BF16), lru_b_a[i][None, :],
                _block_diag(lru_w_x[i], per).astype(BF16), lru_b_x[i][None, :],
                lru_lambda[i][None, :],
                positions.reshape(1, t), lanes_rep(fr), mla_q_norm[i][None, :],
                mla_kv_norm[i][None, :], wq, wkt, wv, ext_gain(mla_q_qknorm[i])[None, :],
                lanes_rep(ext_gain(mla_k_qknorm[i])), batch, seq, tm_mm)
            ob = attention(q, kt, v, batch, seq, tq)
            w_out = od_w_out[i].astype(BF16)
            w1, w2 = w_out[:d], w_out[d:]
        h, hf, logits_t = outproj(oa, ob, w1, w2, h, g1, norm_ffn[layer], sh2, sc2, rwt, seq, tm_mm)
        assert 2 * tmg <= nr
        xs, lpos, wts, cnt = route_sort(logits_t, router_bias, hf, tl, nr)
        plan = _moe_plan(cnt[:, :, 0].astype(jnp.int32), nr, tmg, nt_max)
        y = moe_group(xs, moe_w_gate, moe_w_up, moe_w_down, layer, plan, tmg, nt_max)
        h = moe_combine(y, lpos, wts, h, g2, seq, tl, nr)
    return h.reshape(batch, seq, d)
```

```python
import functools
import math

import jax
import jax.numpy as jnp
from jax import lax
from jax.experimental import pallas as pl
from jax.experimental.pallas import tpu as pltpu

F32 = jnp.float32
BF16 = jnp.bfloat16

EPS = 1e-6
CHUNK = 64
CONV_K = 4
GLA_HEADS, GLA_DK, GLA_DV = 4, 64, 128
GLA_GATE_RANK, GLA_GATE_NORM = 16, 16.0
SSD_HEADS, SSD_HEADDIM, SSD_GROUPS, SSD_STATE = 16, 64, 2, 128
SSD_INNER = SSD_HEADS * SSD_HEADDIM
SSD_GINNER = SSD_INNER // SSD_GROUPS
LRU_BLOCKS, LRU_C = 16, 8.0
MLA_HEADS, MLA_NOPE, MLA_ROPE, MLA_V = 8, 128, 64, 128
MLA_QK = MLA_NOPE + MLA_ROPE
MLA_QK_PAD = 256
MLA_Q_LORA, MLA_KV_LORA = 384, 256
ROPE_THETA = 10000.0
N_EXPERTS, N_GROUPS, D_EXPERT = 16, 4, 512
EXPERTS_PER_GROUP = N_EXPERTS // N_GROUPS

TM_SEQ = 256
TM_MM = 512
TM_MOE = 512
TL_MOE = 512
ROW_CHUNK = 16
TQ_ATTN = 512
ATTN_HEADS_PER_STEP = 4
ATTN_KV_UNROLL = 4

LANES = 128
SUBLANES = 8
VMEM_LIMIT = 48 * 1024 * 1024

NT_DIMS = (((1,), (1,)), ((), ()))
TN_DIMS = (((0,), (0,)), ((), ()))


def _params(*sem):
    return pltpu.CompilerParams(dimension_semantics=sem, vmem_limit_bytes=VMEM_LIMIT)


def _dot(a, b):
    return jnp.dot(a, b, preferred_element_type=F32)


def _dot_nt(a, b):
    return lax.dot_general(a, b, NT_DIMS, preferred_element_type=F32)


def _dot_tn(a, b):
    return lax.dot_general(a, b, TN_DIMS, preferred_element_type=F32)


def _split3(a):
    hi = a.astype(BF16)
    r1 = a - hi.astype(F32)
    mid = r1.astype(BF16)
    lo = (r1 - mid.astype(F32)).astype(BF16)
    return hi, mid, lo


def _dot_exact_rhs(a, b_bf16):
    hi, mid, lo = _split3(a)
    return _dot(hi, b_bf16) + _dot(mid, b_bf16) + _dot(lo, b_bf16)


def _dot_exact_lhs(a_bf16, b):
    hi, mid, lo = _split3(b)
    return _dot(a_bf16, hi) + _dot(a_bf16, mid) + _dot(a_bf16, lo)


def _softplus(x):
    return jnp.maximum(x, 0.0) + jnp.log1p(jnp.exp(-jnp.abs(x)))


def _silu(x):
    return x * jax.nn.sigmoid(x)


def _lane(shape):
    return lax.broadcasted_iota(jnp.int32, shape, len(shape) - 1)


def _row(shape):
    return lax.broadcasted_iota(jnp.int32, shape, len(shape) - 2)


def _ada_kernel(c_ref, w_ref, b_ref, o_ref):
    c = c_ref[...]
    a_hi, a_mid, a_lo = _split3(_silu(c))
    w_hi, w_mid, w_lo = _split3(w_ref[0])
    acc = (_dot(a_hi, w_hi) + _dot(a_hi, w_mid) + _dot(a_mid, w_hi)
           + _dot(a_hi, w_lo) + _dot(a_lo, w_hi) + _dot(a_mid, w_mid))
    o_ref[0] = acc + b_ref[0]


def ada_mod(c, ada_w, ada_b):
    depth, d, n = ada_w.shape
    b = c.shape[0]
    bp = 8
    cp = jnp.zeros((bp, d), F32).at[:b].set(c)
    tn = 1536
    out = pl.pallas_call(
        _ada_kernel,
        grid=(depth, n // tn),
        in_specs=[pl.BlockSpec((bp, d), lambda l, j: (0, 0)),
                  pl.BlockSpec((1, d, tn), lambda l, j: (l, 0, j)),
                  pl.BlockSpec((1, 1, tn), lambda l, j: (l, 0, j))],
        out_specs=pl.BlockSpec((1, bp, tn), lambda l, j: (l, 0, j)),
        out_shape=jax.ShapeDtypeStruct((depth, bp, n), F32),
        compiler_params=_params("arbitrary", "arbitrary"),
        name="ada_mod",
    )(cp, ada_w, ada_b.reshape(depth, 1, n))
    return out[:, :b]


def _inproj_kernel(x_ref, g_ref, sh_ref, sc_ref, w_ref, o_ref, *, tn):
    x = x_ref[...]
    ms = jnp.mean(x * x, axis=-1, keepdims=True)
    y = x * lax.rsqrt(ms + EPS) * g_ref[...]
    hm = (y * (1.0 + sc_ref[0]) + sh_ref[0]).astype(BF16)
    for j in range(w_ref.shape[1] // tn):
        o_ref[:, j * tn:(j + 1) * tn] = _dot(hm, w_ref[:, j * tn:(j + 1) * tn]).astype(o_ref.dtype)


def inproj(h2d, gain, shift, scale, w, seq, tm, tn):
    t, d = h2d.shape
    n = w.shape[1]
    tpb = seq // tm
    return pl.pallas_call(
        functools.partial(_inproj_kernel, tn=tn),
        grid=(t // tm,),
        in_specs=[pl.BlockSpec((tm, d), lambda i: (i, 0)),
                  pl.BlockSpec((1, d), lambda i: (0, 0)),
                  pl.BlockSpec((1, 1, d), lambda i: (i // tpb, 0, 0)),
                  pl.BlockSpec((1, 1, d), lambda i: (i // tpb, 0, 0)),
                  pl.BlockSpec((d, n), lambda i: (0, 0), pipeline_mode=pl.Buffered(1))],
        out_specs=pl.BlockSpec((tm, n), lambda i: (i, 0)),
        out_shape=jax.ShapeDtypeStruct((t, n), BF16),
        compiler_params=_params("arbitrary"),
        name="inproj",
    )(h2d, gain.reshape(1, d), shift[:, None, :], scale[:, None, :], w)


def _gla_body(q_ref, k_ref, v_ref, og_ref, misc_ref, wg_ref, bg_ref, on_ref, tri_ref,
              o_ref, st_ref, cum_ref):
    tm = q_ref.shape[0]
    g = _dot(misc_ref[...], wg_ref[...]) + bg_ref[...]
    la = (jnp.minimum(g, 0.0) - jnp.log1p(jnp.exp(-jnp.abs(g)))) * (1.0 / GLA_GATE_NORM)
    la = jnp.where((_lane(la.shape) & (LANES - 1)) < GLA_DK, la, 0.0)
    cum_ref[...] = _dot_exact_lhs(tri_ref[...], la)

    causal = _row((CHUNK, CHUNK)) >= _lane((CHUNK, CHUNK))
    for c in range(tm // CHUNK):
        rows = pl.ds(c * CHUNK, CHUNK)
        for h in range(GLA_HEADS):
            cols = pl.ds(h * LANES, LANES)
            cu = cum_ref[rows, cols]
            last = cu[CHUNK - 1:CHUNK, :]
            qh = q_ref[rows, cols].astype(F32) * (GLA_DK ** -0.5)
            kh = k_ref[rows, cols].astype(F32)
            q_dec = (qh * jnp.exp(cu)).astype(BF16)
            k_inv = (kh * jnp.exp(-cu)).astype(BF16)
            k_end = (kh * jnp.exp(last - cu)).astype(BF16)
            vh = v_ref[rows, cols]
            att = jnp.where(causal, _dot_nt(q_dec, k_inv), 0.0)
            st = st_ref[h]
            o = _dot(att.astype(BF16), vh) + _dot_nt(q_dec, st.astype(BF16))
            st_ref[h] = st * jnp.exp(last) + _dot_tn(vh, k_end)
            ms = jnp.mean(o * o, axis=-1, keepdims=True)
            on = o * lax.rsqrt(ms + EPS) * on_ref[:, cols]
            o_ref[rows, cols] = (on * _silu(og_ref[rows, cols].astype(F32))).astype(o_ref.dtype)


N_GLA_IN, N_SSD_IN = 9, 15
N_GLA_SCRATCH = 2


def _gla_ssd_kernel(*refs):
    gla_in = refs[:N_GLA_IN]
    ssd_in = refs[N_GLA_IN:N_GLA_IN + N_SSD_IN]
    o_gla, o_ssd = refs[N_GLA_IN + N_SSD_IN:N_GLA_IN + N_SSD_IN + 2]
    scratch = refs[N_GLA_IN + N_SSD_IN + 2:]
    gla_scratch, ssd_scratch = scratch[:N_GLA_SCRATCH], scratch[N_GLA_SCRATCH:]

    @pl.when(pl.program_id(1) == 0)
    def _():
        gla_scratch[0][...] = jnp.zeros_like(gla_scratch[0])
        ssd_scratch[2][...] = jnp.zeros_like(ssd_scratch[2])
        _conv_reset(ssd_scratch[0])
        _conv_reset(ssd_scratch[1])

    _gla_body(*gla_in, o_gla, *gla_scratch)
    _ssd_body(*ssd_in, o_ssd, *ssd_scratch)


def gla_ssd(u, cb, wg, bg, onorm, cwx, cbx, cwb, cbb, dtb, alog, dexp, ng, expand, tri,
            batch, seq, tm):
    t = u.shape[0]
    tpb = seq // tm
    w512 = GLA_HEADS * LANES
    bcw = 2 * SSD_GROUPS * SSD_STATE
    row = lambda b, i: b * tpb + i
    tok = lambda width, name: pl.BlockSpec((tm, width), lambda b, i: (row(b, i), cb[name]))
    const = lambda shape: pl.BlockSpec(shape, lambda b, i: tuple(0 for _ in shape))
    out = lambda width: pl.BlockSpec((tm, width), lambda b, i: (row(b, i), 0))
    return pl.pallas_call(
        _gla_ssd_kernel,
        grid=(batch, tpb),
        in_specs=[tok(w512, "q"), tok(w512, "k"), tok(w512, "v"), tok(w512, "og"), tok(LANES, "glr"),
                  const((LANES, w512)), const((1, w512)), const((1, w512)), const((tm, tm)),
                  tok(SSD_INNER, "xs"), tok(SSD_INNER, "z"), tok(bcw, "bc"),
                  tok(LANES, "dte"), tok(LANES, "dto"),
                  const((CONV_K, SSD_INNER)), const((1, SSD_INNER)),
                  const((CONV_K, bcw)), const((1, bcw)),
                  const((2, LANES)), const((2, LANES)),
                  const((1, SSD_INNER)), const((1, SSD_INNER)),
                  const((2, LANES, SSD_INNER)), const((tm, tm))],
        out_specs=[out(w512), out(SSD_INNER)],
        out_shape=[jax.ShapeDtypeStruct((t, w512), BF16),
                   jax.ShapeDtypeStruct((t, SSD_INNER), BF16)],
        scratch_shapes=[pltpu.VMEM((GLA_HEADS, GLA_DV, LANES), F32),
                        pltpu.VMEM((tm, w512), F32),
                        pltpu.VMEM((tm + 8, SSD_INNER), F32),
                        pltpu.VMEM((tm + 8, bcw), F32),
                        pltpu.VMEM((SSD_GROUPS, SSD_STATE, SSD_GINNER), F32),
                        pltpu.VMEM((tm, LANES), F32),
                        pltpu.VMEM((tm, LANES), F32),
                        pltpu.VMEM((tm, SSD_INNER), F32),
                        pltpu.VMEM((tm, SSD_INNER), F32),
                        pltpu.VMEM((tm, bcw), BF16),
                        pltpu.VMEM((tm, SSD_INNER), F32)],
        compiler_params=_params("arbitrary", "arbitrary"),
        name="gla_ssd",
    )(u, u, u, u, u, wg, bg, onorm, tri,
      u, u, u, u, u, cwx, cbx, cwb, cbb, dtb, alog, dexp, ng, expand, tri)


def _conv_reset(ext_ref):
    ext_ref[0:8, :] = jnp.zeros((8, ext_ref.shape[1]), F32)


def _causal_conv(ext_ref, x, w_ref, b_ref):
    tm = x.shape[0]
    ext_ref[8:8 + tm, :] = x
    y = b_ref[...] + w_ref[CONV_K - 1:CONV_K, :] * x
    for kk in range(CONV_K - 1):
        off = 8 - (CONV_K - 1) + kk
        y = y + w_ref[kk:kk + 1, :] * ext_ref[off:off + tm, :]
    ext_ref[0:8, :] = ext_ref[tm:tm + 8, :]
    return y


def _ssd_body(xs_ref, z_ref, bc_ref, dte_ref, dto_ref,
              cwx_ref, cbx_ref, cwb_ref, cbb_ref, dtb_ref, alog_ref, dexp_ref, ng_ref,
              expand_ref, tri_ref, o_ref,
              extx_ref, extb_ref, st_ref, cume_ref, cumo_ref, cumx_ref, xdt_ref, bcc_ref, xsc_ref):
    tm = xs_ref.shape[0]
    xs = _silu(_causal_conv(extx_ref, xs_ref[...].astype(F32), cwx_ref, cbx_ref))
    bcv = _silu(_causal_conv(extb_ref, bc_ref[...].astype(F32), cwb_ref, cbb_ref))
    xsc_ref[...] = xs
    bcc_ref[...] = bcv.astype(BF16)

    npair = SSD_HEADS // 2
    lane = _lane((tm, LANES))
    valid = lane < npair
    a_e = -jnp.exp(alog_ref[0:1, :])
    a_o = -jnp.exp(alog_ref[1:2, :])
    dt_e = jnp.where(valid, _softplus(dte_ref[...].astype(F32) + dtb_ref[0:1, :]), 0.0)
    dt_o = jnp.where(valid, _softplus(dto_ref[...].astype(F32) + dtb_ref[1:2, :]), 0.0)
    tri = tri_ref[...]
    cum_e = _dot_exact_lhs(tri, dt_e * a_e)
    cum_o = _dot_exact_lhs(tri, dt_o * a_o)
    cume_ref[...] = cum_e
    cumo_ref[...] = cum_o
    ex_e = expand_ref[0]
    ex_o = expand_ref[1]
    cumx_ref[...] = _dot_exact_rhs(cum_e, ex_e) + _dot_exact_rhs(cum_o, ex_o)
    xdt_ref[...] = xs * (_dot_exact_rhs(dt_e, ex_e) + _dot_exact_rhs(dt_o, ex_o))

    rr = _row((CHUNK, LANES))
    ll = _lane((CHUNK, LANES))
    causal2 = rr >= (ll & (CHUNK - 1))
    left = ll < CHUNK
    gw = SSD_GINNER
    for c in range(tm // CHUNK):
        rows = pl.ds(c * CHUNK, CHUNK)
        cumx = cumx_ref[rows, :]
        lastx = cumx[CHUNK - 1:CHUNK, :]
        xdt = xdt_ref[rows, :]
        xw = (xdt * jnp.exp(lastx - cumx)).astype(BF16)
        ecum = jnp.exp(cumx)
        pt = jnp.concatenate([cume_ref[rows, :], cumo_ref[rows, :]], axis=0).T
        ys = []
        for g in range(SSD_GROUPS):
            bg = bcc_ref[rows, pl.ds(g * SSD_STATE, SSD_STATE)]
            cg = bcc_ref[rows, pl.ds((SSD_GROUPS + g) * SSD_STATE, SSD_STATE)]
            st = st_ref[g]
            y_off = _dot(cg, st.astype(BF16)) * ecum[:, g * gw:(g + 1) * gw]
            cbcb = _dot_nt(cg, jnp.concatenate([bg, bg], axis=0))
            parts = []
            for j in range(npair // SSD_GROUPS):
                jp = g * (npair // SSD_GROUPS) + j
                colp = cumx[:, jp * LANES:(jp + 1) * LANES]
                seg = colp - pt[jp:jp + 1, :]
                dec = jnp.exp(jnp.where(causal2, seg, -jnp.inf))
                w = (cbcb * dec).astype(BF16)
                xp = xdt[:, jp * LANES:(jp + 1) * LANES]
                x2 = jnp.concatenate([jnp.where(left, xp, 0.0), jnp.where(left, 0.0, xp)],
                                     axis=0).astype(BF16)
                parts.append(_dot(w, x2))
            ys.append(jnp.concatenate(parts, axis=1) + y_off)
            st_ref[g] = (st * jnp.exp(lastx[:, g * gw:(g + 1) * gw])
                         + _dot_tn(bg, xw[:, g * gw:(g + 1) * gw]))
        y = jnp.concatenate(ys, axis=1)
        y = y + dexp_ref[...] * xsc_ref[rows, :]
        y = y * _silu(z_ref[rows, :].astype(F32))
        for g in range(SSD_GROUPS):
            yg = y[:, g * gw:(g + 1) * gw]
            ms = jnp.mean(yg * yg, axis=-1, keepdims=True)
            o_ref[rows, pl.ds(g * gw, gw)] = (
                yg * lax.rsqrt(ms + EPS) * ng_ref[:, g * gw:(g + 1) * gw]).astype(o_ref.dtype)


def _gelu_tanh(x):
    return 0.5 * x * (1.0 + jnp.tanh(math.sqrt(2.0 / math.pi) * (x + 0.044715 * (x * x * x))))


def _lru_body(gate_ref, xr_ref, cw_ref, cb_ref, wa_ref, ba_ref, wx_ref, bx_ref, lam_ref,
              o_ref, ext_ref, carry_ref):
    tm, width = xr_ref.shape
    x = _causal_conv(ext_ref, xr_ref[...].astype(F32), cw_ref, cb_ref)
    xb = x.astype(BF16)
    nblk = wa_ref.shape[0]
    bw = width // nblk
    ra = jnp.concatenate([_dot(xb[:, n * bw:(n + 1) * bw], wa_ref[n]) for n in range(nblk)], axis=1)
    rx = jnp.concatenate([_dot(xb[:, n * bw:(n + 1) * bw], wx_ref[n]) for n in range(nblk)], axis=1)
    r = jax.nn.sigmoid(ra + ba_ref[...])
    ig = jax.nn.sigmoid(rx + bx_ref[...])
    log_a = (-LRU_C) * r * _softplus(-lam_ref[...])
    a = jnp.exp(log_a)
    u = jnp.sqrt(1.0 - a * a) * (ig * x)

    ngrp = tm // SUBLANES
    a3 = a.reshape(ngrp, SUBLANES, width)
    u3 = u.reshape(ngrp, SUBLANES, width)
    sub = lax.broadcasted_iota(jnp.int32, a3.shape, 1)
    d = 1
    while d < SUBLANES:
        m = sub >= d
        a_s = pltpu.roll(a3, d, 1)
        u_s = pltpu.roll(u3, d, 1)
        u3 = jnp.where(m, a3 * u_s + u3, u3)
        a3 = jnp.where(m, a3 * a_s, a3)
        d *= 2
    carry = carry_ref[0:1, :]
    groups = []
    for j in range(ngrp):
        hj = a3[j] * carry + u3[j]
        groups.append(hj)
        carry = hj[SUBLANES - 1:SUBLANES, :]
    hseq = jnp.concatenate(groups, axis=0)
    carry_ref[...] = jnp.broadcast_to(carry, carry_ref.shape)
    o_ref[...] = (hseq * _gelu_tanh(gate_ref[...].astype(F32))).astype(o_ref.dtype)


def _mla_prep_kernel(uq_ref, ukv_ref, krr_ref, pos_ref, freq_ref, qn_ref, kvn_ref,
                     wq_ref, wkt_ref, wv_ref, qg_ref, kgt_ref, q_ref, kt_ref, v_ref):
    tm = uq_ref.shape[0]
    nrep = tm // LANES
    rep = lambda a: jnp.concatenate([a] * nrep, axis=1)
    lane = _lane((tm, LANES))
    lo_half = lane < MLA_ROPE

    ang_t = rep(freq_ref[...]) * pos_ref[...].astype(F32)
    cos_t = jnp.cos(ang_t)
    sin_t = jnp.sin(ang_t)
    cs_t = jnp.concatenate([cos_t, cos_t, -sin_t, sin_t], axis=0)
    cs = cs_t.T

    def latent_norm(ref, g_ref):
        x = ref[...].astype(F32)
        ms = jnp.mean(x * x, axis=-1, keepdims=True)
        return (x * lax.rsqrt(ms + EPS) * g_ref[...]).astype(BF16)

    qf = _dot(latent_norm(uq_ref, qn_ref), wq_ref[...])
    ukv_n = latent_norm(ukv_ref, kvn_ref)
    vf = _dot(ukv_n, wv_ref[...])
    kn_t = _dot_nt(wkt_ref[...], ukv_n)
    kr_t = krr_ref[...].astype(F32).T
    kr_lo = kr_t[0:MLA_ROPE]
    kr_ss = jnp.sum(kr_lo * kr_lo, axis=0, keepdims=True)
    scale = MLA_QK ** -0.5 * math.log2(math.e)

    def rope_half(y2):
        t = y2 * cs
        return jnp.where(lo_half, t + pltpu.roll(t, MLA_ROPE, 1), 0.0)

    g_nope_t = rep(kgt_ref[0:MLA_NOPE, :])
    t = kr_t * rep(kgt_ref[MLA_NOPE:MLA_NOPE + 2 * MLA_ROPE, :]) * cs_t
    kr_rope_t = t[0:MLA_ROPE] + t[MLA_ROPE:2 * MLA_ROPE]
    pad_rows = MLA_QK_PAD - MLA_QK
    zeros_t = jnp.zeros((pad_rows, tm), kt_ref.dtype)

    for h in range(MLA_HEADS):
        x1 = qf[:, h * MLA_QK_PAD:h * MLA_QK_PAD + LANES]
        x2 = qf[:, h * MLA_QK_PAD + LANES:(h + 1) * MLA_QK_PAD]
        ss = jnp.sum(x1 * x1 + jnp.where(lo_half, x2 * x2, 0.0), axis=-1, keepdims=True)
        r = lax.rsqrt(ss * (1.0 / MLA_QK) + EPS) * scale
        q_ref[:, pl.ds(h * MLA_QK_PAD, LANES)] = (x1 * r * qg_ref[:, 0:LANES]).astype(q_ref.dtype)
        q_ref[:, pl.ds(h * MLA_QK_PAD + LANES, LANES)] = rope_half(
            x2 * r * qg_ref[:, LANES:2 * LANES]).astype(q_ref.dtype)

        kn = kn_t[h * MLA_NOPE:(h + 1) * MLA_NOPE]
        ssk = jnp.sum(kn * kn, axis=0, keepdims=True) + kr_ss
        rk = lax.rsqrt(ssk * (1.0 / MLA_QK) + EPS)
        base = h * MLA_QK_PAD
        kt_ref[0, 0, pl.ds(base, MLA_NOPE), :] = (kn * rk * g_nope_t).astype(kt_ref.dtype)
        kt_ref[0, 0, pl.ds(base + MLA_NOPE, MLA_ROPE), :] = (kr_rope_t * rk).astype(kt_ref.dtype)
        kt_ref[0, 0, pl.ds(base + MLA_QK, pad_rows), :] = zeros_t
        v_ref[:, pl.ds(h * MLA_V, MLA_V)] = vf[:, h * MLA_V:(h + 1) * MLA_V].astype(v_ref.dtype)


N_LRU_IN, N_MLA_IN = 9, 12


def _lru_mla_kernel(*refs):
    lru_in = refs[:N_LRU_IN]
    mla_in = refs[N_LRU_IN:N_LRU_IN + N_MLA_IN]
    o_lru, q_ref, kt_ref, v_ref, ext_ref, carry_ref = refs[N_LRU_IN + N_MLA_IN:]

    @pl.when(pl.program_id(1) == 0)
    def _():
        carry_ref[...] = jnp.zeros_like(carry_ref)
        _conv_reset(ext_ref)

    _lru_body(*lru_in, o_lru, ext_ref, carry_ref)
    _mla_prep_kernel(*mla_in, q_ref, kt_ref, v_ref)


def lru_mla(u, cb, cw, cbias, wa, ba, wx, bx, lam,
            pos_row, freq_col, qn, kvn, wq, wkt, wv, qg, kgt, batch, seq, tm):
    t = u.shape[0]
    width = cw.shape[1]
    tpb = seq // tm
    row = lambda b, i: b * tpb + i
    tok = lambda w, name: pl.BlockSpec((tm, w), lambda b, i: (row(b, i), cb[name]))
    const = lambda shape: pl.BlockSpec(shape, lambda b, i: tuple(0 for _ in shape))
    out = lambda w: pl.BlockSpec((tm, w), lambda b, i: (row(b, i), 0))
    hq = MLA_HEADS * MLA_QK_PAD
    hv = MLA_HEADS * MLA_V
    return pl.pallas_call(
        _lru_mla_kernel,
        grid=(batch, tpb),
        in_specs=[tok(width, "gate"), tok(width, "xr"),
                  const((CONV_K, width)), const((1, width)),
                  const(wa.shape), const((1, width)),
                  const(wx.shape), const((1, width)), const((1, width)),
                  tok(MLA_Q_LORA, "uq"), tok(MLA_KV_LORA, "ukv"), tok(LANES, "krr"),
                  pl.BlockSpec((1, tm), lambda b, i: (0, row(b, i))),
                  const(freq_col.shape), const((1, MLA_Q_LORA)), const((1, MLA_KV_LORA)),
                  const(wq.shape), const(wkt.shape), const(wv.shape),
                  const((1, 2 * LANES)), const(kgt.shape)],
        out_specs=[out(width), out(hq),
                   pl.BlockSpec((1, 1, hq, tm), lambda b, i: (b, i, 0, 0)),
                   out(hv)],
        out_shape=[jax.ShapeDtypeStruct((t, width), BF16),
                   jax.ShapeDtypeStruct((t, hq), BF16),
                   jax.ShapeDtypeStruct((batch, tpb, hq, tm), BF16),
                   jax.ShapeDtypeStruct((t, hv), BF16)],
        scratch_shapes=[pltpu.VMEM((tm + 8, width), F32),
                        pltpu.VMEM((8, width), F32)],
        compiler_params=_params("arbitrary", "arbitrary"),
        name="lru_mla",
    )(u, u, cw, cbias, wa, ba, wx, bx, lam,
      u, u, u, pos_row, freq_col, qn, kvn, wq, wkt, wv, qg, kgt)


def _attn_kernel(q_ref, k_ref, v_ref, o_ref, m_ref, acc_ref):
    i = pl.program_id(2)
    tq = q_ref.shape[0]
    nh = q_ref.shape[1] // MLA_QK_PAD
    nc = tq // LANES

    m_ref[...] = jnp.full_like(m_ref, -jnp.inf)
    acc_ref[...] = jnp.zeros_like(acc_ref)
    ones = jnp.ones((tq, LANES), BF16)

    def update(g, s, v):
        m_prev = m_ref[g]
        m_cur = s[:, 0:LANES]
        for c in range(1, nc):
            m_cur = jnp.maximum(m_cur, s[:, c * LANES:(c + 1) * LANES])
        m_new = jnp.maximum(m_prev, jnp.max(m_cur, axis=-1, keepdims=True))
        alpha = jnp.exp2(m_prev - m_new)
        p = jnp.concatenate(
            [jnp.exp2((s[:, c * LANES:(c + 1) * LANES] - m_new).astype(BF16)) for c in range(nc)],
            axis=1)
        pv = _dot(p, jnp.concatenate([v, ones], axis=1))
        acc_ref[g] = jnp.concatenate([alpha, alpha], axis=1) * acc_ref[g] + pv
        m_ref[g] = m_new

    def block(j, masked):
        rows = pl.ds(pl.multiple_of(j * tq, tq), tq)
        for g in range(nh):
            s = _dot(q_ref[:, g * MLA_QK_PAD:(g + 1) * MLA_QK_PAD],
                     k_ref[0, j, pl.ds(g * MLA_QK_PAD, MLA_QK_PAD), :])
            if masked:
                s = jnp.where(_row((tq, tq)) >= _lane((tq, tq)), s, -jnp.inf)
            update(g, s, v_ref[rows, pl.ds(g * MLA_V, MLA_V)])

    def body(jj, carry):
        for u in range(ATTN_KV_UNROLL):
            block(ATTN_KV_UNROLL * jj + u, False)
        return carry

    shift = ATTN_KV_UNROLL.bit_length() - 1
    ntrip = i >> shift
    lax.fori_loop(0, ntrip, body, 0)

    def rest(j, carry):
        block(j, False)
        return carry

    lax.fori_loop(ntrip << shift, i, rest, 0)

    block(i, True)
    for g in range(nh):
        acc = acc_ref[g]
        o_ref[:, pl.ds(g * MLA_V, MLA_V)] = (acc[:, :MLA_V] / acc[:, MLA_V:]).astype(o_ref.dtype)


def attention(q, k, v, batch, seq, tq):
    t = q.shape[0]
    nq = seq // tq
    nh = ATTN_HEADS_PER_STEP
    return pl.pallas_call(
        _attn_kernel,
        grid=(batch, MLA_HEADS // nh, nq),
        in_specs=[pl.BlockSpec((tq, nh * MLA_QK_PAD), lambda b, h, i: (b * nq + i, h)),
                  pl.BlockSpec((1, nq, nh * MLA_QK_PAD, tq), lambda b, h, i: (b, 0, h, 0),
                               pipeline_mode=pl.Buffered(1)),
                  pl.BlockSpec((seq, nh * MLA_V), lambda b, h, i: (b, h),
                               pipeline_mode=pl.Buffered(1))],
        out_specs=pl.BlockSpec((tq, nh * MLA_V), lambda b, h, i: (b * nq + i, h)),
        out_shape=jax.ShapeDtypeStruct((t, MLA_HEADS * MLA_V), BF16),
        scratch_shapes=[pltpu.VMEM((nh, tq, LANES), F32),
                        pltpu.VMEM((nh, tq, MLA_V + LANES), F32)],
        compiler_params=_params("arbitrary", "arbitrary", "arbitrary"),
        name="attention",
    )(q, k, v)


def _outproj_kernel(oa_ref, ob_ref, w1_ref, w2_ref, h_ref, g1_ref, nf_ref, sh_ref, sc_ref,
                    rwt_ref, hn_ref, hf_ref, lg_ref):
    y = _dot(oa_ref[...], w1_ref[...]) + _dot(ob_ref[...], w2_ref[...])
    hn = h_ref[...] + g1_ref[0] * y
    hn_ref[...] = hn
    ms = jnp.mean(hn * hn, axis=-1, keepdims=True)
    hf = hn * lax.rsqrt(ms + EPS) * nf_ref[...] * (1.0 + sc_ref[0]) + sh_ref[0]
    hf_ref[...] = hf.astype(BF16)
    r_hi, r_mid, r_lo = _split3(rwt_ref[...])
    f_hi, f_mid, _ = _split3(hf)
    ne = r_hi.shape[0]
    a = _dot_nt(jnp.concatenate([r_hi, r_mid, r_lo], axis=0), f_hi)
    b = _dot_nt(jnp.concatenate([r_hi, r_mid], axis=0), f_mid)
    lg_ref[...] = a[0:ne] + b[0:ne] + a[ne:2 * ne] + a[2 * ne:3 * ne] + b[ne:2 * ne]


def outproj(oa, ob, w1, w2, h2d, g1, nf, sh2, sc2, rwt, seq, tm):
    t, d = h2d.shape
    tpb = seq // tm
    const = lambda shape: pl.BlockSpec(shape, lambda i: tuple(0 for _ in shape))
    bvec = pl.BlockSpec((1, 1, d), lambda i: (i // tpb, 0, 0))
    return pl.pallas_call(
        _outproj_kernel,
        grid=(t // tm,),
        in_specs=[pl.BlockSpec((tm, oa.shape[1]), lambda i: (i, 0)),
                  pl.BlockSpec((tm, ob.shape[1]), lambda i: (i, 0)),
                  const(w1.shape), const(w2.shape),
                  pl.BlockSpec((tm, d), lambda i: (i, 0)),
                  bvec, const((1, d)), bvec, bvec, const(rwt.shape)],
        out_specs=[pl.BlockSpec((tm, d), lambda i: (i, 0)),
                   pl.BlockSpec((tm, d), lambda i: (i, 0)),
                   pl.BlockSpec((N_EXPERTS, tm), lambda i: (0, i))],
        out_shape=[jax.ShapeDtypeStruct((t, d), F32),
                   jax.ShapeDtypeStruct((t, d), BF16),
                   jax.ShapeDtypeStruct((N_EXPERTS, t), F32)],
        compiler_params=_params("arbitrary"),
        name="outproj",
    )(oa, ob, w1, w2, h2d, g1[:, None, :], nf.reshape(1, d), sh2[:, None, :], sc2[:, None, :], rwt)


def _route_sort_kernel(*refs):
    last = pl.program_id(0) == pl.num_programs(0) - 1

    @pl.when(last)
    def _():
        xs_ref = refs[5]
        xs_ref[...] = jnp.zeros_like(xs_ref)

    @pl.when(jnp.logical_not(last))
    def _():
        _route_sort_body(*refs)


def _route_sort_body(lg_ref, bias_ref, hf_ref, utri_ref, ltri_ref,
                     xs_ref, lpos_ref, wts_ref, cnt_ref):
    scores = jax.nn.sigmoid(lg_ref[...])
    sel = scores + bias_ref[...]
    tm = sel.shape[1]
    eidx = _row((N_EXPERTS, tm))
    neg = -jnp.inf

    best_score = None
    best_grp = None
    for g in range(N_GROUPS):
        m = [sel[g * EXPERTS_PER_GROUP + i:g * EXPERTS_PER_GROUP + i + 1, :]
             for i in range(EXPERTS_PER_GROUP)]
        gs = None
        for a in range(EXPERTS_PER_GROUP):
            for b in range(a + 1, EXPERTS_PER_GROUP):
                pair = m[a] + m[b]
                gs = pair if gs is None else jnp.maximum(gs, pair)
        if best_score is None:
            best_score, best_grp = gs, jnp.zeros_like(gs, dtype=jnp.int32)
        else:
            better = gs > best_score
            best_score = jnp.where(better, gs, best_score)
            best_grp = jnp.where(better, g, best_grp)

    masked = jnp.where((eidx >> 2) == best_grp, sel, neg)
    m1 = jnp.max(masked, axis=0, keepdims=True)
    i1 = jnp.min(jnp.where(masked == m1, eidx, N_EXPERTS), axis=0, keepdims=True)
    masked2 = jnp.where(eidx == i1, neg, masked)
    m2 = jnp.max(masked2, axis=0, keepdims=True)
    i2 = jnp.min(jnp.where(masked2 == m2, eidx, N_EXPERTS), axis=0, keepdims=True)
    pick1 = eidx == i1
    pick2 = eidx == i2
    s1 = jnp.sum(jnp.where(pick1, scores, 0.0), axis=0, keepdims=True)
    s2 = jnp.sum(jnp.where(pick2, scores, 0.0), axis=0, keepdims=True)
    tot = s1 + s2
    wts_ref[...] = jnp.concatenate([s1 / tot, s2 / tot], axis=0)

    nr = xs_ref.shape[0]
    picks = jnp.where(pick1, 1.0, jnp.where(pick2, 1.0, 0.0))
    csum = _dot(picks.astype(BF16), utri_ref[...])
    cnt = jnp.sum(picks, axis=1, keepdims=True)
    cnt_ref[0] = jnp.broadcast_to(cnt, (N_EXPERTS, LANES))
    cnt_pad = jnp.floor((cnt + (ROW_CHUNK - 1.0)) * (1.0 / ROW_CHUNK)) * ROW_CHUNK
    seg_off = _dot(ltri_ref[...], jnp.broadcast_to(cnt_pad, (N_EXPERTS, LANES)).astype(BF16))[:, 0:1]
    lposmat = seg_off + csum - 1.0
    lp1 = jnp.sum(jnp.where(pick1, lposmat, 0.0), axis=0, keepdims=True).astype(jnp.int32)
    lp2 = jnp.sum(jnp.where(pick2, lposmat, 0.0), axis=0, keepdims=True).astype(jnp.int32)
    lpos_ref[...] = jnp.concatenate([lp1, lp2], axis=0)
    rowi = _row((nr, tm))
    onehot = jnp.where(rowi == lp1, 1.0, jnp.where(rowi == lp2, 1.0, 0.0)).astype(BF16)
    xs_ref[...] = _dot(onehot, hf_ref[...]).astype(xs_ref.dtype)


def route_sort(logits_t, router_bias, hf, tl, nr):
    e, t = logits_t.shape
    d = hf.shape[1]
    ntile = t // tl
    r = jnp.arange(tl)
    utri = (r[:, None] <= r[None, :]).astype(BF16)
    re = jnp.arange(e)
    ltri = (re[None, :] < re[:, None]).astype(BF16)
    const = lambda shape: pl.BlockSpec(shape, lambda i: tuple(0 for _ in shape))
    tile = lambda i: jnp.minimum(i, ntile - 1)
    return pl.pallas_call(
        _route_sort_kernel,
        grid=(ntile + 1,),
        in_specs=[pl.BlockSpec((e, tl), lambda i: (0, tile(i))),
                  const((e, 1)),
                  pl.BlockSpec((tl, d), lambda i: (tile(i), 0)),
                  const((tl, tl)), const((e, e))],
        out_specs=[pl.BlockSpec((nr, d), lambda i: (i, 0)),
                   pl.BlockSpec((2, tl), lambda i: (0, tile(i))),
                   pl.BlockSpec((2, tl), lambda i: (0, tile(i))),
                   pl.BlockSpec((1, e, LANES), lambda i: (tile(i), 0, 0))],
        out_shape=[jax.ShapeDtypeStruct(((ntile + 1) * nr, d), BF16),
                   jax.ShapeDtypeStruct((2, t), jnp.int32),
                   jax.ShapeDtypeStruct((2, t), F32),
                   jax.ShapeDtypeStruct((t // tl, e, LANES), F32)],
        compiler_params=_params("arbitrary"),
        name="route_sort",
    )(logits_t, router_bias.reshape(e, 1).astype(F32), hf, utri, ltri)


def _moe_plan(cnt, nr, tmg, nt_max):
    ntile, ne = cnt.shape
    cpt = tmg // ROW_CHUNK
    nch = (cnt + ROW_CHUNK - 1) // ROW_CHUNK
    seg_off = jnp.cumsum(nch, axis=1) - nch
    cum = jnp.cumsum(nch, axis=0)
    tot = cum[-1]
    padded = (tot + cpt - 1) // cpt * cpt
    gend = jnp.cumsum(padded)
    gstart = gend - padded
    start = (gstart[None, :] + cum - nch).T.reshape(-1)
    end = start + nch.T.reshape(-1)
    base = (jnp.arange(ntile, dtype=jnp.int32)[:, None] * (nr // ROW_CHUNK) + seg_off).T.reshape(-1)
    c = jnp.arange(nt_max * cpt, dtype=jnp.int32)[:, None]
    hit = (c >= start[None, :]) & (c < end[None, :])
    valid = jnp.any(hit, axis=1)
    src = jnp.sum(jnp.where(hit, base[None, :] - start[None, :] + c, 0), axis=1) * ROW_CHUNK
    src = jnp.where(valid, src, nr - ROW_CHUNK)
    cflat = c[:, 0]
    spare = ntile * nr + ((cflat // cpt) % 2) * tmg + (cflat % cpt) * ROW_CHUNK
    dst = jnp.where(valid, src, spare)
    first = jnp.arange(nt_max, dtype=jnp.int32) * cpt
    e_first = jnp.sum(first[:, None] >= gend[None, :], axis=1).astype(jnp.int32)
    tile_on = (e_first < ne).astype(jnp.int32)
    return src.astype(jnp.int32), dst.astype(jnp.int32), jnp.minimum(e_first, ne - 1), tile_on


def _moe_group_kernel(src_ref, dst_ref, te_ref, on_ref, xs_hbm, wg_ref, wu_ref, wd_ref,
                      y_hbm, xbuf, ybuf, wgb, wub, wdb, in_sem, out_sem):
    del xs_hbm
    g = pl.program_id(0)
    ng = pl.num_programs(0)
    slot = lax.rem(g, 2)
    tmg = xbuf.shape[1]
    cpt = tmg // ROW_CHUNK

    def in_copy(tile, sl, k):
        row = pl.multiple_of(src_ref[tile * cpt + k], ROW_CHUNK)
        return pltpu.make_async_copy(y_hbm.at[pl.ds(row, ROW_CHUNK), :],
                                     xbuf.at[sl, pl.ds(k * ROW_CHUNK, ROW_CHUNK), :], in_sem.at[sl])

    def out_copy(tile, sl, k):
        row = pl.multiple_of(dst_ref[tile * cpt + k], ROW_CHUNK)
        return pltpu.make_async_copy(ybuf.at[sl, pl.ds(k * ROW_CHUNK, ROW_CHUNK), :],
                                     y_hbm.at[pl.ds(row, ROW_CHUNK), :], out_sem.at[sl])

    def start_in(tile, sl):
        for k in range(cpt):
            in_copy(tile, sl, k).start(priority=k % 2)

    def wait_in(tile, sl):
        for k in range(cpt):
            in_copy(tile, sl, k).wait()

    def start_out(tile, sl):
        for k in range(cpt):
            out_copy(tile, sl, k).start(priority=k % 2)

    def wait_out(tile, sl):
        for k in range(cpt):
            out_copy(tile, sl, k).wait()

    @pl.when((g >= 2) & (on_ref[jnp.maximum(g - 2, 0)] == 1))
    def _():
        wait_out(g - 2, slot)

    @pl.when((g == 0) & (on_ref[0] == 1))
    def _():
        start_in(0, 0)

    @pl.when((g + 1 < ng) & (on_ref[jnp.minimum(g + 1, ng - 1)] == 1))
    def _():
        start_in(g + 1, 1 - slot)

    @pl.when(on_ref[g] == 1)
    def _():
        @pl.when((g == 0) | (te_ref[g] != te_ref[jnp.maximum(g - 1, 0)]))
        def _():
            wgb[...] = wg_ref[0, 0].astype(BF16)
            wub[...] = wu_ref[0, 0].astype(BF16)
            wdb[...] = wd_ref[0, 0].astype(BF16)

        wait_in(g, slot)
        x = xbuf[slot]
        hid = (_silu(_dot(x, wgb[...])) * _dot(x, wub[...])).astype(BF16)
        ybuf[slot] = _dot(hid, wdb[...]).astype(ybuf.dtype)
        start_out(g, slot)

    @pl.when(g == ng - 1)
    def _():
        @pl.when((ng >= 2) & (on_ref[jnp.maximum(g - 1, 0)] == 1))
        def _():
            wait_out(g - 1, 1 - slot)

        @pl.when(on_ref[g] == 1)
        def _():
            wait_out(g, slot)


def moe_group(xs, wg, wu, wd, layer, plan, tmg, nt_max):
    src, dst, tile_e, tile_on = plan
    rows_out, d = xs.shape
    wspec = lambda w: pl.BlockSpec((1, 1) + w.shape[2:], lambda g, s, v, te, on: (layer, te[g], 0, 0))
    grid_spec = pltpu.PrefetchScalarGridSpec(
        num_scalar_prefetch=4,
        grid=(nt_max,),
        in_specs=[pl.BlockSpec(memory_space=pl.ANY), wspec(wg), wspec(wu), wspec(wd)],
        out_specs=pl.BlockSpec(memory_space=pl.ANY),
        scratch_shapes=[pltpu.VMEM((2, tmg, d), BF16), pltpu.VMEM((2, tmg, d), BF16),
                        pltpu.VMEM(wg.shape[2:], BF16), pltpu.VMEM(wu.shape[2:], BF16),
                        pltpu.VMEM(wd.shape[2:], BF16),
                        pltpu.SemaphoreType.DMA((2,)), pltpu.SemaphoreType.DMA((2,))])
    return pl.pallas_call(
        _moe_group_kernel,
        grid_spec=grid_spec,
        out_shape=jax.ShapeDtypeStruct((rows_out, d), BF16),
        input_output_aliases={4: 0},
        compiler_params=_params("arbitrary"),
        name="moe_group",
    )(src, dst, tile_e, tile_on, xs, wg, wu, wd)


def _moe_combine_kernel(y_ref, lpos_ref, wts_ref, h_ref, g2_ref, o_ref):
    nr = y_ref.shape[0]
    tl = h_ref.shape[0]
    rowi = _row((nr, tl))
    wc = (jnp.where(rowi == lpos_ref[0:1, :], wts_ref[0:1, :], 0.0)
          + jnp.where(rowi == lpos_ref[1:2, :], wts_ref[1:2, :], 0.0)).astype(BF16)
    o_ref[...] = h_ref[...] + g2_ref[0] * _dot_tn(wc, y_ref[...])


def moe_combine(y, lpos, wts, h2d, g2, seq, tl, nr):
    t, d = h2d.shape
    tpb = seq // tl
    return pl.pallas_call(
        _moe_combine_kernel,
        grid=(t // tl,),
        in_specs=[pl.BlockSpec((nr, d), lambda i: (i, 0)),
                  pl.BlockSpec((2, tl), lambda i: (0, i)),
                  pl.BlockSpec((2, tl), lambda i: (0, i)),
                  pl.BlockSpec((tl, d), lambda i: (i, 0)),
                  pl.BlockSpec((1, 1, d), lambda i: (i // tpb, 0, 0))],
        out_specs=pl.BlockSpec((tl, d), lambda i: (i, 0)),
        out_shape=jax.ShapeDtypeStruct((t, d), F32),
        compiler_params=_params("arbitrary"),
        name="moe_combine",
    )(y, lpos, wts, h2d, g2[:, None, :])


def _pad_heads(w, heads, dh):
    lead = w.shape[:-1]
    w = w.reshape(lead + (heads, dh))
    w = jnp.pad(w, [(0, 0)] * len(lead) + [(0, 0), (0, LANES - dh)])
    return w.reshape(lead + (heads * LANES,))


def _pad_cols(w, n):
    return jnp.pad(w, [(0, 0)] * (w.ndim - 1) + [(0, n - w.shape[-1])])


def _tri_blocks(tm):
    r = jnp.arange(tm)
    return ((r[:, None] >= r[None, :]) & (r[:, None] // CHUNK == r[None, :] // CHUNK)).astype(BF16)


def _even_layout(w_in):
    sizes = (GLA_HEADS * GLA_DK, GLA_HEADS * GLA_DK, GLA_HEADS * GLA_DV, GLA_GATE_RANK,
             GLA_HEADS * GLA_DV, SSD_INNER, SSD_INNER + 2 * SSD_GROUPS * SSD_STATE, SSD_HEADS)
    offs = [0]
    for s in sizes:
        offs.append(offs[-1] + s)
    seg = lambda i: w_in[:, offs[i]:offs[i + 1]]
    q, k, v, glr, og, z, xbc, dt = (seg(i) for i in range(8))
    xs, bc = xbc[:, :SSD_INNER], xbc[:, SSD_INNER:]
    cols = [xs, z, v, og, _pad_heads(q, GLA_HEADS, GLA_DK), _pad_heads(k, GLA_HEADS, GLA_DK), bc,
            _pad_cols(glr, LANES), _pad_cols(dt[:, 0::2], LANES), _pad_cols(dt[:, 1::2], LANES)]
    w = jnp.concatenate(cols, axis=1).astype(BF16)
    cb = {"xs": 0, "z": 1, "v": 4, "og": 5, "q": 6, "k": 7, "bc": 8, "glr": 36, "dte": 37, "dto": 38}
    return w, cb


def _odd_layout(w_in):
    d = w_in.shape[0]
    o = [0, d, 2 * d, 2 * d + MLA_Q_LORA, 2 * d + MLA_Q_LORA + MLA_KV_LORA,
         2 * d + MLA_Q_LORA + MLA_KV_LORA + MLA_ROPE]
    gate, xr, uq, ukv, kr = (w_in[:, o[i]:o[i + 1]] for i in range(5))
    half = MLA_ROPE // 2
    kr_sw = jnp.concatenate([kr[:, half:], kr[:, :half]], axis=1)
    w = jnp.concatenate([gate, xr, ukv, uq, kr, kr_sw], axis=1).astype(BF16)
    cb = {"gate": 0, "xr": 1, "ukv": 8, "uq": 6, "krr": 21}
    return w, cb


def _swap_halves(x):
    half = x.shape[-1] // 2
    return jnp.concatenate([x[..., half:], x[..., :half]], axis=-1)


def _block_diag(w, per):
    nb, bw, _ = w.shape
    w = w.reshape(nb // per, per, bw, bw)
    eye = jnp.eye(per, dtype=w.dtype)
    out = jnp.einsum("gpij,pq->gpiqj", w, eye)
    return out.reshape(nb // per, per * bw, per * bw)


def kernel(x, c, positions, router_w, router_bias, ada_w, ada_b, norm_mix, norm_ffn, moe_w_gate, moe_w_up, moe_w_down, ev_w_in, gla_w_g2, gla_b_g2, gla_onorm, ssd_conv_w, ssd_conv_b, ssd_dt_bias, ssd_a_log, ssd_d, ssd_norm, ev_w_out, od_w_in, lru_conv_w, lru_conv_b, lru_w_a, lru_b_a, lru_w_x, lru_b_x, lru_lambda, mla_q_norm, mla_w_q_up, mla_kv_norm, mla_w_kv_up, mla_q_qknorm, mla_k_qknorm, od_w_out):
    batch, seq, d = x.shape
    t = batch * seq
    depth = ada_w.shape[0]
    tm_seq = min(TM_SEQ, seq)
    tm_mm = min(TM_MM, seq)
    tq = min(TQ_ATTN, seq)
    tl = min(TL_MOE, seq)
    nr = 2 * tl + N_EXPERTS * ROW_CHUNK
    tmg = TM_MOE
    chunks_max = 2 * t // ROW_CHUNK + (t // tl) * N_EXPERTS + N_EXPERTS * (tmg // ROW_CHUNK - 1)
    nt_max = -(-chunks_max // (tmg // ROW_CHUNK))

    mod = ada_mod(c, ada_w, ada_b)
    rwt = router_w.T.astype(F32)
    tri = _tri_blocks(tm_seq)
    h = x.reshape(t, d)

    for layer in range(depth):
        sh1, sc1, g1, sh2, sc2, g2 = (mod[layer, :, i * d:(i + 1) * d] for i in range(6))
        i = layer // 2
        if layer % 2 == 0:
            w_in, cb = _even_layout(ev_w_in[i])
            u = inproj(h, norm_mix[layer], sh1, sc1, w_in, seq, tm_mm, w_in.shape[1] // 3)
            wg = jnp.pad(_pad_heads(gla_w_g2[i], GLA_HEADS, GLA_DK),
                         ((0, LANES - GLA_GATE_RANK), (0, 0))).astype(BF16)
            bg = _pad_heads(gla_b_g2[i][None, :], GLA_HEADS, GLA_DK)
            cw, cbias = ssd_conv_w[i], ssd_conv_b[i][None, :]
            perm = lambda v: jnp.stack([_pad_cols(v[0::2], LANES), _pad_cols(v[1::2], LANES)])
            npair = SSD_HEADS // 2
            hp = jnp.arange(SSD_INNER) // SSD_HEADDIM
            ex_e = (jnp.arange(LANES)[:, None] * 2 == hp[None, :]) & (jnp.arange(LANES)[:, None] < npair)
            ex_o = (jnp.arange(LANES)[:, None] * 2 + 1 == hp[None, :]) & (jnp.arange(LANES)[:, None] < npair)
            expand = jnp.stack([ex_e, ex_o]).astype(BF16)
            oa, ob = gla_ssd(u, cb, wg, bg, gla_onorm[i][None, :],
                             cw[:, :SSD_INNER], cbias[:, :SSD_INNER], cw[:, SSD_INNER:],
                             cbias[:, SSD_INNER:], perm(ssd_dt_bias[i]), perm(ssd_a_log[i]),
                             jnp.repeat(ssd_d[i], SSD_HEADDIM)[None, :], ssd_norm[i][None, :],
                             expand, tri, batch, seq, tm_seq)
            w_out = ev_w_out[i].astype(BF16)
            w1, w2 = w_out[:GLA_HEADS * GLA_DV], w_out[GLA_HEADS * GLA_DV:]
        else:
            w_in, cb = _odd_layout(od_w_in[i])
            u = inproj(h, norm_mix[layer], sh1, sc1, w_in, seq, tm_mm, w_in.shape[1] // 2)
            per = 4
            wq =mla_w_q_up[i].reshape(MLA_Q_LORA, MLA_HEADS, MLA_QK)
            wq = jnp.concatenate([wq, _swap_halves(wq[..., MLA_NOPE:])], axis=-1)
            wq = wq.reshape(MLA_Q_LORA, MLA_HEADS * MLA_QK_PAD).astype(BF16)
            wkv = mla_w_kv_up[i].reshape(MLA_KV_LORA, MLA_HEADS, MLA_NOPE + MLA_V)
            wkt = wkv[..., :MLA_NOPE].reshape(MLA_KV_LORA, -1).T.astype(BF16)
            wv = wkv[..., MLA_NOPE:].reshape(MLA_KV_LORA, -1).astype(BF16)
            ext_gain = lambda gq: jnp.concatenate([gq, _swap_halves(gq[MLA_NOPE:])])
            fr = ROPE_THETA ** (-jnp.arange(0, MLA_ROPE, 2, dtype=F32) / MLA_ROPE)
            lanes_rep = lambda col: jnp.broadcast_to(col[:, None], (col.shape[0], LANES))
            assert tm_mm == tq
            oa, q, kt, v = lru_mla(
                u, cb, lru_conv_w[i], lru_conv_b[i][None, :],
                _block_diag(lru_w_a[i], per).astype(BF16), lru_b_a[i][None, :],
                _block_diag(lru_w_x[i], per).astype(BF16), lru_b_x[i][None, :],
                lru_lambda[i][None, :],
                positions.reshape(1, t), lanes_rep(fr), mla_q_norm[i][None, :],
                mla_kv_norm[i][None, :], wq, wkt, wv, ext_gain(mla_q_qknorm[i])[None, :],
                lanes_rep(ext_gain(mla_k_qknorm[i])), batch, seq, tm_mm)
            ob = attention(q, kt, v, batch, seq, tq)
            w_out = od_w_out[i].astype(BF16)
            w1, w2 = w_out[:d], w_out[d:]
        h, hf, logits_t = outproj(oa, ob, w1, w2, h, g1, norm_ffn[layer], sh2, sc2, rwt, seq, tm_mm)
        assert 2 * tmg <= nr
        xs, lpos, wts, cnt = route_sort(logits_t, router_bias, hf, tl, nr)
        plan = _moe_plan(cnt[:, :, 0].astype(jnp.int32), nr, tmg, nt_max)
        y = moe_group(xs, moe_w_gate, moe_w_up, moe_w_down, layer, plan, tmg, nt_max)
        h = moe_combine(y, lpos, wts, h, g2, seq, tl, nr)
    return h.reshape(batch, seq, d)
```
